```python
import math
import jax, jax.numpy as jnp
from jax import lax
import numpy as np

D_MODEL = 1024
BATCH = 8
SEQ = 2048
DEPTH = 1
DEC_BATCH = 128
DEC_SEQ = 1
PAST_LEN = 16384
PAGE_SIZE = 128

MIX_WIDTH = D_MODEL
W_A = MIX_WIDTH // 2
W_B = MIX_WIDTH - W_A
S5_H = 16
S5_GROUPS = W_A // S5_H
S5_STATE = 64
POOL_WINDOWS = (2, 4, 8, 16)
POOL_GROUPS = len(POOL_WINDOWS)
POOL_CH = W_B // POOL_GROUPS
POOL_BUF = max(POOL_WINDOWS) - 1
D_FF = 4 * D_MODEL
DT_MIN = 0.001
DT_MAX = 0.1
EPS = 1e-6

kernel_name = "s5_pool_parallel_hybrid_step"


def rms_norm(x, g):
    xf = x.astype(jnp.float32)
    y = xf * lax.rsqrt(jnp.mean(xf * xf, axis=-1, keepdims=True) + EPS)
    return (y * g.astype(jnp.float32)).astype(x.dtype)


def _scan_combine(left, right):
    a_l, b_l = left
    a_r, b_r = right
    return a_r * a_l, a_r * b_l + b_r


def s5_mixer(u, h0_re, h0_im, lam_re, lam_im, log_dt, b_re, b_im, c_re, c_im, d, w_glu):
    f32 = jnp.float32
    bsz, L, _ = u.shape
    lam = lax.complex(lam_re.astype(f32), lam_im.astype(f32))
    dt = jnp.exp(log_dt.astype(f32))[:, None]
    lam_bar = jnp.exp(lam * dt)
    b_mat = lax.complex(b_re.astype(f32), b_im.astype(f32))
    b_bar = ((lam_bar - 1.0) / lam)[..., None] * b_mat
    uf = u.astype(f32).reshape(bsz, L, S5_GROUPS, S5_H)
    bu = jnp.einsum('gph,blgh->blgp', b_bar, uf.astype(jnp.complex64))
    h0 = lax.complex(h0_re.astype(f32), h0_im.astype(f32))
    bu = bu.at[:, 0].add(lam_bar[None] * h0)
    a = jnp.broadcast_to(lam_bar, bu.shape)
    _, hs = lax.associative_scan(_scan_combine, (a, bu), axis=1)
    c_mat = lax.complex(c_re.astype(f32), c_im.astype(f32))
    y = jnp.real(jnp.einsum('ghp,blgp->blgh', c_mat, hs))
    y = y + d.astype(f32).reshape(S5_GROUPS, S5_H) * uf
    y = jax.nn.gelu(y)
    y = y * jax.nn.sigmoid(jnp.einsum('blgh,ghk->blgk', y, w_glu.astype(f32)))
    h_last = hs[:, -1]
    return (y.reshape(bsz, L, W_A).astype(u.dtype),
            jnp.real(h_last).astype(h0_re.dtype),
            jnp.imag(h_last).astype(h0_im.dtype))


def pool_mixer(u, prefix, start_pos, pool_w, pool_scale):
    f32 = jnp.float32
    bsz, L, _ = u.shape
    u_ext = jnp.concatenate([prefix.astype(u.dtype), u], axis=1)
    uf = u.astype(f32)
    cs = jnp.cumsum(u_ext.astype(f32), axis=1)
    cs = jnp.concatenate([jnp.zeros((bsz, 1, W_B), f32), cs], axis=1)
    pos = start_pos + jnp.arange(L)
    off = POOL_BUF + 1
    outs = []
    for gi, w in enumerate(POOL_WINDOWS):
        sl = slice(gi * POOL_CH, (gi + 1) * POOL_CH)
        win = cs[:, off:off + L, sl] - cs[:, off - w:off - w + L, sl]
        count = jnp.minimum(pos + 1, w).astype(f32)[None, :, None]
        pooled = win / count - uf[..., sl]
        outs.append(jnp.einsum('blc,cd->bld', pooled, pool_w[gi].astype(f32)))
    y = jnp.concatenate(outs, axis=-1) * pool_scale.astype(f32)
    new_buf = u_ext[:, -POOL_BUF:]
    return y.astype(u.dtype), new_buf.astype(prefix.dtype)


def layer(x, h0_re, h0_im, pool_prefix, start_pos,
          norm_mix_pre, norm_mix_post, norm_mlp_pre, norm_mlp_post,
          w_in, s5_lambda_re, s5_lambda_im, s5_log_dt, s5_b_re, s5_b_im,
          s5_c_re, s5_c_im, s5_d, s5_w_glu, pool_w, pool_scale, w_out,
          w_mlp_up, w_mlp_down):
    xn = rms_norm(x, norm_mix_pre)
    u = jnp.einsum('bld,dm->blm', xn, w_in)
    y_a, h_re, h_im = s5_mixer(u[..., :W_A], h0_re, h0_im, s5_lambda_re, s5_lambda_im,
                               s5_log_dt, s5_b_re, s5_b_im, s5_c_re, s5_c_im, s5_d, s5_w_glu)
    y_b, buf = pool_mixer(u[..., W_A:], pool_prefix, start_pos, pool_w, pool_scale)
    mix = jnp.einsum('blm,md->bld', jnp.concatenate([y_a, y_b], axis=-1), w_out)
    h = x + rms_norm(mix, norm_mix_post)
    hn = rms_norm(h, norm_mlp_pre)
    ff = jnp.square(jax.nn.relu(jnp.einsum('bld,df->blf', hn, w_mlp_up)))
    ff = jnp.einsum('blf,fd->bld', ff, w_mlp_down)
    out = h + rms_norm(ff, norm_mlp_post)
    return out, h_re, h_im, buf


def setup_inputs(seed: int = 0) -> dict:
    key = jax.random.key(seed)
    ks = jax.random.split(key, 24)
    f32 = jnp.float32
    n = jnp.arange(S5_STATE, dtype=f32)
    lam_re = -0.5 + 0.01 * jax.random.normal(ks[5], (S5_GROUPS, S5_STATE), f32)
    lam_im = math.pi * n[None, :] + 0.01 * jax.random.normal(ks[6], (S5_GROUPS, S5_STATE), f32)
    log_dt = jax.random.uniform(ks[7], (S5_GROUPS,), f32, math.log(DT_MIN), math.log(DT_MAX))
    b_scale = (2.0 * S5_H) ** -0.5
    c_scale = (2.0 * S5_STATE) ** -0.5
    return {
        "x_prompt": jax.random.normal(ks[0], (BATCH, SEQ, D_MODEL), f32),
        "x_sample": jax.random.normal(ks[1], (DEC_BATCH, DEC_SEQ, D_MODEL), f32),
        "state_s5_re": 0.5 * jax.random.normal(ks[2], (DEC_BATCH, S5_GROUPS, S5_STATE), f32),
        "state_s5_im": 0.5 * jax.random.normal(ks[3], (DEC_BATCH, S5_GROUPS, S5_STATE), f32),
        "state_pool": jax.random.normal(ks[4], (DEC_BATCH, POOL_BUF, W_B), f32),
        "norm_mix_pre": 1.0 + 0.05 * jax.random.normal(ks[8], (D_MODEL,), f32),
        "norm_mix_post": 1.0 + 0.05 * jax.random.normal(ks[9], (D_MODEL,), f32),
        "norm_mlp_pre": 1.0 + 0.05 * jax.random.normal(ks[10], (D_MODEL,), f32),
        "norm_mlp_post": 1.0 + 0.05 * jax.random.normal(ks[11], (D_MODEL,), f32),
        "w_in": jax.random.normal(ks[12], (D_MODEL, MIX_WIDTH), f32) * D_MODEL ** -0.5,
        "s5_lambda_re": lam_re,
        "s5_lambda_im": lam_im,
        "s5_log_dt": log_dt,
        "s5_b_re": jax.random.normal(ks[13], (S5_GROUPS, S5_STATE, S5_H), f32) * b_scale,
        "s5_b_im": jax.random.normal(ks[14], (S5_GROUPS, S5_STATE, S5_H), f32) * b_scale,
        "s5_c_re": jax.random.normal(ks[15], (S5_GROUPS, S5_H, S5_STATE), f32) * c_scale,
        "s5_c_im": jax.random.normal(ks[16], (S5_GROUPS, S5_H, S5_STATE), f32) * c_scale,
        "s5_d": jax.random.normal(ks[17], (W_A,), f32),
        "s5_w_glu": jax.random.normal(ks[18], (S5_GROUPS, S5_H, S5_H), f32) * S5_H ** -0.5,
        "pool_w": jax.random.normal(ks[19], (POOL_GROUPS, POOL_CH, POOL_CH), f32) * POOL_CH ** -0.5,
        "pool_scale": 1.0 + 0.1 * jax.random.normal(ks[20], (W_B,), f32),
        "w_out": jax.random.normal(ks[21], (MIX_WIDTH, D_MODEL), f32) * MIX_WIDTH ** -0.5,
        "w_mlp_up": jax.random.normal(ks[22], (D_MODEL, D_FF), f32) * D_MODEL ** -0.5,
        "w_mlp_down": jax.random.normal(ks[23], (D_FF, D_MODEL), f32) * D_FF ** -0.5,
    }


def reference(x_prompt, x_sample, state_s5_re, state_s5_im, state_pool,
              norm_mix_pre, norm_mix_post, norm_mlp_pre, norm_mlp_post,
              w_in, s5_lambda_re, s5_lambda_im, s5_log_dt, s5_b_re, s5_b_im,
              s5_c_re, s5_c_im, s5_d, s5_w_glu, pool_w, pool_scale, w_out,
              w_mlp_up, w_mlp_down):
    weights = (norm_mix_pre, norm_mix_post, norm_mlp_pre, norm_mlp_post,
               w_in, s5_lambda_re, s5_lambda_im, s5_log_dt, s5_b_re, s5_b_im,
               s5_c_re, s5_c_im, s5_d, s5_w_glu, pool_w, pool_scale, w_out,
               w_mlp_up, w_mlp_down)
    hp_re = jnp.zeros((x_prompt.shape[0], S5_GROUPS, S5_STATE), state_s5_re.dtype)
    hp_im = jnp.zeros((x_prompt.shape[0], S5_GROUPS, S5_STATE), state_s5_im.dtype)
    pp = jnp.zeros((x_prompt.shape[0], POOL_BUF, W_B), state_pool.dtype)
    hs_re, hs_im, ps = state_s5_re, state_s5_im, state_pool
    y_p, y_s = x_prompt, x_sample
    for _ in range(DEPTH):
        y_p, hp_re, hp_im, pp = layer(y_p, hp_re, hp_im, pp, 0, *weights)
        y_s, hs_re, hs_im, ps = layer(y_s, hs_re, hs_im, ps, PAST_LEN, *weights)
    return (y_p, y_s, hp_re, hp_im, pp, hs_re, hs_im, ps)
```

```python
import functools
import math

import jax
import jax.numpy as jnp
from jax import lax
from jax.experimental import pallas as pl
from jax.experimental.pallas import tpu as pltpu

F32 = jnp.float32
BF16 = jnp.bfloat16

D_MODEL = 1024
W_A = 512
W_B = 512
S5_H = 16
S5_GROUPS = 32
S5_STATE = 64
N_STATE = S5_GROUPS * S5_STATE
N_SLAB = 4
SLAB_GROUPS = S5_GROUPS // N_SLAB
SLAB_U = SLAB_GROUPS * S5_H
SLAB_S = SLAB_GROUPS * S5_STATE
POOL_WINDOWS = (2, 4, 8, 16)
POOL_CH = 128
POOL_HIST = 16
D_FF = 4096
FF_CHUNK = 1024
EPS = 1e-6
PAST_LEN = 16384
GELU_C = math.sqrt(2.0 / math.pi)

VMEM_LIMIT_BYTES = 56 * 1024 * 1024


def _rms_norm(x, g):
    ms = jnp.mean(x * x, axis=-1, keepdims=True)
    return x * lax.rsqrt(ms + EPS) * g


def _full_spec(shape):
    return pl.BlockSpec(shape, lambda *_: (0,) * len(shape))


def _s5_prep_kernel(lam_re_ref, lam_im_ref, log_dt_ref, bt_re_ref, bt_im_ref,
                    ct_re_ref, ct_im_ref,
                    a_re_ref, a_im_ref, bm_ref, cmt_ref):
    lam_re = lam_re_ref[...]
    lam_im = lam_im_ref[...]
    dt = jnp.exp(log_dt_ref[...])
    mag = jnp.exp(lam_re * dt)
    ang = lam_im * dt
    a_re = mag * jnp.cos(ang)
    a_im = mag * jnp.sin(ang)
    a_re_ref[...] = a_re
    a_im_ref[...] = a_im
    n_re = a_re - 1.0
    n_im = a_im
    den = lam_re * lam_re + lam_im * lam_im
    k_re = (n_re * lam_re + n_im * lam_im) / den
    k_im = (n_im * lam_re - n_re * lam_im) / den
    bt_re = bt_re_ref[...]
    bt_im = bt_im_ref[...]
    bb_re = k_re * bt_re - k_im * bt_im
    bb_im = k_re * bt_im + k_im * bt_re
    ct_re = ct_re_ref[...]
    ct_im_neg = -ct_im_ref[...]

    rows = lax.broadcasted_iota(jnp.int32, (SLAB_U, SLAB_S), 0) // S5_H
    cols = lax.broadcasted_iota(jnp.int32, (SLAB_U, SLAB_S), 1) // S5_STATE
    diag = rows == cols

    def block_diag(m, j):
        sl = m[:, j * SLAB_S:(j + 1) * SLAB_S]
        tiled = jnp.concatenate([sl] * SLAB_GROUPS, axis=0)
        return jnp.where(diag, tiled, 0.0)

    for j in range(N_SLAB):
        bm_ref[j, :, :SLAB_S] = block_diag(bb_re, j).astype(BF16)
        bm_ref[j, :, SLAB_S:] = block_diag(bb_im, j).astype(BF16)
        cmt_ref[j, :, :SLAB_S] = block_diag(ct_re, j).astype(BF16)
        cmt_ref[j, :, SLAB_S:] = block_diag(ct_im_neg, j).astype(BF16)


def _s5_prep(lam_re, lam_im, log_dt, b_re, b_im, c_re, c_im):
    row = lambda a: a.reshape(1, N_STATE)
    bt = lambda b: jnp.transpose(b, (2, 0, 1)).reshape(S5_H, N_STATE)
    ct = lambda c: jnp.transpose(c, (1, 0, 2)).reshape(S5_H, N_STATE)
    ins = (row(lam_re), row(lam_im),
           row(jnp.broadcast_to(log_dt[:, None], (S5_GROUPS, S5_STATE))),
           bt(b_re), bt(b_im), ct(c_re), ct(c_im))
    out_shape = (
        jax.ShapeDtypeStruct((1, N_STATE), F32),
        jax.ShapeDtypeStruct((1, N_STATE), F32),
        jax.ShapeDtypeStruct((N_SLAB, SLAB_U, 2 * SLAB_S), BF16),
        jax.ShapeDtypeStruct((N_SLAB, SLAB_U, 2 * SLAB_S), BF16),
    )
    return pl.pallas_call(
        _s5_prep_kernel,
        out_shape=out_shape,
        name="s5_prep",
    )(*ins)


def _mixer_kernel(x_ref, h0r_ref, h0i_ref, hist0_ref, g_pre_ref, g_post_ref,
                  win_ref, a_re_ref, a_im_ref, bm_ref, cmt_ref, d_ref, glu_ref,
                  poolw_ref, pscale_ref, wout_ref,
                  h1_ref, hr_out_ref, hi_out_ref, hist_out_ref,
                  slab_ref, hr_s, hi_s, hist_s,
                  *, batch, steps, start_pos):
    i = pl.program_id(0)
    rows = batch * steps

    @pl.when(i == 0)
    def _():
        hr_s[...] = h0r_ref[...]
        hi_s[...] = h0i_ref[...]
        hist_s[...] = hist0_ref[...]

    x = x_ref[...]
    xn = _rms_norm(x, g_pre_ref[...]).astype(BF16)
    u = jnp.dot(xn, win_ref[...], preferred_element_type=F32)
    ua = u[:, :W_A]
    ub = u[:, W_A:]
    ua_bf = ua.astype(BF16)

    ys = []
    for j in range(N_SLAB):
        st = slice(j * SLAB_S, (j + 1) * SLAB_S)
        slab_ref[...] = jnp.dot(ua_bf[:, j * SLAB_U:(j + 1) * SLAB_U], bm_ref[j],
                                preferred_element_type=F32)
        ar = jnp.broadcast_to(a_re_ref[:, st], (batch, SLAB_S))
        ai = jnp.broadcast_to(a_im_ref[:, st], (batch, SLAB_S))

        def step(t, carry):
            hr, hi = carry
            r0 = pl.multiple_of(t * batch, batch)
            bur = slab_ref[pl.ds(r0, batch), :SLAB_S]
            bui = slab_ref[pl.ds(r0, batch), SLAB_S:]
            nr = ar * hr - ai * hi + bur
            ni = ar * hi + ai * hr + bui
            slab_ref[pl.ds(r0, batch), :SLAB_S] = nr
            slab_ref[pl.ds(r0, batch), SLAB_S:] = ni
            return nr, ni

        hr, hi = lax.fori_loop(0, steps, step, (hr_s[:, st], hi_s[:, st]),
                               unroll=min(steps, 8))
        hr_s[:, st] = hr
        hi_s[:, st] = hi
        ys.append(lax.dot_general(slab_ref[...].astype(BF16), cmt_ref[j],
                                  (((1,), (1,)), ((), ())),
                                  preferred_element_type=F32))
    hr_out_ref[...] = hr_s[...]
    hi_out_ref[...] = hi_s[...]

    y = jnp.concatenate(ys, axis=1) + d_ref[...] * ua
    y = y * (0.5 * (1.0 + jnp.tanh(GELU_C * (y + 0.044715 * (y * y * y)))))
    y_bf = y.astype(BF16)
    half = W_A // 2
    gate = jnp.concatenate(
        [jnp.dot(y_bf[:, :half], glu_ref[0], preferred_element_type=F32),
         jnp.dot(y_bf[:, half:], glu_ref[1], preferred_element_type=F32)], axis=1)
    ya = y * (1.0 / (1.0 + jnp.exp(-gate)))

    ext = jnp.concatenate([hist_s[...], ub], axis=0)
    n_ext = POOL_HIST * batch + rows
    hist_s[...] = ext[n_ext - POOL_HIST * batch:, :]
    hist_out_ref[...] = ext[n_ext - POOL_HIST * batch:, :]
    t_loc = lax.broadcasted_iota(jnp.int32, (rows, 1), 0) // batch
    pos1 = t_loc + (start_pos + 1) + i * steps
    pooled = []
    for gi, w in enumerate(POOL_WINDOWS):
        s = ext[:, gi * POOL_CH:(gi + 1) * POOL_CH]
        span = 1
        while span < w:
            n = s.shape[0]
            s = s[span * batch:, :] + s[:n - span * batch, :]
            span *= 2
        win = s[s.shape[0] - rows:, :]
        count = jnp.minimum(pos1, w).astype(F32)
        pooled.append(win / count - ub[:, gi * POOL_CH:(gi + 1) * POOL_CH])
    pooled = jnp.concatenate(pooled, axis=1).astype(BF16)
    halfb = W_B // 2
    yb = jnp.concatenate(
        [jnp.dot(pooled[:, :halfb], poolw_ref[0], preferred_element_type=F32),
         jnp.dot(pooled[:, halfb:], poolw_ref[1], preferred_element_type=F32)], axis=1)
    yb = yb * pscale_ref[...]

    ycat = jnp.concatenate([ya, yb], axis=1).astype(BF16)
    mix = jnp.dot(ycat, wout_ref[...], preferred_element_type=F32)
    h1_ref[...] = x + _rms_norm(mix, g_post_ref[...])


def _mixer(x_tb, h0r, h0i, hist0, g_pre, g_post, win, a_re, a_im, bm, cmt, d,
           glu, poolw, pscale, wout, *, batch, steps, start_pos):
    n_rows = x_tb.shape[0]
    rows = batch * steps
    n_chunks = n_rows // rows
    assert n_chunks * rows == n_rows
    kern = functools.partial(_mixer_kernel, batch=batch, steps=steps,
                             start_pos=start_pos)
    small = [h0r, h0i, hist0, g_pre, g_post, win, a_re, a_im, bm, cmt, d, glu,
             poolw, pscale, wout]
    out_shape = (
        jax.ShapeDtypeStruct((n_rows, D_MODEL), F32),
        jax.ShapeDtypeStruct((batch, N_STATE), F32),
        jax.ShapeDtypeStruct((batch, N_STATE), F32),
        jax.ShapeDtypeStruct((POOL_HIST * batch, W_B), F32),
    )
    return pl.pallas_call(
        kern,
        out_shape=out_shape,
        grid=(n_chunks,),
        in_specs=[pl.BlockSpec((rows, D_MODEL), lambda i: (i, 0))]
        + [_full_spec(a.shape) for a in small],
        out_specs=(
            pl.BlockSpec((rows, D_MODEL), lambda i: (i, 0)),
            _full_spec((batch, N_STATE)),
            _full_spec((batch, N_STATE)),
            _full_spec((POOL_HIST * batch, W_B)),
        ),
        scratch_shapes=[
            pltpu.VMEM((rows, 2 * SLAB_S), F32),
            pltpu.VMEM((batch, N_STATE), F32),
            pltpu.VMEM((batch, N_STATE), F32),
            pltpu.VMEM((POOL_HIST * batch, W_B), F32),
        ],
        compiler_params=pltpu.CompilerParams(
            dimension_semantics=("arbitrary",),
            vmem_limit_bytes=VMEM_LIMIT_BYTES),
        name="mixer",
    )(x_tb, *small)


def _mlp_kernel(h_ref, g_pre_ref, g_post_ref, wup_ref, wdown_ref, o_ref):
    h = h_ref[...]
    hn = _rms_norm(h, g_pre_ref[...]).astype(BF16)
    acc = None
    for j in range(D_FF // FF_CHUNK):
        sl = slice(j * FF_CHUNK, (j + 1) * FF_CHUNK)
        up = jnp.dot(hn, wup_ref[:, sl], preferred_element_type=F32)
        up = jnp.maximum(up, 0.0)
        ff = (up * up).astype(BF16)
        part = jnp.dot(ff, wdown_ref[sl, :], preferred_element_type=F32)
        acc = part if acc is None else acc + part
    o_ref[...] = h + _rms_norm(acc, g_post_ref[...])


def _mlp(h, g_pre, g_post, wup, wdown, *, block_rows):
    n_rows = h.shape[0]
    assert n_rows % block_rows == 0
    return pl.pallas_call(
        _mlp_kernel,
        out_shape=jax.ShapeDtypeStruct((n_rows, D_MODEL), F32),
        grid=(n_rows // block_rows,),
        in_specs=[
            pl.BlockSpec((block_rows, D_MODEL), lambda i: (i, 0)),
            _full_spec(g_pre.shape),
            _full_spec(g_post.shape),
            pl.BlockSpec(memory_space=pltpu.VMEM),
            pl.BlockSpec(memory_space=pltpu.VMEM),
        ],
        out_specs=pl.BlockSpec((block_rows, D_MODEL), lambda i: (i, 0)),
        compiler_params=pltpu.CompilerParams(
            dimension_semantics=("arbitrary",),
            vmem_limit_bytes=VMEM_LIMIT_BYTES),
        name="mlp",
    )(h, g_pre, g_post, wup, wdown)


def _block_diag_pairs(w):
    n2, k, _ = w.shape
    w = w.reshape(n2 // 2, 2, k, k)
    z = jnp.zeros_like(w[:, 0])
    top = jnp.concatenate([w[:, 0], z], axis=2)
    bot = jnp.concatenate([z, w[:, 1]], axis=2)
    return jnp.concatenate([top, bot], axis=1)


def _glu_block_diag(w_glu):
    w = w_glu.reshape(2, 16, S5_H, S5_H)
    eye = jnp.eye(16, dtype=w.dtype)
    m = w[:, :, :, None, :] * eye[None, :, None, :, None]
    return m.reshape(2, 16 * S5_H, 16 * S5_H)


def kernel(x_prompt, x_sample, state_s5_re, state_s5_im, state_pool, norm_mix_pre, norm_mix_post, norm_mlp_pre, norm_mlp_post, w_in, s5_lambda_re, s5_lambda_im, s5_log_dt, s5_b_re, s5_b_im, s5_c_re, s5_c_im, s5_d, s5_w_glu, pool_w, pool_scale, w_out, w_mlp_up, w_mlp_down):
    bp, seq, _ = x_prompt.shape
    bs = x_sample.shape[0]

    a_re, a_im, bm, cmt = _s5_prep(s5_lambda_re, s5_lambda_im, s5_log_dt,
                                   s5_b_re, s5_b_im, s5_c_re, s5_c_im)
    row = lambda v: v.reshape(1, -1)
    shared = dict(
        g_pre=row(norm_mix_pre), g_post=row(norm_mix_post),
        win=w_in.astype(BF16), a_re=a_re, a_im=a_im, bm=bm, cmt=cmt,
        d=row(s5_d), glu=_glu_block_diag(s5_w_glu).astype(BF16),
        poolw=_block_diag_pairs(pool_w).astype(BF16), pscale=row(pool_scale),
        wout=w_out.astype(BF16))
    mlp_w = (row(norm_mlp_pre), row(norm_mlp_post),
             w_mlp_up.astype(BF16), w_mlp_down.astype(BF16))

    xp = jnp.transpose(x_prompt, (1, 0, 2)).reshape(seq * bp, D_MODEL)
    zeros_state = jnp.zeros((bp, N_STATE), F32)
    h1p, hpr, hpi, histp = _mixer(
        xp, zeros_state, zeros_state, jnp.zeros((POOL_HIST * bp, W_B), F32),
        batch=bp, steps=128, start_pos=0, **shared)
    yp = _mlp(h1p, *mlp_w, block_rows=1024)
    y_prompt = jnp.transpose(yp.reshape(seq, bp, D_MODEL), (1, 0, 2))
    pool_prompt = jnp.transpose(histp.reshape(POOL_HIST, bp, W_B)[1:], (1, 0, 2))

    xs = x_sample.reshape(bs, D_MODEL)
    hist_s = jnp.concatenate(
        [jnp.zeros((1, bs, W_B), F32), jnp.transpose(state_pool, (1, 0, 2))],
        axis=0).reshape(POOL_HIST * bs, W_B)
    h1s, hsr, hsi, hists = _mixer(
        xs, state_s5_re.reshape(bs, N_STATE), state_s5_im.reshape(bs, N_STATE),
        hist_s, batch=bs, steps=1, start_pos=PAST_LEN, **shared)
    ys = _mlp(h1s, *mlp_w, block_rows=bs)
    y_sample = ys.reshape(bs, 1, D_MODEL)
    pool_sample = jnp.transpose(hists.reshape(POOL_HIST, bs, W_B)[1:], (1, 0, 2))

    st = lambda a, b: a.reshape(b, S5_GROUPS, S5_STATE)
    return (y_prompt, y_sample, st(hpr, bp), st(hpi, bp), pool_prompt,
            st(hsr, bs), st(hsi, bs), pool_sample)
```

```python
import functools
import math

import jax
import jax.numpy as jnp
from jax import lax
from jax.experimental import pallas as pl
from jax.experimental.pallas import tpu as pltpu

F32 = jnp.float32
BF16 = jnp.bfloat16

D_MODEL = 1024
W_A = 512
W_B = 512
S5_H = 16
S5_GROUPS = 32
S5_STATE = 64
N_STATE = S5_GROUPS * S5_STATE
N_SLAB = 4
SLAB_GROUPS = S5_GROUPS // N_SLAB
SLAB_U = SLAB_GROUPS * S5_H
SLAB_S = SLAB_GROUPS * S5_STATE
POOL_WINDOWS = (2, 4, 8, 16)
POOL_CH = 128
POOL_HIST = 16
D_FF = 4096
FF_CHUNK = 1024
EPS = 1e-6
PAST_LEN = 16384
GELU_C = math.sqrt(2.0 / math.pi)

VMEM_LIMIT_BYTES = 56 * 1024 * 1024


def _rms_norm(x, g):
    ms = jnp.mean(x * x, axis=-1, keepdims=True)
    return x * lax.rsqrt(ms + EPS) * g


def _full_spec(shape):
    return pl.BlockSpec(shape, lambda *_: (0,) * len(shape))


def _s5_prep_kernel(lam_re_ref, lam_im_ref, log_dt_ref, bt_re_ref, bt_im_ref,
                    ct_re_ref, ct_im_ref,
                    a_re_ref, a_im_ref, bm_ref, cmt_ref):
    lam_re = lam_re_ref[...]
    lam_im = lam_im_ref[...]
    dt = jnp.exp(log_dt_ref[...])
    mag = jnp.exp(lam_re * dt)
    ang = lam_im * dt
    a_re = mag * jnp.cos(ang)
    a_im = mag * jnp.sin(ang)
    a_re_ref[...] = a_re
    a_im_ref[...] = a_im
    n_re = a_re - 1.0
    n_im = a_im
    den = lam_re * lam_re + lam_im * lam_im
    k_re = (n_re * lam_re + n_im * lam_im) / den
    k_im = (n_im * lam_re - n_re * lam_im) / den
    bt_re = bt_re_ref[...]
    bt_im = bt_im_ref[...]
    bb_re = k_re * bt_re - k_im * bt_im
    bb_im = k_re * bt_im + k_im * bt_re
    ct_re = ct_re_ref[...]
    ct_im_neg = -ct_im_ref[...]

    rows = lax.broadcasted_iota(jnp.int32, (SLAB_U, SLAB_S), 0) // S5_H
    cols = lax.broadcasted_iota(jnp.int32, (SLAB_U, SLAB_S), 1) // S5_STATE
    diag = rows == cols

    def block_diag(m, j):
        sl = m[:, j * SLAB_S:(j + 1) * SLAB_S]
        tiled = jnp.concatenate([sl] * SLAB_GROUPS, axis=0)
        return jnp.where(diag, tiled, 0.0)

    for j in range(N_SLAB):
        bm_ref[j, :, :SLAB_S] = block_diag(bb_re, j).astype(BF16)
        bm_ref[j, :, SLAB_S:] = block_diag(bb_im, j).astype(BF16)
        cmt_ref[j, :, :SLAB_S] = block_diag(ct_re, j).astype(BF16)
        cmt_ref[j, :, SLAB_S:] = block_diag(ct_im_neg, j).astype(BF16)


def _s5_prep(lam_re, lam_im, log_dt, b_re, b_im, c_re, c_im):
    row = lambda a: a.reshape(1, N_STATE)
    bt = lambda b: jnp.transpose(b, (2, 0, 1)).reshape(S5_H, N_STATE)
    ct = lambda c: jnp.transpose(c, (1, 0, 2)).reshape(S5_H, N_STATE)
    ins = (row(lam_re), row(lam_im),
           row(jnp.broadcast_to(log_dt[:, None], (S5_GROUPS, S5_STATE))),
           bt(b_re), bt(b_im), ct(c_re), ct(c_im))
    out_shape = (
        jax.ShapeDtypeStruct((1, N_STATE), F32),
        jax.ShapeDtypeStruct((1, N_STATE), F32),
        jax.ShapeDtypeStruct((N_SLAB, SLAB_U, 2 * SLAB_S), BF16),
        jax.ShapeDtypeStruct((N_SLAB, SLAB_U, 2 * SLAB_S), BF16),
    )
    return pl.pallas_call(
        _s5_prep_kernel,
        out_shape=out_shape,
        name="s5_prep",
    )(*ins)


def _mixer_math(x, i, h0r_ref, h0i_ref, hist0_ref, g_pre_ref, g_post_ref,
                win_ref, a_re_ref, a_im_ref, bm_ref, cmt_ref, d_ref, glu_ref,
                poolw_ref, pscale_ref, wout_ref,
                hr_out_ref, hi_out_ref, hist_out_ref,
                slab_ref, hr_s, hi_s, hist_s,
                *, batch, steps, start_pos):
    rows = batch * steps

    @pl.when(i == 0)
    def _():
        hr_s[...] = h0r_ref[...]
        hi_s[...] = h0i_ref[...]
        hist_s[...] = hist0_ref[...]

    xn = _rms_norm(x, g_pre_ref[...]).astype(BF16)
    u = jnp.dot(xn, win_ref[...], preferred_element_type=F32)
    ua = u[:, :W_A]
    ub = u[:, W_A:]
    ua_bf = ua.astype(BF16)

    ys = []
    for j in range(N_SLAB):
        st = slice(j * SLAB_S, (j + 1) * SLAB_S)
        slab_ref[...] = jnp.dot(ua_bf[:, j * SLAB_U:(j + 1) * SLAB_U], bm_ref[j],
                                preferred_element_type=F32)
        ar = jnp.broadcast_to(a_re_ref[:, st], (batch, SLAB_S))
        ai = jnp.broadcast_to(a_im_ref[:, st], (batch, SLAB_S))

        def step(t, carry):
            hr, hi = carry
            r0 = pl.multiple_of(t * batch, batch)
            bur = slab_ref[pl.ds(r0, batch), :SLAB_S]
            bui = slab_ref[pl.ds(r0, batch), SLAB_S:]
            nr = ar * hr - ai * hi + bur
            ni = ar * hi + ai * hr + bui
            slab_ref[pl.ds(r0, batch), :SLAB_S] = nr
            slab_ref[pl.ds(r0, batch), SLAB_S:] = ni
            return nr, ni

        hr, hi = lax.fori_loop(0, steps, step, (hr_s[:, st], hi_s[:, st]),
                               unroll=min(steps, 8))
        hr_s[:, st] = hr
        hi_s[:, st] = hi
        ys.append(lax.dot_general(slab_ref[...].astype(BF16), cmt_ref[j],
                                  (((1,), (1,)), ((), ())),
                                  preferred_element_type=F32))
    hr_out_ref[...] = hr_s[...]
    hi_out_ref[...] = hi_s[...]

    y = jnp.concatenate(ys, axis=1) + d_ref[...] * ua
    y = y * (0.5 * (1.0 + jnp.tanh(GELU_C * (y + 0.044715 * (y * y * y)))))
    y_bf = y.astype(BF16)
    half = W_A // 2
    gate = jnp.concatenate(
        [jnp.dot(y_bf[:, :half], glu_ref[0], preferred_element_type=F32),
         jnp.dot(y_bf[:, half:], glu_ref[1], preferred_element_type=F32)], axis=1)
    ya = y * (1.0 / (1.0 + jnp.exp(-gate)))

    ext = jnp.concatenate([hist_s[...], ub], axis=0)
    n_ext = POOL_HIST * batch + rows
    hist_s[...] = ext[n_ext - POOL_HIST * batch:, :]
    hist_out_ref[...] = ext[n_ext - POOL_HIST * batch:, :]
    t_loc = lax.broadcasted_iota(jnp.int32, (rows, 1), 0) // batch
    pos1 = t_loc + (start_pos + 1) + i * steps
    pooled = []
    for gi, w in enumerate(POOL_WINDOWS):
        s = ext[:, gi * POOL_CH:(gi + 1) * POOL_CH]
        span = 1
        while span < w:
            n = s.shape[0]
            s = s[span * batch:, :] + s[:n - span * batch, :]
            span *= 2
        win = s[s.shape[0] - rows:, :]
        count = jnp.minimum(pos1, w).astype(F32)
        pooled.append(win / count - ub[:, gi * POOL_CH:(gi + 1) * POOL_CH])
    pooled = jnp.concatenate(pooled, axis=1).astype(BF16)
    halfb = W_B // 2
    yb = jnp.concatenate(
        [jnp.dot(pooled[:, :halfb], poolw_ref[0], preferred_element_type=F32),
         jnp.dot(pooled[:, halfb:], poolw_ref[1], preferred_element_type=F32)], axis=1)
    yb = yb * pscale_ref[...]

    ycat = jnp.concatenate([ya, yb], axis=1).astype(BF16)
    mix = jnp.dot(ycat, wout_ref[...], preferred_element_type=F32)
    return x + _rms_norm(mix, g_post_ref[...])


N_SMALL = 15


def _mixer_kernel_rows(x_ref, *refs, batch, steps, start_pos):
    small, (h1_ref, *outs), scratch = refs[:N_SMALL], refs[N_SMALL:N_SMALL + 4], refs[N_SMALL + 4:]
    i = pl.program_id(0)
    h1_ref[...] = _mixer_math(x_ref[...], i, *small, *outs, *scratch,
                              batch=batch, steps=steps, start_pos=start_pos)


def _mixer_kernel_seq(x_hbm, *refs, batch, steps, start_pos):
    small = refs[:N_SMALL]
    h1_hbm, *outs = refs[N_SMALL:N_SMALL + 4]
    scratch = refs[N_SMALL + 4:N_SMALL + 8]
    xbuf, hbuf, in_sem, out_sem = refs[N_SMALL + 8:]
    i = pl.program_id(0)
    n_chunks = pl.num_programs(0)
    slot = lax.rem(i, 2)

    def x_copy(b, chunk, sl):
        return pltpu.make_async_copy(
            x_hbm.at[b, pl.ds(chunk * steps, steps), :], xbuf.at[sl, :, b, :],
            in_sem.at[sl])

    def h_copy(b, chunk, sl):
        return pltpu.make_async_copy(
            hbuf.at[sl, :, b, :], h1_hbm.at[b, pl.ds(chunk * steps, steps), :],
            out_sem.at[sl])

    @pl.when(i == 0)
    def _():
        for b in range(batch):
            x_copy(b, 0, 0).start()

    @pl.when(i + 1 < n_chunks)
    def _():
        for b in range(batch):
            x_copy(b, i + 1, 1 - slot).start()

    for b in range(batch):
        x_copy(b, i, slot).wait()
    x = xbuf[slot].reshape(steps * batch, D_MODEL)
    h1 = _mixer_math(x, i, *small, *outs, *scratch,
                     batch=batch, steps=steps, start_pos=start_pos)

    @pl.when(i >= 2)
    def _():
        for b in range(batch):
            h_copy(b, i - 2, slot).wait()

    hbuf[slot] = h1.reshape(steps, batch, D_MODEL)
    for b in range(batch):
        h_copy(b, i, slot).start()

    @pl.when(i == n_chunks - 1)
    def _():
        @pl.when(i >= 1)
        def _():
            for b in range(batch):
                h_copy(b, i - 1, 1 - slot).wait()
        for b in range(batch):
            h_copy(b, i, slot).wait()


def _mixer(x, h0r, h0i, hist0, g_pre, g_post, win, a_re, a_im, bm, cmt, d,
           glu, poolw, pscale, wout, *, batch, steps, start_pos):
    small = [h0r, h0i, hist0, g_pre, g_post, win, a_re, a_im, bm, cmt, d, glu,
             poolw, pscale, wout]
    assert len(small) == N_SMALL
    rows = batch * steps
    state_shapes = (
        jax.ShapeDtypeStruct((batch, N_STATE), F32),
        jax.ShapeDtypeStruct((batch, N_STATE), F32),
        jax.ShapeDtypeStruct((POOL_HIST * batch, W_B), F32),
    )
    state_specs = (
        _full_spec((batch, N_STATE)),
        _full_spec((batch, N_STATE)),
        _full_spec((POOL_HIST * batch, W_B)),
    )
    scratch = [
        pltpu.VMEM((rows, 2 * SLAB_S), F32),
        pltpu.VMEM((batch, N_STATE), F32),
        pltpu.VMEM((batch, N_STATE), F32),
        pltpu.VMEM((POOL_HIST * batch, W_B), F32),
    ]
    params = pltpu.CompilerParams(dimension_semantics=("arbitrary",),
                                  vmem_limit_bytes=VMEM_LIMIT_BYTES)
    kw = dict(batch=batch, steps=steps, start_pos=start_pos)
    if steps == 1:
        return pl.pallas_call(
            functools.partial(_mixer_kernel_rows, **kw),
            out_shape=(jax.ShapeDtypeStruct(x.shape, F32),) + state_shapes,
            grid=(1,),
            in_specs=[_full_spec(x.shape)] + [_full_spec(a.shape) for a in small],
            out_specs=(_full_spec(x.shape),) + state_specs,
            scratch_shapes=scratch,
            compiler_params=params,
            name="mixer_step",
        )(x, *small)
    seq = x.shape[1]
    assert x.shape == (batch, seq, D_MODEL) and seq % steps == 0
    return pl.pallas_call(
        functools.partial(_mixer_kernel_seq, **kw),
        out_shape=(jax.ShapeDtypeStruct(x.shape, F32),) + state_shapes,
        grid=(seq // steps,),
        in_specs=[pl.BlockSpec(memory_space=pl.ANY)]
        + [_full_spec(a.shape) for a in small],
        out_specs=(pl.BlockSpec(memory_space=pl.ANY),) + state_specs,
        scratch_shapes=scratch + [
            pltpu.VMEM((2, steps, batch, D_MODEL), F32),
            pltpu.VMEM((2, steps, batch, D_MODEL), F32),
            pltpu.SemaphoreType.DMA((2,)),
            pltpu.SemaphoreType.DMA((2,)),
        ],
        compiler_params=params,
        name="mixer_seq",
    )(x, *small)


def _mlp_kernel(h_ref, g_pre_ref, g_post_ref, wup_ref, wdown_ref, o_ref):
    h = h_ref[...]
    hn = _rms_norm(h, g_pre_ref[...]).astype(BF16)
    acc = None
    for j in range(D_FF // FF_CHUNK):
        sl = slice(j * FF_CHUNK, (j + 1) * FF_CHUNK)
        up = jnp.dot(hn, wup_ref[:, sl], preferred_element_type=F32)
        up = jnp.maximum(up, 0.0)
        ff = (up * up).astype(BF16)
        part = jnp.dot(ff, wdown_ref[sl, :], preferred_element_type=F32)
        acc = part if acc is None else acc + part
    o_ref[...] = h + _rms_norm(acc, g_post_ref[...])


def _mlp(h, g_pre, g_post, wup, wdown, *, block_rows):
    n_rows = h.shape[0]
    assert n_rows % block_rows == 0
    return pl.pallas_call(
        _mlp_kernel,
        out_shape=jax.ShapeDtypeStruct((n_rows, D_MODEL), F32),
        grid=(n_rows // block_rows,),
        in_specs=[
            pl.BlockSpec((block_rows, D_MODEL), lambda i: (i, 0)),
            _full_spec(g_pre.shape),
            _full_spec(g_post.shape),
            pl.BlockSpec(memory_space=pltpu.VMEM),
            pl.BlockSpec(memory_space=pltpu.VMEM),
        ],
        out_specs=pl.BlockSpec((block_rows, D_MODEL), lambda i: (i, 0)),
        compiler_params=pltpu.CompilerParams(
            dimension_semantics=("arbitrary",),
            vmem_limit_bytes=VMEM_LIMIT_BYTES),
        name="mlp",
    )(h, g_pre, g_post, wup, wdown)


def _block_diag_pairs(w):
    n2, k, _ = w.shape
    w = w.reshape(n2 // 2, 2, k, k)
    z = jnp.zeros_like(w[:, 0])
    top = jnp.concatenate([w[:, 0], z], axis=2)
    bot = jnp.concatenate([z, w[:, 1]], axis=2)
    return jnp.concatenate([top, bot], axis=1)


def _glu_block_diag(w_glu):
    w = w_glu.reshape(2, 16, S5_H, S5_H)
    eye = jnp.eye(16, dtype=w.dtype)
    m = w[:, :, :, None, :] * eye[None, :, None, :, None]
    return m.reshape(2, 16 * S5_H, 16 * S5_H)


def kernel(x_prompt, x_sample, state_s5_re, state_s5_im, state_pool, norm_mix_pre, norm_mix_post, norm_mlp_pre, norm_mlp_post, w_in, s5_lambda_re, s5_lambda_im, s5_log_dt, s5_b_re, s5_b_im, s5_c_re, s5_c_im, s5_d, s5_w_glu, pool_w, pool_scale, w_out, w_mlp_up, w_mlp_down):
    bp, seq, _ = x_prompt.shape
    bs = x_sample.shape[0]

    a_re, a_im, bm, cmt = _s5_prep(s5_lambda_re, s5_lambda_im, s5_log_dt,
                                   s5_b_re, s5_b_im, s5_c_re, s5_c_im)
    row = lambda v: v.reshape(1, -1)
    shared = dict(
        g_pre=row(norm_mix_pre), g_post=row(norm_mix_post),
        win=w_in.astype(BF16), a_re=a_re, a_im=a_im, bm=bm, cmt=cmt,
        d=row(s5_d), glu=_glu_block_diag(s5_w_glu).astype(BF16),
        poolw=_block_diag_pairs(pool_w).astype(BF16), pscale=row(pool_scale),
        wout=w_out.astype(BF16))
    mlp_w = (row(norm_mlp_pre), row(norm_mlp_post),
             w_mlp_up.astype(BF16), w_mlp_down.astype(BF16))

    zeros_state = jnp.zeros((bp, N_STATE), F32)
    h1p, hpr, hpi, histp = _mixer(
        x_prompt, zeros_state, zeros_state, jnp.zeros((POOL_HIST * bp, W_B), F32),
        batch=bp, steps=128, start_pos=0, **shared)
    yp = _mlp(h1p.reshape(bp * seq, D_MODEL), *mlp_w, block_rows=1024)
    y_prompt = yp.reshape(bp, seq, D_MODEL)
    pool_prompt = jnp.transpose(histp.reshape(POOL_HIST, bp, W_B)[1:], (1, 0, 2))

    xs = x_sample.reshape(bs, D_MODEL)
    hist_s = jnp.concatenate(
        [jnp.zeros((1, bs, W_B), F32), jnp.transpose(state_pool, (1, 0, 2))],
        axis=0).reshape(POOL_HIST * bs, W_B)
    h1s, hsr, hsi, hists = _mixer(
        xs, state_s5_re.reshape(bs, N_STATE), state_s5_im.reshape(bs, N_STATE),
        hist_s, batch=bs, steps=1, start_pos=PAST_LEN, **shared)
    ys = _mlp(h1s, *mlp_w, block_rows=bs)
    y_sample = ys.reshape(bs, 1, D_MODEL)
    pool_sample = jnp.transpose(hists.reshape(POOL_HIST, bs, W_B)[1:], (1, 0, 2))

    st = lambda a, b: a.reshape(b, S5_GROUPS, S5_STATE)
    return (y_prompt, y_sample, st(hpr, bp), st(hpi, bp), pool_prompt,
            st(hsr, bs), st(hsi, bs), pool_sample)
```

```python
import functools
import math

import jax
import jax.numpy as jnp
from jax import lax
from jax.experimental import pallas as pl
from jax.experimental.pallas import tpu as pltpu

F32 = jnp.float32
BF16 = jnp.bfloat16

D_MODEL = 1024
W_A = 512
W_B = 512
S5_H = 16
S5_GROUPS = 32
S5_STATE = 64
N_STATE = S5_GROUPS * S5_STATE
N_SLAB = 4
SLAB_GROUPS = S5_GROUPS // N_SLAB
SLAB_U = SLAB_GROUPS * S5_H
SLAB_S = SLAB_GROUPS * S5_STATE
POOL_WINDOWS = (2, 4, 8, 16)
POOL_CH = 128
POOL_HIST = 16
D_FF = 4096
FF_CHUNK = 1024
EPS = 1e-6
PAST_LEN = 16384
GELU_C = math.sqrt(2.0 / math.pi)

VMEM_LIMIT_BYTES = 56 * 1024 * 1024


def _rms_norm(x, g):
    ms = jnp.mean(x * x, axis=-1, keepdims=True)
    return x * lax.rsqrt(ms + EPS) * g


def _full_spec(shape):
    return pl.BlockSpec(shape, lambda *_: (0,) * len(shape))


def _s5_prep_kernel(lam_re_ref, lam_im_ref, log_dt_ref, bt_re_ref, bt_im_ref,
                    ct_re_ref, ct_im_ref,
                    a_re_ref, a_im_ref, bm_ref, cmt_ref):
    lam_re = lam_re_ref[...]
    lam_im = lam_im_ref[...]
    dt = jnp.exp(log_dt_ref[...])
    mag = jnp.exp(lam_re * dt)
    ang = lam_im * dt
    a_re = mag * jnp.cos(ang)
    a_im = mag * jnp.sin(ang)
    a_re_ref[...] = a_re
    a_im_ref[...] = a_im
    n_re = a_re - 1.0
    n_im = a_im
    den = lam_re * lam_re + lam_im * lam_im
    k_re = (n_re * lam_re + n_im * lam_im) / den
    k_im = (n_im * lam_re - n_re * lam_im) / den
    bt_re = bt_re_ref[...]
    bt_im = bt_im_ref[...]
    bb_re = k_re * bt_re - k_im * bt_im
    bb_im = k_re * bt_im + k_im * bt_re
    ct_re = ct_re_ref[...]
    ct_im_neg = -ct_im_ref[...]

    rows = lax.broadcasted_iota(jnp.int32, (SLAB_U, SLAB_S), 0) // S5_H
    cols = lax.broadcasted_iota(jnp.int32, (SLAB_U, SLAB_S), 1) // S5_STATE
    diag = rows == cols

    def block_diag(m, j):
        sl = m[:, j * SLAB_S:(j + 1) * SLAB_S]
        tiled = jnp.concatenate([sl] * SLAB_GROUPS, axis=0)
        return jnp.where(diag, tiled, 0.0)

    for j in range(N_SLAB):
        bm_ref[j, :, :SLAB_S] = block_diag(bb_re, j).astype(BF16)
        bm_ref[j, :, SLAB_S:] = block_diag(bb_im, j).astype(BF16)
        cmt_ref[j, :, :SLAB_S] = block_diag(ct_re, j).astype(BF16)
        cmt_ref[j, :, SLAB_S:] = block_diag(ct_im_neg, j).astype(BF16)


def _s5_prep(lam_re, lam_im, log_dt, b_re, b_im, c_re, c_im):
    row = lambda a: a.reshape(1, N_STATE)
    bt = lambda b: jnp.transpose(b, (2, 0, 1)).reshape(S5_H, N_STATE)
    ct = lambda c: jnp.transpose(c, (1, 0, 2)).reshape(S5_H, N_STATE)
    ins = (row(lam_re), row(lam_im),
           row(jnp.broadcast_to(log_dt[:, None], (S5_GROUPS, S5_STATE))),
           bt(b_re), bt(b_im), ct(c_re), ct(c_im))
    out_shape = (
        jax.ShapeDtypeStruct((1, N_STATE), F32),
        jax.ShapeDtypeStruct((1, N_STATE), F32),
        jax.ShapeDtypeStruct((N_SLAB, SLAB_U, 2 * SLAB_S), BF16),
        jax.ShapeDtypeStruct((N_SLAB, SLAB_U, 2 * SLAB_S), BF16),
    )
    return pl.pallas_call(
        _s5_prep_kernel,
        out_shape=out_shape,
        name="s5_prep",
    )(*ins)


def _mixer_math(x, i, h0r_ref, h0i_ref, hist0_ref, g_pre_ref, g_post_ref,
                win_ref, a_re_ref, a_im_ref, bm_ref, cmt_ref, d_ref, glu_ref,
                poolw_ref, pscale_ref, wout_ref,
                hr_out_ref, hi_out_ref, hist_out_ref,
                slab_ref, hr_s, hi_s, hist_s,
                *, batch, steps, start_pos):
    rows = batch * steps

    @pl.when(i == 0)
    def _():
        hr_s[...] = h0r_ref[...]
        hi_s[...] = h0i_ref[...]
        hist_s[...] = hist0_ref[...]

    xn = _rms_norm(x, g_pre_ref[...]).astype(BF16)
    u = jnp.dot(xn, win_ref[...], preferred_element_type=F32)
    ua = u[:, :W_A]
    ub = u[:, W_A:]
    ua_bf = ua.astype(BF16)

    ys = []
    for j in range(N_SLAB):
        st = slice(j * SLAB_S, (j + 1) * SLAB_S)
        sb = slab_ref.at[j % 2]
        sb[...] = jnp.dot(ua_bf[:, j * SLAB_U:(j + 1) * SLAB_U], bm_ref[j],
                          preferred_element_type=F32)
        ar = jnp.broadcast_to(a_re_ref[:, st], (batch, SLAB_S))
        ai = jnp.broadcast_to(a_im_ref[:, st], (batch, SLAB_S))
        hr = hr_s[:, st]
        hi = hi_s[:, st]
        for t in range(steps):
            rt = slice(t * batch, (t + 1) * batch)
            nr = ar * hr - ai * hi + sb[rt, :SLAB_S]
            ni = ar * hi + ai * hr + sb[rt, SLAB_S:]
            sb[rt, :SLAB_S] = nr
            sb[rt, SLAB_S:] = ni
            hr, hi = nr, ni
        hr_s[:, st] = hr
        hi_s[:, st] = hi
        ys.append(lax.dot_general(sb[...].astype(BF16), cmt_ref[j],
                                  (((1,), (1,)), ((), ())),
                                  preferred_element_type=F32))
    hr_out_ref[...] = hr_s[...]
    hi_out_ref[...] = hi_s[...]

    y = jnp.concatenate(ys, axis=1) + d_ref[...] * ua
    y = y * (0.5 * (1.0 + jnp.tanh(GELU_C * (y + 0.044715 * (y * y * y)))))
    y_bf = y.astype(BF16)
    half = W_A // 2
    gate = jnp.concatenate(
        [jnp.dot(y_bf[:, :half], glu_ref[0], preferred_element_type=F32),
         jnp.dot(y_bf[:, half:], glu_ref[1], preferred_element_type=F32)], axis=1)
    ya = y * (1.0 / (1.0 + jnp.exp(-gate)))

    ext = jnp.concatenate([hist_s[...], ub], axis=0)
    n_ext = POOL_HIST * batch + rows
    hist_s[...] = ext[n_ext - POOL_HIST * batch:, :]
    hist_out_ref[...] = ext[n_ext - POOL_HIST * batch:, :]
    t_loc = lax.broadcasted_iota(jnp.int32, (rows, 1), 0) // batch
    pos1 = t_loc + (start_pos + 1) + i * steps
    pooled = []
    for gi, w in enumerate(POOL_WINDOWS):
        s = ext[:, gi * POOL_CH:(gi + 1) * POOL_CH]
        span = 1
        while span < w:
            n = s.shape[0]
            s = s[span * batch:, :] + s[:n - span * batch, :]
            span *= 2
        win = s[s.shape[0] - rows:, :]
        count = jnp.minimum(pos1, w).astype(F32)
        pooled.append(win / count - ub[:, gi * POOL_CH:(gi + 1) * POOL_CH])
    pooled = jnp.concatenate(pooled, axis=1).astype(BF16)
    halfb = W_B // 2
    yb = jnp.concatenate(
        [jnp.dot(pooled[:, :halfb], poolw_ref[0], preferred_element_type=F32),
         jnp.dot(pooled[:, halfb:], poolw_ref[1], preferred_element_type=F32)], axis=1)
    yb = yb * pscale_ref[...]

    ycat = jnp.concatenate([ya, yb], axis=1).astype(BF16)
    mix = jnp.dot(ycat, wout_ref[...], preferred_element_type=F32)
    return x + _rms_norm(mix, g_post_ref[...])


N_SMALL = 15


def _mixer_kernel_rows(x_ref, *refs, batch, steps, start_pos):
    small, (h1_ref, *outs), scratch = refs[:N_SMALL], refs[N_SMALL:N_SMALL + 4], refs[N_SMALL + 4:]
    i = pl.program_id(0)
    h1_ref[...] = _mixer_math(x_ref[...], i, *small, *outs, *scratch,
                              batch=batch, steps=steps, start_pos=start_pos)


def _mixer_kernel_seq(x_hbm, *refs, batch, steps, start_pos):
    small = refs[:N_SMALL]
    h1_hbm, *outs = refs[N_SMALL:N_SMALL + 4]
    scratch = refs[N_SMALL + 4:N_SMALL + 8]
    xbuf, hbuf, in_sem, out_sem = refs[N_SMALL + 8:]
    i = pl.program_id(0)
    n_chunks = pl.num_programs(0)
    slot = lax.rem(i, 2)

    def x_copy(b, chunk, sl):
        return pltpu.make_async_copy(
            x_hbm.at[b, pl.ds(chunk * steps, steps), :], xbuf.at[sl, :, b, :],
            in_sem.at[sl])

    def h_copy(b, chunk, sl):
        return pltpu.make_async_copy(
            hbuf.at[sl, :, b, :], h1_hbm.at[b, pl.ds(chunk * steps, steps), :],
            out_sem.at[sl])

    @pl.when(i == 0)
    def _():
        for b in range(batch):
            x_copy(b, 0, 0).start()

    @pl.when(i + 1 < n_chunks)
    def _():
        for b in range(batch):
            x_copy(b, i + 1, 1 - slot).start()

    for b in range(batch):
        x_copy(b, i, slot).wait()
    x = xbuf[slot].reshape(steps * batch, D_MODEL)
    h1 = _mixer_math(x, i, *small, *outs, *scratch,
                     batch=batch, steps=steps, start_pos=start_pos)

    @pl.when(i >= 2)
    def _():
        for b in range(batch):
            h_copy(b, i - 2, slot).wait()

    hbuf[slot] = h1.reshape(steps, batch, D_MODEL)
    for b in range(batch):
        h_copy(b, i, slot).start()

    @pl.when(i == n_chunks - 1)
    def _():
        @pl.when(i >= 1)
        def _():
            for b in range(batch):
                h_copy(b, i - 1, 1 - slot).wait()
        for b in range(batch):
            h_copy(b, i, slot).wait()


def _mixer(x, h0r, h0i, hist0, g_pre, g_post, win, a_re, a_im, bm, cmt, d,
           glu, poolw, pscale, wout, *, batch, steps, start_pos):
    small = [h0r, h0i, hist0, g_pre, g_post, win, a_re, a_im, bm, cmt, d, glu,
             poolw, pscale, wout]
    assert len(small) == N_SMALL
    rows = batch * steps
    state_shapes = (
        jax.ShapeDtypeStruct((batch, N_STATE), F32),
        jax.ShapeDtypeStruct((batch, N_STATE), F32),
        jax.ShapeDtypeStruct((POOL_HIST * batch, W_B), F32),
    )
    state_specs = (
        _full_spec((batch, N_STATE)),
        _full_spec((batch, N_STATE)),
        _full_spec((POOL_HIST * batch, W_B)),
    )
    scratch = [
        pltpu.VMEM((2, rows, 2 * SLAB_S), F32),
        pltpu.VMEM((batch, N_STATE), F32),
        pltpu.VMEM((batch, N_STATE), F32),
        pltpu.VMEM((POOL_HIST * batch, W_B), F32),
    ]
    params = pltpu.CompilerParams(dimension_semantics=("arbitrary",),
                                  vmem_limit_bytes=VMEM_LIMIT_BYTES)
    kw = dict(batch=batch, steps=steps, start_pos=start_pos)
    if steps == 1:
        return pl.pallas_call(
            functools.partial(_mixer_kernel_rows, **kw),
            out_shape=(jax.ShapeDtypeStruct(x.shape, F32),) + state_shapes,
            grid=(1,),
            in_specs=[_full_spec(x.shape)] + [_full_spec(a.shape) for a in small],
            out_specs=(_full_spec(x.shape),) + state_specs,
            scratch_shapes=scratch,
            compiler_params=params,
            name="mixer_step",
        )(x, *small)
    seq = x.shape[1]
    assert x.shape == (batch, seq, D_MODEL) and seq % steps == 0
    return pl.pallas_call(
        functools.partial(_mixer_kernel_seq, **kw),
        out_shape=(jax.ShapeDtypeStruct(x.shape, F32),) + state_shapes,
        grid=(seq // steps,),
        in_specs=[pl.BlockSpec(memory_space=pl.ANY)]
        + [_full_spec(a.shape) for a in small],
        out_specs=(pl.BlockSpec(memory_space=pl.ANY),) + state_specs,
        scratch_shapes=scratch + [
            pltpu.VMEM((2, steps, batch, D_MODEL), F32),
            pltpu.VMEM((2, steps, batch, D_MODEL), F32),
            pltpu.SemaphoreType.DMA((2,)),
            pltpu.SemaphoreType.DMA((2,)),
        ],
        compiler_params=params,
        name="mixer_seq",
    )(x, *small)


def _mlp_kernel(h_ref, g_pre_ref, g_post_ref, wup_ref, wdown_ref, o_ref):
    h = h_ref[...]
    hn = _rms_norm(h, g_pre_ref[...]).astype(BF16)
    acc = None
    for j in range(D_FF // FF_CHUNK):
        sl = slice(j * FF_CHUNK, (j + 1) * FF_CHUNK)
        up = jnp.dot(hn, wup_ref[:, sl], preferred_element_type=F32)
        up = jnp.maximum(up, 0.0)
        ff = (up * up).astype(BF16)
        part = jnp.dot(ff, wdown_ref[sl, :], preferred_element_type=F32)
        acc = part if acc is None else acc + part
    o_ref[...] = h + _rms_norm(acc, g_post_ref[...])


def _mlp(h, g_pre, g_post, wup, wdown, *, block_rows):
    n_rows = h.shape[0]
    assert n_rows % block_rows == 0
    return pl.pallas_call(
        _mlp_kernel,
        out_shape=jax.ShapeDtypeStruct((n_rows, D_MODEL), F32),
        grid=(n_rows // block_rows,),
        in_specs=[
            pl.BlockSpec((block_rows, D_MODEL), lambda i: (i, 0)),
            _full_spec(g_pre.shape),
            _full_spec(g_post.shape),
            pl.BlockSpec(memory_space=pltpu.VMEM),
            pl.BlockSpec(memory_space=pltpu.VMEM),
        ],
        out_specs=pl.BlockSpec((block_rows, D_MODEL), lambda i: (i, 0)),
        compiler_params=pltpu.CompilerParams(
            dimension_semantics=("arbitrary",),
            vmem_limit_bytes=VMEM_LIMIT_BYTES),
        name="mlp",
    )(h, g_pre, g_post, wup, wdown)


def _block_diag_pairs(w):
    n2, k, _ = w.shape
    w = w.reshape(n2 // 2, 2, k, k)
    z = jnp.zeros_like(w[:, 0])
    top = jnp.concatenate([w[:, 0], z], axis=2)
    bot = jnp.concatenate([z, w[:, 1]], axis=2)
    return jnp.concatenate([top, bot], axis=1)


def _glu_block_diag(w_glu):
    w = w_glu.reshape(2, 16, S5_H, S5_H)
    eye = jnp.eye(16, dtype=w.dtype)
    m = w[:, :, :, None, :] * eye[None, :, None, :, None]
    return m.reshape(2, 16 * S5_H, 16 * S5_H)


def kernel(x_prompt, x_sample, state_s5_re, state_s5_im, state_pool, norm_mix_pre, norm_mix_post, norm_mlp_pre, norm_mlp_post, w_in, s5_lambda_re, s5_lambda_im, s5_log_dt, s5_b_re, s5_b_im, s5_c_re, s5_c_im, s5_d, s5_w_glu, pool_w, pool_scale, w_out, w_mlp_up, w_mlp_down):
    bp, seq, _ = x_prompt.shape
    bs = x_sample.shape[0]

    a_re, a_im, bm, cmt = _s5_prep(s5_lambda_re, s5_lambda_im, s5_log_dt,
                                   s5_b_re, s5_b_im, s5_c_re, s5_c_im)
    row = lambda v: v.reshape(1, -1)
    shared = dict(
        g_pre=row(norm_mix_pre), g_post=row(norm_mix_post),
        win=w_in.astype(BF16), a_re=a_re, a_im=a_im, bm=bm, cmt=cmt,
        d=row(s5_d), glu=_glu_block_diag(s5_w_glu).astype(BF16),
        poolw=_block_diag_pairs(pool_w).astype(BF16), pscale=row(pool_scale),
        wout=w_out.astype(BF16))
    mlp_w = (row(norm_mlp_pre), row(norm_mlp_post),
             w_mlp_up.astype(BF16), w_mlp_down.astype(BF16))

    zeros_state = jnp.zeros((bp, N_STATE), F32)
    h1p, hpr, hpi, histp = _mixer(
        x_prompt, zeros_state, zeros_state, jnp.zeros((POOL_HIST * bp, W_B), F32),
        batch=bp, steps=128, start_pos=0, **shared)
    yp = _mlp(h1p.reshape(bp * seq, D_MODEL), *mlp_w, block_rows=1024)
    y_prompt = yp.reshape(bp, seq, D_MODEL)
    pool_prompt = jnp.transpose(histp.reshape(POOL_HIST, bp, W_B)[1:], (1, 0, 2))

    xs = x_sample.reshape(bs, D_MODEL)
    hist_s = jnp.concatenate(
        [jnp.zeros((1, bs, W_B), F32), jnp.transpose(state_pool, (1, 0, 2))],
        axis=0).reshape(POOL_HIST * bs, W_B)
    h1s, hsr, hsi, hists = _mixer(
        xs, state_s5_re.reshape(bs, N_STATE), state_s5_im.reshape(bs, N_STATE),
        hist_s, batch=bs, steps=1, start_pos=PAST_LEN, **shared)
    ys = _mlp(h1s, *mlp_w, block_rows=bs)
    y_sample = ys.reshape(bs, 1, D_MODEL)
    pool_sample = jnp.transpose(hists.reshape(POOL_HIST, bs, W_B)[1:], (1, 0, 2))

    st = lambda a, b: a.reshape(b, S5_GROUPS, S5_STATE)
    return (y_prompt, y_sample, st(hpr, bp), st(hpi, bp), pool_prompt,
            st(hsr, bs), st(hsi, bs), pool_sample)
```

```python
import functools
import math

import jax
import jax.numpy as jnp
from jax import lax
from jax.experimental import pallas as pl
from jax.experimental.pallas import tpu as pltpu

F32 = jnp.float32
BF16 = jnp.bfloat16

D_MODEL = 1024
W_A = 512
W_B = 512
S5_H = 16
S5_GROUPS = 32
S5_STATE = 64
N_STATE = S5_GROUPS * S5_STATE
N_SLAB = 4
SLAB_GROUPS = S5_GROUPS // N_SLAB
SLAB_U = SLAB_GROUPS * S5_H
SLAB_S = SLAB_GROUPS * S5_STATE
POOL_WINDOWS = (2, 4, 8, 16)
POOL_CH = 128
POOL_HIST = 16
D_FF = 4096
FF_CHUNK = 1024
EPS = 1e-6
PAST_LEN = 16384
GELU_C = math.sqrt(2.0 / math.pi)

VMEM_LIMIT_BYTES = 56 * 1024 * 1024


def _rms_norm(x, g):
    ms = jnp.mean(x * x, axis=-1, keepdims=True)
    return x * lax.rsqrt(ms + EPS) * g


def _rev_rows(v, group):
    n = v.shape[0] // group
    return jnp.concatenate([v[(n - 1 - k) * group:(n - k) * group] for k in range(n)],
                           axis=0)


def _finish(gen):
    while True:
        try:
            next(gen)
        except StopIteration as e:
            return e.value


def _interleave(gens, order):
    results = {}
    for key in order:
        try:
            next(gens[key])
        except StopIteration as e:
            results[key] = e.value
    for key, gen in gens.items():
        if key not in results:
            results[key] = _finish(gen)
    return results


def _full_spec(shape):
    return pl.BlockSpec(shape, lambda *_: (0,) * len(shape))


def _s5_prep_kernel(lam_re_ref, lam_im_ref, log_dt_ref, bt_re_ref, bt_im_ref,
                    ct_re_ref, ct_im_ref,
                    a_re_ref, a_im_ref, bm_ref, cmt_ref):
    lam_re = lam_re_ref[...]
    lam_im = lam_im_ref[...]
    dt = jnp.exp(log_dt_ref[...])
    mag = jnp.exp(lam_re * dt)
    ang = lam_im * dt
    a_re = mag * jnp.cos(ang)
    a_im = mag * jnp.sin(ang)
    a_re_ref[...] = a_re
    a_im_ref[...] = a_im
    n_re = a_re - 1.0
    n_im = a_im
    den = lam_re * lam_re + lam_im * lam_im
    k_re = (n_re * lam_re + n_im * lam_im) / den
    k_im = (n_im * lam_re - n_re * lam_im) / den
    bt_re = bt_re_ref[...]
    bt_im = bt_im_ref[...]
    bb_re = k_re * bt_re - k_im * bt_im
    bb_im = k_re * bt_im + k_im * bt_re
    ct_re = ct_re_ref[...]
    ct_im_neg = -ct_im_ref[...]

    rows = lax.broadcasted_iota(jnp.int32, (SLAB_U, SLAB_S), 0) // S5_H
    cols = lax.broadcasted_iota(jnp.int32, (SLAB_U, SLAB_S), 1) // S5_STATE
    diag = rows == cols

    def block_diag(m, j):
        sl = m[:, j * SLAB_S:(j + 1) * SLAB_S]
        tiled = jnp.concatenate([sl] * SLAB_GROUPS, axis=0)
        return jnp.where(diag, tiled, 0.0)

    for j in range(N_SLAB):
        bm_ref[j, :, :SLAB_S] = block_diag(bb_re, j).astype(BF16)
        bm_ref[j, :, SLAB_S:] = block_diag(bb_im, j).astype(BF16)
        cmt_ref[j, :, :SLAB_S] = block_diag(ct_re, j).astype(BF16)
        cmt_ref[j, :, SLAB_S:] = block_diag(ct_im_neg, j).astype(BF16)


def _s5_prep(lam_re, lam_im, log_dt, b_re, b_im, c_re, c_im):
    row = lambda a: a.reshape(1, N_STATE)
    bt = lambda b: jnp.transpose(b, (2, 0, 1)).reshape(S5_H, N_STATE)
    ct = lambda c: jnp.transpose(c, (1, 0, 2)).reshape(S5_H, N_STATE)
    ins = (row(lam_re), row(lam_im),
           row(jnp.broadcast_to(log_dt[:, None], (S5_GROUPS, S5_STATE))),
           bt(b_re), bt(b_im), ct(c_re), ct(c_im))
    out_shape = (
        jax.ShapeDtypeStruct((1, N_STATE), F32),
        jax.ShapeDtypeStruct((1, N_STATE), F32),
        jax.ShapeDtypeStruct((N_SLAB, SLAB_U, 2 * SLAB_S), BF16),
        jax.ShapeDtypeStruct((N_SLAB, SLAB_U, 2 * SLAB_S), BF16),
    )
    return pl.pallas_call(
        _s5_prep_kernel,
        out_shape=out_shape,
        name="s5_prep",
    )(*ins)


N_SMALL = 15
N_MLP = 4


def _mixer_stage(xn, i, h0r_ref, h0i_ref, hist0_ref, g_pre_ref, g_post_ref,
                 win_ref, a_re_ref, a_im_ref, bm_ref, cmt_ref, d_ref, glu_ref,
                 poolw_ref, pscale_ref, wout_ref,
                 hr_out_ref, hi_out_ref, hist_out_ref,
                 slab_ref, hr_s, hi_s, hist_s,
                 *, batch, steps, start_pos):
    del h0r_ref, h0i_ref, hist0_ref, g_pre_ref, g_post_ref
    del hr_out_ref, hi_out_ref, hist_out_ref
    rows = batch * steps

    ua = jnp.dot(xn, win_ref[:, :W_A], preferred_element_type=F32)
    yield
    ua_bf = ua.astype(BF16)

    def project_in(j):
        slab_ref[j % 2] = jnp.dot(ua_bf[:, j * SLAB_U:(j + 1) * SLAB_U], bm_ref[j],
                                  preferred_element_type=F32)

    def scan(j):
        st = slice(j * SLAB_S, (j + 1) * SLAB_S)
        sb = slab_ref.at[j % 2]
        ar = jnp.broadcast_to(a_re_ref[:, st], (batch, SLAB_S))
        ai = jnp.broadcast_to(a_im_ref[:, st], (batch, SLAB_S))
        hr = hr_s[:, st]
        hi = hi_s[:, st]
        for t in range(steps):
            rt = slice(t * batch, (t + 1) * batch)
            nr = ar * hr - ai * hi + sb[rt, :SLAB_S]
            ni = ar * hi + ai * hr + sb[rt, SLAB_S:]
            sb[rt, :SLAB_S] = nr
            sb[rt, SLAB_S:] = ni
            hr, hi = nr, ni
        hr_s[:, st] = hr
        hi_s[:, st] = hi

    def project_out(j):
        hs_rev = _rev_rows(slab_ref[j % 2].astype(BF16), 16)
        return _rev_rows(lax.dot_general(hs_rev, cmt_ref[j], (((1,), (1,)), ((), ())),
                                         preferred_element_type=F32), 16)

    ys = []
    project_in(0)
    yield
    scan(0)
    ub = jnp.dot(xn, win_ref[:, W_A:], preferred_element_type=F32)
    yield
    project_in(1)
    yield

    ext = jnp.concatenate([hist_s[...], ub], axis=0)
    n_ext = POOL_HIST * batch + rows
    hist_s[...] = ext[n_ext - POOL_HIST * batch:, :]
    t_loc = lax.broadcasted_iota(jnp.int32, (rows, 1), 0) // batch
    pos1 = t_loc + (start_pos + 1) + i * steps
    pooled = []
    for gi, w in enumerate(POOL_WINDOWS):
        s = ext[:, gi * POOL_CH:(gi + 1) * POOL_CH]
        span = 1
        while span < w:
            n = s.shape[0]
            s = s[span * batch:, :] + s[:n - span * batch, :]
            span *= 2
        win = s[s.shape[0] - rows:, :]
        count = jnp.minimum(pos1, w).astype(F32)
        pooled.append(win / count - ub[:, gi * POOL_CH:(gi + 1) * POOL_CH])
    pooled = jnp.concatenate(pooled, axis=1).astype(BF16)
    halfb = W_B // 2
    yb0 = jnp.dot(pooled[:, :halfb], poolw_ref[0], preferred_element_type=F32)
    yb1 = jnp.dot(pooled[:, halfb:], poolw_ref[1], preferred_element_type=F32)
    yield
    yb = jnp.concatenate([yb0, yb1], axis=1) * pscale_ref[...]

    ys.append(project_out(0))
    yield
    scan(1)
    for j in range(2, N_SLAB):
        project_in(j)
        yield
        ys.append(project_out(j - 1))
        yield
        scan(j)
    ys.append(project_out(N_SLAB - 1))
    yield

    y = jnp.concatenate(ys, axis=1) + d_ref[...] * ua
    y = y * (0.5 * (1.0 + jnp.tanh(GELU_C * (y + 0.044715 * (y * y * y)))))
    y_bf = y.astype(BF16)
    half = W_A // 2
    gate0 = jnp.dot(y_bf[:, :half], glu_ref[0], preferred_element_type=F32)
    gate1 = jnp.dot(y_bf[:, half:], glu_ref[1], preferred_element_type=F32)
    yield
    gate = jnp.concatenate([gate0, gate1], axis=1)
    ya = y * (1.0 / (1.0 + jnp.exp(-gate)))

    ycat = jnp.concatenate([ya, yb], axis=1).astype(BF16)
    mix = jnp.dot(ycat, wout_ref[...], preferred_element_type=F32)
    yield
    return mix


STAGE_ORDER = "AAAFAAFAAFAAFAFAFAFFA"


def _mlp_stage(h, g_pre_ref, g_post_ref, wup_ref, wdown_ref):
    hn = _rms_norm(h, g_pre_ref[...]).astype(BF16)
    acc = None
    for j in range(D_FF // FF_CHUNK):
        sl = slice(j * FF_CHUNK, (j + 1) * FF_CHUNK)
        up = jnp.dot(hn, wup_ref[:, sl], preferred_element_type=F32)
        yield
        up = jnp.maximum(up, 0.0)
        ff = (up * up).astype(BF16)
        part = jnp.dot(ff, wdown_ref[sl, :], preferred_element_type=F32)
        yield
        acc = part if acc is None else acc + part
    return h + _rms_norm(acc, g_post_ref[...])


def _load_state(small, scratch):
    scratch[1][...] = small[0][...]
    scratch[2][...] = small[1][...]
    scratch[3][...] = small[2][...]


def _store_state(outs, scratch):
    outs[0][...] = scratch[1][...]
    outs[1][...] = scratch[2][...]
    outs[2][...] = scratch[3][...]


def _layer_kernel_rows(x_ref, *refs, batch, steps, start_pos):
    small = refs[:N_SMALL]
    mlp_w = refs[N_SMALL:N_SMALL + N_MLP]
    y_ref, *outs = refs[N_SMALL + N_MLP:N_SMALL + N_MLP + 4]
    scratch = refs[N_SMALL + N_MLP + 4:]
    g_pre_ref, g_post_ref = small[3], small[4]
    _load_state(small, scratch)
    x = x_ref[...]
    xn = _rms_norm(x, g_pre_ref[...]).astype(BF16)
    mix = _finish(_mixer_stage(xn, 0, *small, *outs, *scratch,
                               batch=batch, steps=steps, start_pos=start_pos))
    h1 = x + _rms_norm(mix, g_post_ref[...])
    y_ref[...] = _finish(_mlp_stage(h1, *mlp_w))
    _store_state(outs, scratch)


N_XBUF = 4


def _layer_kernel_seq(x_hbm, *refs, batch, steps, start_pos):
    small = refs[:N_SMALL]
    mlp_w = refs[N_SMALL:N_SMALL + N_MLP]
    y_hbm, *outs = refs[N_SMALL + N_MLP:N_SMALL + N_MLP + 4]
    scratch = refs[N_SMALL + N_MLP + 4:N_SMALL + N_MLP + 8]
    xbuf, xnbuf, mixbuf, ybuf, in_sem, out_sem = refs[N_SMALL + N_MLP + 8:]
    g_pre_ref, g_post_ref = small[3], small[4]
    s = pl.program_id(0)
    n_chunks = pl.num_programs(0) - 1
    rows = steps * batch

    def x_copy(b, chunk):
        sl = lax.rem(chunk, N_XBUF)
        return pltpu.make_async_copy(
            x_hbm.at[b, pl.ds(chunk * steps, steps), :], xbuf.at[sl, :, b, :],
            in_sem.at[sl])

    def y_copy(b, chunk):
        sl = lax.rem(chunk, 2)
        return pltpu.make_async_copy(
            ybuf.at[sl, :, b, :], y_hbm.at[b, pl.ds(chunk * steps, steps), :],
            out_sem.at[sl])

    def x_rows(chunk):
        return xbuf[lax.rem(chunk + N_XBUF, N_XBUF)].reshape(rows, D_MODEL)

    def pre_norm(chunk):
        xn = _rms_norm(x_rows(chunk), g_pre_ref[...]).astype(BF16)
        xnbuf[lax.rem(chunk, 2)] = xn

    @pl.when(s == 0)
    def _():
        xbuf[2] = jnp.zeros(xbuf.shape[1:], F32)
        xbuf[3] = jnp.zeros(xbuf.shape[1:], F32)
        mixbuf[...] = jnp.zeros_like(mixbuf)
        for b in range(batch):
            x_copy(b, 0).start()
        for b in range(batch):
            x_copy(b, 1).start()
        for b in range(batch):
            x_copy(b, 0).wait()
        pre_norm(0)
        _load_state(small, scratch)

    @pl.when(s + 2 < n_chunks)
    def _():
        for b in range(batch):
            x_copy(b, s + 2).start()

    @pl.when(s + 1 < n_chunks)
    def _():
        for b in range(batch):
            x_copy(b, s + 1).wait()

    @pl.when(s >= 3)
    def _():
        for b in range(batch):
            y_copy(b, s - 3).wait()

    h1 = x_rows(s - 1) + _rms_norm(mixbuf[...], g_post_ref[...])
    stages = {
        "A": _mixer_stage(xnbuf[lax.rem(s, 2)], s, *small, *outs, *scratch,
                          batch=batch, steps=steps, start_pos=start_pos),
        "F": _mlp_stage(h1, *mlp_w),
    }
    res = _interleave(stages, STAGE_ORDER)
    ybuf[lax.rem(s + 1, 2)] = res["F"].reshape(steps, batch, D_MODEL)
    mixbuf[...] = res["A"]
    pre_norm(s + 1)

    @pl.when(s == n_chunks - 1)
    def _():
        _store_state(outs, scratch)

    @pl.when(s >= 1)
    def _():
        for b in range(batch):
            y_copy(b, s - 1).start()

    @pl.when(s == n_chunks)
    def _():
        @pl.when(s >= 2)
        def _():
            for b in range(batch):
                y_copy(b, s - 2).wait()
        for b in range(batch):
            y_copy(b, s - 1).wait()


def _state_shapes(batch):
    return (
        jax.ShapeDtypeStruct((batch, N_STATE), F32),
        jax.ShapeDtypeStruct((batch, N_STATE), F32),
        jax.ShapeDtypeStruct((POOL_HIST * batch, W_B), F32),
    )


def _state_specs(batch):
    return (
        _full_spec((batch, N_STATE)),
        _full_spec((batch, N_STATE)),
        _full_spec((POOL_HIST * batch, W_B)),
    )


def _mixer_scratch(batch, steps):
    return [
        pltpu.VMEM((2, batch * steps, 2 * SLAB_S), F32),
        pltpu.VMEM((batch, N_STATE), F32),
        pltpu.VMEM((batch, N_STATE), F32),
        pltpu.VMEM((POOL_HIST * batch, W_B), F32),
    ]


_VMEM_WHOLE = pl.BlockSpec(memory_space=pltpu.VMEM)
_COMPILER_PARAMS = pltpu.CompilerParams(dimension_semantics=("arbitrary",),
                                        vmem_limit_bytes=VMEM_LIMIT_BYTES)


def _layer_rows(x, small, mlp_w, *, start_pos):
    batch = x.shape[0]
    assert len(small) == N_SMALL and len(mlp_w) == N_MLP
    return pl.pallas_call(
        functools.partial(_layer_kernel_rows, batch=batch, steps=1,
                          start_pos=start_pos),
        out_shape=(jax.ShapeDtypeStruct(x.shape, F32),) + _state_shapes(batch),
        grid=(1,),
        in_specs=[_VMEM_WHOLE] * (1 + N_SMALL + N_MLP),
        out_specs=(_full_spec(x.shape),) + _state_specs(batch),
        scratch_shapes=_mixer_scratch(batch, 1),
        compiler_params=_COMPILER_PARAMS,
        name="layer_rows",
    )(x, *small, *mlp_w)


def _layer_seq(x, small, mlp_w, *, steps, start_pos):
    batch, seq, _ = x.shape
    rows = batch * steps
    assert len(small) == N_SMALL and len(mlp_w) == N_MLP
    assert seq % steps == 0 and seq // steps >= N_XBUF
    return pl.pallas_call(
        functools.partial(_layer_kernel_seq, batch=batch, steps=steps,
                          start_pos=start_pos),
        out_shape=(jax.ShapeDtypeStruct(x.shape, F32),) + _state_shapes(batch),
        grid=(seq // steps + 1,),
        in_specs=[pl.BlockSpec(memory_space=pl.ANY)]
        + [_VMEM_WHOLE] * (N_SMALL + N_MLP),
        out_specs=(pl.BlockSpec(memory_space=pl.ANY),) + _state_specs(batch),
        scratch_shapes=_mixer_scratch(batch, steps) + [
            pltpu.VMEM((N_XBUF, steps, batch, D_MODEL), F32),
            pltpu.VMEM((2, rows, D_MODEL), BF16),
            pltpu.VMEM((rows, D_MODEL), F32),
            pltpu.VMEM((2, steps, batch, D_MODEL), F32),
            pltpu.SemaphoreType.DMA((N_XBUF,)),
            pltpu.SemaphoreType.DMA((2,)),
        ],
        compiler_params=_COMPILER_PARAMS,
        name="layer_seq",
    )(x, *small, *mlp_w)


def _block_diag_pairs(w):
    n2, k, _ = w.shape
    w = w.reshape(n2 // 2, 2, k, k)
    z = jnp.zeros_like(w[:, 0])
    top = jnp.concatenate([w[:, 0], z], axis=2)
    bot = jnp.concatenate([z, w[:, 1]], axis=2)
    return jnp.concatenate([top, bot], axis=1)


def _glu_block_diag(w_glu):
    w = w_glu.reshape(2, 16, S5_H, S5_H)
    eye = jnp.eye(16, dtype=w.dtype)
    m = w[:, :, :, None, :] * eye[None, :, None, :, None]
    return m.reshape(2, 16 * S5_H, 16 * S5_H)


def kernel(x_prompt, x_sample, state_s5_re, state_s5_im, state_pool, norm_mix_pre, norm_mix_post, norm_mlp_pre, norm_mlp_post, w_in, s5_lambda_re, s5_lambda_im, s5_log_dt, s5_b_re, s5_b_im, s5_c_re, s5_c_im, s5_d, s5_w_glu, pool_w, pool_scale, w_out, w_mlp_up, w_mlp_down):
    bp, seq, _ = x_prompt.shape
    bs = x_sample.shape[0]

    a_re, a_im, bm, cmt = _s5_prep(s5_lambda_re, s5_lambda_im, s5_log_dt,
                                   s5_b_re, s5_b_im, s5_c_re, s5_c_im)
    row = lambda v: v.reshape(1, -1)
    params = [row(norm_mix_pre), row(norm_mix_post), w_in.astype(BF16), a_re, a_im,
              bm, cmt, row(s5_d), _glu_block_diag(s5_w_glu).astype(BF16),
              _block_diag_pairs(pool_w).astype(BF16), row(pool_scale),
              w_out.astype(BF16)]
    mlp_w = (row(norm_mlp_pre), row(norm_mlp_post),
             w_mlp_up.astype(BF16), w_mlp_down.astype(BF16))

    zeros_state = jnp.zeros((bp, N_STATE), F32)
    init_p = [zeros_state, zeros_state, jnp.zeros((POOL_HIST * bp, W_B), F32)]
    y_prompt, hpr, hpi, histp = _layer_seq(x_prompt, init_p + params, mlp_w,
                                           steps=64, start_pos=0)
    pool_prompt = jnp.transpose(histp.reshape(POOL_HIST, bp, W_B)[1:], (1, 0, 2))

    xs = x_sample.reshape(bs, D_MODEL)
    hist_s = jnp.concatenate(
        [jnp.zeros((1, bs, W_B), F32), jnp.transpose(state_pool, (1, 0, 2))],
        axis=0).reshape(POOL_HIST * bs, W_B)
    init_s = [state_s5_re.reshape(bs, N_STATE), state_s5_im.reshape(bs, N_STATE),
              hist_s]
    ys, hsr, hsi, hists = _layer_rows(xs, init_s + params, mlp_w, start_pos=PAST_LEN)
    y_sample = ys.reshape(bs, 1, D_MODEL)
    pool_sample = jnp.transpose(hists.reshape(POOL_HIST, bs, W_B)[1:], (1, 0, 2))

    st = lambda a, b: a.reshape(b, S5_GROUPS, S5_STATE)
    return (y_prompt, y_sample, st(hpr, bp), st(hpi, bp), pool_prompt,
            st(hsr, bs), st(hsi, bs), pool_sample)
```

```python
import functools
import math

import jax
import jax.numpy as jnp
from jax import lax
from jax.experimental import pallas as pl
from jax.experimental.pallas import tpu as pltpu

F32 = jnp.float32
BF16 = jnp.bfloat16

D_MODEL = 1024
W_A = 512
W_B = 512
S5_H = 16
S5_GROUPS = 32
S5_STATE = 64
N_STATE = S5_GROUPS * S5_STATE
N_SLAB = 4
SLAB_GROUPS = S5_GROUPS // N_SLAB
SLAB_U = SLAB_GROUPS * S5_H
SLAB_S = SLAB_GROUPS * S5_STATE
POOL_WINDOWS = (2, 4, 8, 16)
POOL_CH = 128
POOL_HIST = 16
D_FF = 4096
FF_CHUNK = 1024
EPS = 1e-6
PAST_LEN = 16384
GELU_C = math.sqrt(2.0 / math.pi)

VMEM_LIMIT_BYTES = 56 * 1024 * 1024


def _rms_norm(x, g):
    ms = jnp.mean(x * x, axis=-1, keepdims=True)
    return x * lax.rsqrt(ms + EPS) * g


def _full_spec(shape):
    return pl.BlockSpec(shape, lambda *_: (0,) * len(shape))


def _s5_prep_kernel(lam_re_ref, lam_im_ref, log_dt_ref, bt_re_ref, bt_im_ref,
                    ct_re_ref, ct_im_ref,
                    a_re_ref, a_im_ref, bm_ref, cmt_ref):
    lam_re = lam_re_ref[...]
    lam_im = lam_im_ref[...]
    dt = jnp.exp(log_dt_ref[...])
    mag = jnp.exp(lam_re * dt)
    ang = lam_im * dt
    a_re = mag * jnp.cos(ang)
    a_im = mag * jnp.sin(ang)
    a_re_ref[...] = a_re
    a_im_ref[...] = a_im
    n_re = a_re - 1.0
    n_im = a_im
    den = lam_re * lam_re + lam_im * lam_im
    k_re = (n_re * lam_re + n_im * lam_im) / den
    k_im = (n_im * lam_re - n_re * lam_im) / den
    bt_re = bt_re_ref[...]
    bt_im = bt_im_ref[...]
    bb_re = k_re * bt_re - k_im * bt_im
    bb_im = k_re * bt_im + k_im * bt_re
    ct_re = ct_re_ref[...]
    ct_im_neg = -ct_im_ref[...]

    rows = lax.broadcasted_iota(jnp.int32, (SLAB_U, SLAB_S), 0) // S5_H
    cols = lax.broadcasted_iota(jnp.int32, (SLAB_U, SLAB_S), 1) // S5_STATE
    diag = rows == cols

    def block_diag(m, j):
        sl = m[:, j * SLAB_S:(j + 1) * SLAB_S]
        tiled = jnp.concatenate([sl] * SLAB_GROUPS, axis=0)
        return jnp.where(diag, tiled, 0.0)

    for j in range(N_SLAB):
        bm_ref[j, :, :SLAB_S] = block_diag(bb_re, j).astype(BF16)
        bm_ref[j, :, SLAB_S:] = block_diag(bb_im, j).astype(BF16)
        cmt_ref[j, :, :SLAB_S] = block_diag(ct_re, j).astype(BF16)
        cmt_ref[j, :, SLAB_S:] = block_diag(ct_im_neg, j).astype(BF16)


def _s5_prep(lam_re, lam_im, log_dt, b_re, b_im, c_re, c_im):
    row = lambda a: a.reshape(1, N_STATE)
    bt = lambda b: jnp.transpose(b, (2, 0, 1)).reshape(S5_H, N_STATE)
    ct = lambda c: jnp.transpose(c, (1, 0, 2)).reshape(S5_H, N_STATE)
    ins = (row(lam_re), row(lam_im),
           row(jnp.broadcast_to(log_dt[:, None], (S5_GROUPS, S5_STATE))),
           bt(b_re), bt(b_im), ct(c_re), ct(c_im))
    out_shape = (
        jax.ShapeDtypeStruct((1, N_STATE), F32),
        jax.ShapeDtypeStruct((1, N_STATE), F32),
        jax.ShapeDtypeStruct((N_SLAB, SLAB_U, 2 * SLAB_S), BF16),
        jax.ShapeDtypeStruct((N_SLAB, SLAB_U, 2 * SLAB_S), BF16),
    )
    return pl.pallas_call(
        _s5_prep_kernel,
        out_shape=out_shape,
        name="s5_prep",
    )(*ins)


N_PARAM = 12


def _mixer_math(x, i, g_pre_ref, g_post_ref, win_ref, a_re_ref, a_im_ref, bm_ref,
                cmt_ref, d_ref, glu_ref, poolw_ref, pscale_ref, wout_ref,
                slab_ref, hr_s, hi_s, hist_s, *, batch, steps, start_pos):
    rows = batch * steps
    xn = _rms_norm(x, g_pre_ref[...]).astype(BF16)
    u = jnp.dot(xn, win_ref[...], preferred_element_type=F32)
    ua = u[:, :W_A]
    ub = u[:, W_A:]
    ua_bf = ua.astype(BF16)

    ys = []
    for j in range(N_SLAB):
        st = slice(j * SLAB_S, (j + 1) * SLAB_S)
        sb = slab_ref.at[j % 2]
        sb[...] = jnp.dot(ua_bf[:, j * SLAB_U:(j + 1) * SLAB_U], bm_ref[j],
                          preferred_element_type=F32)
        ar = jnp.broadcast_to(a_re_ref[:, st], (batch, SLAB_S))
        ai = jnp.broadcast_to(a_im_ref[:, st], (batch, SLAB_S))
        hr = hr_s[:, st]
        hi = hi_s[:, st]
        for t in range(steps):
            rt = slice(t * batch, (t + 1) * batch)
            nr = ar * hr - ai * hi + sb[rt, :SLAB_S]
            ni = ar * hi + ai * hr + sb[rt, SLAB_S:]
            sb[rt, :SLAB_S] = nr
            sb[rt, SLAB_S:] = ni
            hr, hi = nr, ni
        hr_s[:, st] = hr
        hi_s[:, st] = hi
        ys.append(lax.dot_general(sb[...].astype(BF16), cmt_ref[j],
                                  (((1,), (1,)), ((), ())),
                                  preferred_element_type=F32))

    y = jnp.concatenate(ys, axis=1) + d_ref[...] * ua
    y = y * (0.5 * (1.0 + jnp.tanh(GELU_C * (y + 0.044715 * (y * y * y)))))
    y_bf = y.astype(BF16)
    half = W_A // 2
    gate = jnp.concatenate(
        [jnp.dot(y_bf[:, :half], glu_ref[0], preferred_element_type=F32),
         jnp.dot(y_bf[:, half:], glu_ref[1], preferred_element_type=F32)], axis=1)
    ya = y * (1.0 / (1.0 + jnp.exp(-gate)))

    ext = jnp.concatenate([hist_s[...], ub], axis=0)
    n_ext = POOL_HIST * batch + rows
    hist_s[...] = ext[n_ext - POOL_HIST * batch:, :]
    t_loc = lax.broadcasted_iota(jnp.int32, (rows, 1), 0) // batch
    pos1 = t_loc + (start_pos + 1) + i * steps
    pooled = []
    for gi, w in enumerate(POOL_WINDOWS):
        s = ext[:, gi * POOL_CH:(gi + 1) * POOL_CH]
        span = 1
        while span < w:
            n = s.shape[0]
            s = s[span * batch:, :] + s[:n - span * batch, :]
            span *= 2
        win = s[s.shape[0] - rows:, :]
        count = jnp.minimum(pos1, w).astype(F32)
        pooled.append(win / count - ub[:, gi * POOL_CH:(gi + 1) * POOL_CH])
    pooled = jnp.concatenate(pooled, axis=1).astype(BF16)
    halfb = W_B // 2
    yb = jnp.concatenate(
        [jnp.dot(pooled[:, :halfb], poolw_ref[0], preferred_element_type=F32),
         jnp.dot(pooled[:, halfb:], poolw_ref[1], preferred_element_type=F32)], axis=1)
    yb = yb * pscale_ref[...]

    ycat = jnp.concatenate([ya, yb], axis=1).astype(BF16)
    mix = jnp.dot(ycat, wout_ref[...], preferred_element_type=F32)
    return x + _rms_norm(mix, g_post_ref[...])


def _mlp_math(h, g_pre_ref, g_post_ref, wup_ref, wdown_ref):
    hn = _rms_norm(h, g_pre_ref[...]).astype(BF16)
    acc = None
    for j in range(D_FF // FF_CHUNK):
        sl = slice(j * FF_CHUNK, (j + 1) * FF_CHUNK)
        up = jnp.dot(hn, wup_ref[:, sl], preferred_element_type=F32)
        up = jnp.maximum(up, 0.0)
        ff = (up * up).astype(BF16)
        part = jnp.dot(ff, wdown_ref[sl, :], preferred_element_type=F32)
        acc = part if acc is None else acc + part
    return h + _rms_norm(acc, g_post_ref[...])


def _pool_rows_from_hist(hist_ref, pool_ref, batch):
    for k in range(POOL_HIST - 1):
        pool_ref[:, k, :] = hist_ref[pl.ds((k + 1) * batch, batch), :]


def _mixer_scratch(batch, steps):
    return [
        pltpu.VMEM((2, batch * steps, 2 * SLAB_S), F32),
        pltpu.VMEM((batch, N_STATE), F32),
        pltpu.VMEM((batch, N_STATE), F32),
        pltpu.VMEM((POOL_HIST * batch, W_B), F32),
    ]


_VMEM_WHOLE = pl.BlockSpec(memory_space=pltpu.VMEM)
_HBM = pl.BlockSpec(memory_space=pl.ANY)
_COMPILER_PARAMS = pltpu.CompilerParams(dimension_semantics=("arbitrary",),
                                        vmem_limit_bytes=VMEM_LIMIT_BYTES)


def _mixer_kernel_seq(x_hbm, *refs, batch, steps):
    params = refs[:N_PARAM]
    h1_hbm, hr_out, hi_out, pool_out = refs[N_PARAM:N_PARAM + 4]
    scratch = refs[N_PARAM + 4:N_PARAM + 8]
    xbuf, hbuf, in_sem, out_sem = refs[N_PARAM + 8:]
    _, hr_s, hi_s, hist_s = scratch
    i = pl.program_id(0)
    n_chunks = pl.num_programs(0)
    slot = lax.rem(i, 2)

    def x_copy(b, chunk, sl):
        return pltpu.make_async_copy(
            x_hbm.at[b, pl.ds(chunk * steps, steps), :], xbuf.at[sl, :, b, :],
            in_sem.at[sl])

    def h_copy(b, chunk, sl):
        return pltpu.make_async_copy(
            hbuf.at[sl, :, b, :], h1_hbm.at[b, pl.ds(chunk * steps, steps), :],
            out_sem.at[sl])

    @pl.when(i == 0)
    def _():
        for b in range(batch):
            x_copy(b, 0, 0).start()
        hr_s[...] = jnp.zeros_like(hr_s)
        hi_s[...] = jnp.zeros_like(hi_s)
        hist_s[...] = jnp.zeros_like(hist_s)

    @pl.when(i + 1 < n_chunks)
    def _():
        for b in range(batch):
            x_copy(b, i + 1, 1 - slot).start()

    for b in range(batch):
        x_copy(b, i, slot).wait()
    x = xbuf[slot].reshape(steps * batch, D_MODEL)
    h1 = _mixer_math(x, i, *params, *scratch, batch=batch, steps=steps, start_pos=0)

    @pl.when(i >= 2)
    def _():
        for b in range(batch):
            h_copy(b, i - 2, slot).wait()

    hbuf[slot] = h1.reshape(steps, batch, D_MODEL)
    for b in range(batch):
        h_copy(b, i, slot).start()

    @pl.when(i == n_chunks - 1)
    def _():
        hr_out[...] = hr_s[...]
        hi_out[...] = hi_s[...]
        _pool_rows_from_hist(hist_s, pool_out, batch)

        @pl.when(i >= 1)
        def _():
            for b in range(batch):
                h_copy(b, i - 1, 1 - slot).wait()
        for b in range(batch):
            h_copy(b, i, slot).wait()


def _mixer_seq(x, params, *, steps):
    batch, seq, _ = x.shape
    assert len(params) == N_PARAM and seq % steps == 0
    return pl.pallas_call(
        functools.partial(_mixer_kernel_seq, batch=batch, steps=steps),
        out_shape=(jax.ShapeDtypeStruct(x.shape, F32),
                   jax.ShapeDtypeStruct((batch, N_STATE), F32),
                   jax.ShapeDtypeStruct((batch, N_STATE), F32),
                   jax.ShapeDtypeStruct((batch, POOL_HIST - 1, W_B), F32)),
        grid=(seq // steps,),
        in_specs=[_HBM] + [_VMEM_WHOLE] * N_PARAM,
        out_specs=(_HBM, _full_spec((batch, N_STATE)), _full_spec((batch, N_STATE)),
                   _full_spec((batch, POOL_HIST - 1, W_B))),
        scratch_shapes=_mixer_scratch(batch, steps) + [
            pltpu.VMEM((2, steps, batch, D_MODEL), F32),
            pltpu.VMEM((2, steps, batch, D_MODEL), F32),
            pltpu.SemaphoreType.DMA((2,)),
            pltpu.SemaphoreType.DMA((2,)),
        ],
        compiler_params=_COMPILER_PARAMS,
        name="mixer_seq",
    )(x, *params)


def _mlp_kernel(h_ref, g_pre_ref, g_post_ref, wup_ref, wdown_ref, o_ref):
    o_ref[...] = _mlp_math(h_ref[...], g_pre_ref, g_post_ref, wup_ref, wdown_ref)


def _mlp(h, g_pre, g_post, wup, wdown, *, block_rows):
    n_rows = h.shape[0]
    assert n_rows % block_rows == 0
    return pl.pallas_call(
        _mlp_kernel,
        out_shape=jax.ShapeDtypeStruct((n_rows, D_MODEL), F32),
        grid=(n_rows // block_rows,),
        in_specs=[
            pl.BlockSpec((block_rows, D_MODEL), lambda i: (i, 0)),
            _VMEM_WHOLE, _VMEM_WHOLE, _VMEM_WHOLE, _VMEM_WHOLE,
        ],
        out_specs=pl.BlockSpec((block_rows, D_MODEL), lambda i: (i, 0)),
        compiler_params=_COMPILER_PARAMS,
        name="mlp",
    )(h, g_pre, g_post, wup, wdown)


N_MLP = 4


def _layer_kernel_step(x_hbm, h0r_ref, h0i_ref, pool0_ref, *refs, batch, start_pos):
    params = refs[:N_PARAM]
    mlp_w = refs[N_PARAM:N_PARAM + N_MLP]
    y_hbm, hr_out, hi_out, pool_out = refs[N_PARAM + N_MLP:N_PARAM + N_MLP + 4]
    scratch = refs[N_PARAM + N_MLP + 4:N_PARAM + N_MLP + 8]
    xs, ys, sem = refs[N_PARAM + N_MLP + 8:]
    _, hr_s, hi_s, hist_s = scratch

    x_in = pltpu.make_async_copy(x_hbm.at[:, 0, :], xs, sem.at[0])
    x_in.start()
    hist_s[pl.ds(0, batch), :] = jnp.zeros((batch, W_B), F32)
    for k in range(POOL_HIST - 1):
        hist_s[pl.ds((k + 1) * batch, batch), :] = pool0_ref[:, k, :]
    hr_s[...] = h0r_ref[...]
    hi_s[...] = h0i_ref[...]
    x_in.wait()

    h1 = _mixer_math(xs[...], 0, *params, *scratch, batch=batch, steps=1,
                     start_pos=start_pos)
    ys[...] = _mlp_math(h1, *mlp_w)
    y_out = pltpu.make_async_copy(ys, y_hbm.at[:, 0, :], sem.at[1])
    y_out.start()
    hr_out[...] = hr_s[...]
    hi_out[...] = hi_s[...]
    _pool_rows_from_hist(hist_s, pool_out, batch)
    y_out.wait()


def _layer_step(x, h0r, h0i, pool0, params, mlp_w, *, start_pos):
    batch = x.shape[0]
    assert len(params) == N_PARAM and len(mlp_w) == N_MLP
    return pl.pallas_call(
        functools.partial(_layer_kernel_step, batch=batch, start_pos=start_pos),
        out_shape=(jax.ShapeDtypeStruct(x.shape, F32),
                   jax.ShapeDtypeStruct((batch, N_STATE), F32),
                   jax.ShapeDtypeStruct((batch, N_STATE), F32),
                   jax.ShapeDtypeStruct(pool0.shape, F32)),
        in_specs=[_HBM, _VMEM_WHOLE, _VMEM_WHOLE, _VMEM_WHOLE]
        + [_VMEM_WHOLE] * (N_PARAM + N_MLP),
        out_specs=(_HBM, _VMEM_WHOLE, _VMEM_WHOLE, _VMEM_WHOLE),
        scratch_shapes=_mixer_scratch(batch, 1) + [
            pltpu.VMEM((batch, D_MODEL), F32),
            pltpu.VMEM((batch, D_MODEL), F32),
            pltpu.SemaphoreType.DMA((2,)),
        ],
        compiler_params=pltpu.CompilerParams(vmem_limit_bytes=VMEM_LIMIT_BYTES),
        name="layer_step",
    )(x, h0r, h0i, pool0, *params, *mlp_w)


def _block_diag_pairs(w):
    n2, k, _ = w.shape
    w = w.reshape(n2 // 2, 2, k, k)
    z = jnp.zeros_like(w[:, 0])
    top = jnp.concatenate([w[:, 0], z], axis=2)
    bot = jnp.concatenate([z, w[:, 1]], axis=2)
    return jnp.concatenate([top, bot], axis=1)


def _glu_block_diag(w_glu):
    w = w_glu.reshape(2, 16, S5_H, S5_H)
    eye = jnp.eye(16, dtype=w.dtype)
    m = w[:, :, :, None, :] * eye[None, :, None, :, None]
    return m.reshape(2, 16 * S5_H, 16 * S5_H)


def kernel(x_prompt, x_sample, state_s5_re, state_s5_im, state_pool, norm_mix_pre, norm_mix_post, norm_mlp_pre, norm_mlp_post, w_in, s5_lambda_re, s5_lambda_im, s5_log_dt, s5_b_re, s5_b_im, s5_c_re, s5_c_im, s5_d, s5_w_glu, pool_w, pool_scale, w_out, w_mlp_up, w_mlp_down):
    bp, seq, _ = x_prompt.shape
    bs = x_sample.shape[0]

    a_re, a_im, bm, cmt = _s5_prep(s5_lambda_re, s5_lambda_im, s5_log_dt,
                                   s5_b_re, s5_b_im, s5_c_re, s5_c_im)
    row = lambda v: v.reshape(1, -1)
    params = [row(norm_mix_pre), row(norm_mix_post), w_in.astype(BF16), a_re, a_im,
              bm, cmt, row(s5_d), _glu_block_diag(s5_w_glu).astype(BF16),
              _block_diag_pairs(pool_w).astype(BF16), row(pool_scale),
              w_out.astype(BF16)]
    mlp_w = (row(norm_mlp_pre), row(norm_mlp_post),
             w_mlp_up.astype(BF16), w_mlp_down.astype(BF16))
    st = lambda a, b: a.reshape(b, S5_GROUPS, S5_STATE)

    h1p, hpr, hpi, pool_prompt = _mixer_seq(x_prompt, params, steps=128)
    y_prompt = _mlp(h1p.reshape(bp * seq, D_MODEL), *mlp_w,
                    block_rows=1024).reshape(bp, seq, D_MODEL)

    y_sample, hsr, hsi, pool_sample = _layer_step(
        x_sample, state_s5_re.reshape(bs, N_STATE), state_s5_im.reshape(bs, N_STATE),
        state_pool, params, mlp_w, start_pos=PAST_LEN)

    return (y_prompt, y_sample, st(hpr, bp), st(hpi, bp), pool_prompt,
            st(hsr, bs), st(hsi, bs), pool_sample)
```

```python
import functools
import math

import jax
import jax.numpy as jnp
from jax import lax
from jax.experimental import pallas as pl
from jax.experimental.pallas import tpu as pltpu

F32 = jnp.float32
BF16 = jnp.bfloat16

D_MODEL = 1024
W_A = 512
W_B = 512
S5_H = 16
S5_GROUPS = 32
S5_STATE = 64
N_STATE = S5_GROUPS * S5_STATE
N_SLAB = 4
SLAB_GROUPS = S5_GROUPS // N_SLAB
SLAB_U = SLAB_GROUPS * S5_H
SLAB_S = SLAB_GROUPS * S5_STATE
POOL_WINDOWS = (2, 4, 8, 16)
POOL_CH = 128
POOL_HIST = 16
D_FF = 4096
FF_CHUNK = 1024
EPS = 1e-6
PAST_LEN = 16384
GELU_C = math.sqrt(2.0 / math.pi)

VMEM_LIMIT_BYTES = 56 * 1024 * 1024


def _rms_norm(x, g):
    ms = jnp.mean(x * x, axis=-1, keepdims=True)
    return x * lax.rsqrt(ms + EPS) * g


def _full_spec(shape):
    return pl.BlockSpec(shape, lambda *_: (0,) * len(shape))


def _s5_prep_kernel(lam_re_ref, lam_im_ref, log_dt_ref, b_re_ref, b_im_ref,
                    c_re_ref, c_im_ref, wglu_ref, poolw_ref,
                    a_re_ref, a_im_ref, bm_ref, cmt_ref, glu_ref, poolbd_ref):
    lam_re = lam_re_ref[...]
    lam_im = lam_im_ref[...]
    eye = (lax.broadcasted_iota(jnp.int32, (S5_GROUPS, S5_GROUPS), 0)
           == lax.broadcasted_iota(jnp.int32, (S5_GROUPS, S5_GROUPS), 1))
    log_dt = jnp.sum(jnp.where(eye, log_dt_ref[...], 0.0), axis=1, keepdims=True)
    dt = jnp.exp(log_dt)
    mag = jnp.exp(lam_re * dt)
    ang = lam_im * dt
    a_re = mag * jnp.cos(ang)
    a_im = mag * jnp.sin(ang)
    lanes = lambda m: jnp.concatenate([m[g:g + 1, :] for g in range(S5_GROUPS)], axis=1)
    a_re_ref[...] = lanes(a_re)
    a_im_ref[...] = lanes(a_im)
    n_re = a_re - 1.0
    n_im = a_im
    den = lam_re * lam_re + lam_im * lam_im
    k_re = (n_re * lam_re + n_im * lam_im) / den
    k_im = (n_im * lam_re - n_re * lam_im) / den
    per_h = lambda m: jnp.broadcast_to(m[:, None, :], (S5_GROUPS, S5_H, S5_STATE)).reshape(
        S5_GROUPS * S5_H, S5_STATE)
    k_re = per_h(k_re)
    k_im = per_h(k_im)
    b_re = b_re_ref[...]
    b_im = b_im_ref[...]
    bb_re = k_re * b_re - k_im * b_im
    bb_im = k_re * b_im + k_im * b_re
    c_re = c_re_ref[...]
    c_im_neg = -c_im_ref[...]

    rows = lax.broadcasted_iota(jnp.int32, (SLAB_U, SLAB_S), 0) // S5_H
    cols = lax.broadcasted_iota(jnp.int32, (SLAB_U, SLAB_S), 1) // S5_STATE
    diag = rows == cols

    def block_diag(m, j):
        sl = m[j * SLAB_U:(j + 1) * SLAB_U, :]
        tiled = jnp.concatenate([sl] * SLAB_GROUPS, axis=1)
        return jnp.where(diag, tiled, 0.0)

    for j in range(N_SLAB):
        bm_ref[j, :, :SLAB_S] = block_diag(bb_re, j).astype(BF16)
        bm_ref[j, :, SLAB_S:] = block_diag(bb_im, j).astype(BF16)
        cmt_ref[j, :, :SLAB_S] = block_diag(c_re, j).astype(BF16)
        cmt_ref[j, :, SLAB_S:] = block_diag(c_im_neg, j).astype(BF16)

    half = W_A // 2
    g_rows = lax.broadcasted_iota(jnp.int32, (half, half), 0) // S5_H
    g_cols = lax.broadcasted_iota(jnp.int32, (half, half), 1) // S5_H
    for t in range(2):
        blk = wglu_ref[t * half:(t + 1) * half, :]
        tiled = jnp.concatenate([blk] * (half // S5_H), axis=1)
        glu_ref[t] = jnp.where(g_rows == g_cols, tiled, 0.0).astype(BF16)

    zeros = jnp.zeros((POOL_CH, POOL_CH), BF16)
    for t in range(2):
        poolbd_ref[t, :POOL_CH, :POOL_CH] = poolw_ref[2 * t].astype(BF16)
        poolbd_ref[t, :POOL_CH, POOL_CH:] = zeros
        poolbd_ref[t, POOL_CH:, :POOL_CH] = zeros
        poolbd_ref[t, POOL_CH:, POOL_CH:] = poolw_ref[2 * t + 1].astype(BF16)


def _s5_prep(lam_re, lam_im, log_dt, b_re, b_im, c_re, c_im, w_glu, pool_w):
    gh_p = lambda b: jnp.transpose(b, (0, 2, 1)).reshape(S5_GROUPS * S5_H, S5_STATE)
    ins = (lam_re, lam_im, log_dt.reshape(1, S5_GROUPS), gh_p(b_re), gh_p(b_im),
           c_re.reshape(S5_GROUPS * S5_H, S5_STATE),
           c_im.reshape(S5_GROUPS * S5_H, S5_STATE),
           w_glu.reshape(S5_GROUPS * S5_H, S5_H), pool_w)
    out_shape = (
        jax.ShapeDtypeStruct((1, N_STATE), F32),
        jax.ShapeDtypeStruct((1, N_STATE), F32),
        jax.ShapeDtypeStruct((N_SLAB, SLAB_U, 2 * SLAB_S), BF16),
        jax.ShapeDtypeStruct((N_SLAB, SLAB_U, 2 * SLAB_S), BF16),
        jax.ShapeDtypeStruct((2, W_A // 2, W_A // 2), BF16),
        jax.ShapeDtypeStruct((2, 2 * POOL_CH, 2 * POOL_CH), BF16),
    )
    return pl.pallas_call(
        _s5_prep_kernel,
        out_shape=out_shape,
        name="s5_prep",
    )(*ins)


N_PARAM = 12


def _mixer_math(x, i, g_pre_ref, g_post_ref, win_ref, a_re_ref, a_im_ref, bm_ref,
                cmt_ref, d_ref, glu_ref, poolw_ref, pscale_ref, wout_ref,
                slab_ref, hr_s, hi_s, hist_s, *, batch, steps, start_pos):
    rows = batch * steps
    xn = _rms_norm(x, g_pre_ref[...]).astype(BF16)
    u = jnp.dot(xn, win_ref[...], preferred_element_type=F32)
    ua = u[:, :W_A]
    ub = u[:, W_A:]
    ua_bf = ua.astype(BF16)

    ys = []
    for j in range(N_SLAB):
        st = slice(j * SLAB_S, (j + 1) * SLAB_S)
        sb = slab_ref.at[j % 2]
        sb[...] = jnp.dot(ua_bf[:, j * SLAB_U:(j + 1) * SLAB_U], bm_ref[j],
                          preferred_element_type=F32)
        ar = jnp.broadcast_to(a_re_ref[:, st], (batch, SLAB_S))
        ai = jnp.broadcast_to(a_im_ref[:, st], (batch, SLAB_S))
        hr = hr_s[:, st]
        hi = hi_s[:, st]
        for t in range(steps):
            rt = slice(t * batch, (t + 1) * batch)
            nr = ar * hr - ai * hi + sb[rt, :SLAB_S]
            ni = ar * hi + ai * hr + sb[rt, SLAB_S:]
            sb[rt, :SLAB_S] = nr
            sb[rt, SLAB_S:] = ni
            hr, hi = nr, ni
        hr_s[:, st] = hr
        hi_s[:, st] = hi
        ys.append(lax.dot_general(sb[...].astype(BF16), cmt_ref[j],
                                  (((1,), (1,)), ((), ())),
                                  preferred_element_type=F32))

    y = jnp.concatenate(ys, axis=1) + d_ref[...] * ua
    y = y * (0.5 * (1.0 + jnp.tanh(GELU_C * (y + 0.044715 * (y * y * y)))))
    y_bf = y.astype(BF16)
    half = W_A // 2
    gate = jnp.concatenate(
        [jnp.dot(y_bf[:, :half], glu_ref[0], preferred_element_type=F32),
         jnp.dot(y_bf[:, half:], glu_ref[1], preferred_element_type=F32)], axis=1)
    ya = y * (1.0 / (1.0 + jnp.exp(-gate)))

    ext = jnp.concatenate([hist_s[...], ub], axis=0)
    n_ext = POOL_HIST * batch + rows
    hist_s[...] = ext[n_ext - POOL_HIST * batch:, :]
    t_loc = lax.broadcasted_iota(jnp.int32, (rows, 1), 0) // batch
    pos1 = t_loc + (start_pos + 1) + i * steps
    pooled = []
    for gi, w in enumerate(POOL_WINDOWS):
        s = ext[:, gi * POOL_CH:(gi + 1) * POOL_CH]
        span = 1
        while span < w:
            n = s.shape[0]
            s = s[span * batch:, :] + s[:n - span * batch, :]
            span *= 2
        win = s[s.shape[0] - rows:, :]
        count = jnp.minimum(pos1, w).astype(F32)
        pooled.append(win / count - ub[:, gi * POOL_CH:(gi + 1) * POOL_CH])
    pooled = jnp.concatenate(pooled, axis=1).astype(BF16)
    halfb = W_B // 2
    yb = jnp.concatenate(
        [jnp.dot(pooled[:, :halfb], poolw_ref[0], preferred_element_type=F32),
         jnp.dot(pooled[:, halfb:], poolw_ref[1], preferred_element_type=F32)], axis=1)
    yb = yb * pscale_ref[...]

    ycat = jnp.concatenate([ya, yb], axis=1).astype(BF16)
    mix = jnp.dot(ycat, wout_ref[...], preferred_element_type=F32)
    return x + _rms_norm(mix, g_post_ref[...])


def _mlp_math(h, g_pre_ref, g_post_ref, wup_ref, wdown_ref):
    hn = _rms_norm(h, g_pre_ref[...]).astype(BF16)
    acc = None
    for j in range(D_FF // FF_CHUNK):
        sl = slice(j * FF_CHUNK, (j + 1) * FF_CHUNK)
        up = jnp.dot(hn, wup_ref[:, sl], preferred_element_type=F32)
        up = jnp.maximum(up, 0.0)
        ff = (up * up).astype(BF16)
        part = jnp.dot(ff, wdown_ref[sl, :], preferred_element_type=F32)
        acc = part if acc is None else acc + part
    return h + _rms_norm(acc, g_post_ref[...])


def _pool_rows_from_hist(hist_ref, pool_ref, batch):
    pool_ref[...] = hist_ref[pl.ds(batch, (POOL_HIST - 1) * batch), :].reshape(
        POOL_HIST - 1, batch, W_B)


def _mixer_scratch(batch, steps):
    return [
        pltpu.VMEM((2, batch * steps, 2 * SLAB_S), F32),
        pltpu.VMEM((batch, N_STATE), F32),
        pltpu.VMEM((batch, N_STATE), F32),
        pltpu.VMEM((POOL_HIST * batch, W_B), F32),
    ]


_VMEM_WHOLE = pl.BlockSpec(memory_space=pltpu.VMEM)
_HBM = pl.BlockSpec(memory_space=pl.ANY)
_COMPILER_PARAMS = pltpu.CompilerParams(dimension_semantics=("arbitrary",),
                                        vmem_limit_bytes=VMEM_LIMIT_BYTES)


def _mixer_kernel_seq(x_hbm, *refs, batch, steps):
    params = refs[:N_PARAM]
    h1_hbm, hr_out, hi_out, pool_out = refs[N_PARAM:N_PARAM + 4]
    scratch = refs[N_PARAM + 4:N_PARAM + 8]
    xbuf, hbuf, in_sem, out_sem = refs[N_PARAM + 8:]
    _, hr_s, hi_s, hist_s = scratch
    i = pl.program_id(0)
    n_chunks = pl.num_programs(0)
    slot = lax.rem(i, 2)

    def x_copy(b, chunk, sl):
        return pltpu.make_async_copy(
            x_hbm.at[b, pl.ds(chunk * steps, steps), :], xbuf.at[sl, :, b, :],
            in_sem.at[sl])

    def h_copy(b, chunk, sl):
        return pltpu.make_async_copy(
            hbuf.at[sl, :, b, :], h1_hbm.at[b, pl.ds(chunk * steps, steps), :],
            out_sem.at[sl])

    @pl.when(i == 0)
    def _():
        for b in range(batch):
            x_copy(b, 0, 0).start()
        hr_s[...] = jnp.zeros_like(hr_s)
        hi_s[...] = jnp.zeros_like(hi_s)
        hist_s[...] = jnp.zeros_like(hist_s)

    @pl.when(i + 1 < n_chunks)
    def _():
        for b in range(batch):
            x_copy(b, i + 1, 1 - slot).start()

    for b in range(batch):
        x_copy(b, i, slot).wait()
    x = xbuf[slot].reshape(steps * batch, D_MODEL)
    h1 = _mixer_math(x, i, *params, *scratch, batch=batch, steps=steps, start_pos=0)

    @pl.when(i >= 2)
    def _():
        for b in range(batch):
            h_copy(b, i - 2, slot).wait()

    hbuf[slot] = h1.reshape(steps, batch, D_MODEL)
    for b in range(batch):
        h_copy(b, i, slot).start()

    @pl.when(i == n_chunks - 1)
    def _():
        hr_out[...] = hr_s[...]
        hi_out[...] = hi_s[...]
        _pool_rows_from_hist(hist_s, pool_out, batch)

        @pl.when(i >= 1)
        def _():
            for b in range(batch):
                h_copy(b, i - 1, 1 - slot).wait()
        for b in range(batch):
            h_copy(b, i, slot).wait()


def _mixer_seq(x, params, *, steps):
    batch, seq, _ = x.shape
    assert len(params) == N_PARAM and seq % steps == 0
    return pl.pallas_call(
        functools.partial(_mixer_kernel_seq, batch=batch, steps=steps),
        out_shape=(jax.ShapeDtypeStruct(x.shape, F32),
                   jax.ShapeDtypeStruct((batch, N_STATE), F32),
                   jax.ShapeDtypeStruct((batch, N_STATE), F32),
                   jax.ShapeDtypeStruct((POOL_HIST - 1, batch, W_B), F32)),
        grid=(seq // steps,),
        in_specs=[_HBM] + [_VMEM_WHOLE] * N_PARAM,
        out_specs=(_HBM, _full_spec((batch, N_STATE)), _full_spec((batch, N_STATE)),
                   _full_spec((POOL_HIST - 1, batch, W_B))),
        scratch_shapes=_mixer_scratch(batch, steps) + [
            pltpu.VMEM((2, steps, batch, D_MODEL), F32),
            pltpu.VMEM((2, steps, batch, D_MODEL), F32),
            pltpu.SemaphoreType.DMA((2,)),
            pltpu.SemaphoreType.DMA((2,)),
        ],
        compiler_params=_COMPILER_PARAMS,
        name="mixer_seq",
    )(x, *params)


def _mlp_kernel(h_ref, g_pre_ref, g_post_ref, wup_ref, wdown_ref, o_ref):
    o_ref[...] = _mlp_math(h_ref[...], g_pre_ref, g_post_ref, wup_ref, wdown_ref)


def _mlp(h, g_pre, g_post, wup, wdown, *, block_rows):
    n_rows = h.shape[0]
    assert n_rows % block_rows == 0
    return pl.pallas_call(
        _mlp_kernel,
        out_shape=jax.ShapeDtypeStruct((n_rows, D_MODEL), F32),
        grid=(n_rows // block_rows,),
        in_specs=[
            pl.BlockSpec((block_rows, D_MODEL), lambda i: (i, 0)),
            _VMEM_WHOLE, _VMEM_WHOLE, _VMEM_WHOLE, _VMEM_WHOLE,
        ],
        out_specs=pl.BlockSpec((block_rows, D_MODEL), lambda i: (i, 0)),
        compiler_params=_COMPILER_PARAMS,
        name="mlp",
    )(h, g_pre, g_post, wup, wdown)


N_MLP = 4


def _layer_kernel_step(x_hbm, h0r_ref, h0i_ref, pool0_ref, *refs, batch, start_pos):
    params = refs[:N_PARAM]
    mlp_w = refs[N_PARAM:N_PARAM + N_MLP]
    y_hbm, hr_out, hi_out, pool_out = refs[N_PARAM + N_MLP:N_PARAM + N_MLP + 4]
    scratch = refs[N_PARAM + N_MLP + 4:N_PARAM + N_MLP + 8]
    xs, ys, sem = refs[N_PARAM + N_MLP + 8:]
    _, hr_s, hi_s, hist_s = scratch

    x_in = pltpu.make_async_copy(x_hbm.at[:, 0, :], xs, sem.at[0])
    x_in.start()
    hist_s[pl.ds(0, batch), :] = jnp.zeros((batch, W_B), F32)
    hist_s[pl.ds(batch, (POOL_HIST - 1) * batch), :] = pool0_ref[...].reshape(
        (POOL_HIST - 1) * batch, W_B)
    hr_s[...] = h0r_ref[...].T
    hi_s[...] = h0i_ref[...].T
    x_in.wait()

    h1 = _mixer_math(xs[...], 0, *params, *scratch, batch=batch, steps=1,
                     start_pos=start_pos)
    ys[...] = _mlp_math(h1, *mlp_w)
    y_out = pltpu.make_async_copy(ys, y_hbm.at[:, 0, :], sem.at[1])
    y_out.start()
    hr_out[...] = hr_s[...].T
    hi_out[...] = hi_s[...].T
    _pool_rows_from_hist(hist_s, pool_out, batch)
    y_out.wait()


def _layer_step(x, h0r, h0i, pool0, params, mlp_w, *, start_pos):
    batch = x.shape[0]
    assert len(params) == N_PARAM and len(mlp_w) == N_MLP
    return pl.pallas_call(
        functools.partial(_layer_kernel_step, batch=batch, start_pos=start_pos),
        out_shape=(jax.ShapeDtypeStruct(x.shape, F32),
                   jax.ShapeDtypeStruct((N_STATE, batch), F32),
                   jax.ShapeDtypeStruct((N_STATE, batch), F32),
                   jax.ShapeDtypeStruct(pool0.shape, F32)),
        in_specs=[_HBM, _VMEM_WHOLE, _VMEM_WHOLE, _VMEM_WHOLE]
        + [_VMEM_WHOLE] * (N_PARAM + N_MLP),
        out_specs=(_HBM, _VMEM_WHOLE, _VMEM_WHOLE, _VMEM_WHOLE),
        scratch_shapes=_mixer_scratch(batch, 1) + [
            pltpu.VMEM((batch, D_MODEL), F32),
            pltpu.VMEM((batch, D_MODEL), F32),
            pltpu.SemaphoreType.DMA((2,)),
        ],
        compiler_params=pltpu.CompilerParams(vmem_limit_bytes=VMEM_LIMIT_BYTES),
        name="layer_step",
    )(x, h0r, h0i, pool0, *params, *mlp_w)


def kernel(x_prompt, x_sample, state_s5_re, state_s5_im, state_pool, norm_mix_pre, norm_mix_post, norm_mlp_pre, norm_mlp_post, w_in, s5_lambda_re, s5_lambda_im, s5_log_dt, s5_b_re, s5_b_im, s5_c_re, s5_c_im, s5_d, s5_w_glu, pool_w, pool_scale, w_out, w_mlp_up, w_mlp_down):
    bp, seq, _ = x_prompt.shape
    bs = x_sample.shape[0]

    a_re, a_im, bm, cmt, glu, poolbd = _s5_prep(
        s5_lambda_re, s5_lambda_im, s5_log_dt, s5_b_re, s5_b_im, s5_c_re, s5_c_im,
        s5_w_glu, pool_w)
    row = lambda v: v.reshape(1, -1)
    params = [row(norm_mix_pre), row(norm_mix_post), w_in.astype(BF16), a_re, a_im,
              bm, cmt, row(s5_d), glu, poolbd, row(pool_scale),
              w_out.astype(BF16)]
    mlp_w = (row(norm_mlp_pre), row(norm_mlp_post),
             w_mlp_up.astype(BF16), w_mlp_down.astype(BF16))
    st = lambda a, b: a.reshape(b, S5_GROUPS, S5_STATE)

    st_in = lambda a: jnp.transpose(a, (1, 2, 0)).reshape(N_STATE, bs)
    st_out = lambda a: jnp.transpose(a.reshape(S5_GROUPS, S5_STATE, bs), (2, 0, 1))
    tbc = lambda a: jnp.transpose(a, (1, 0, 2))

    h1p, hpr, hpi, pool_p = _mixer_seq(x_prompt, params, steps=128)
    y_prompt = _mlp(h1p.reshape(bp * seq, D_MODEL), *mlp_w,
                    block_rows=1024).reshape(bp, seq, D_MODEL)

    y_sample, hsr, hsi, pool_s = _layer_step(
        x_sample, st_in(state_s5_re), st_in(state_s5_im), tbc(state_pool),
        params, mlp_w, start_pos=PAST_LEN)

    return (y_prompt, y_sample, st(hpr, bp), st(hpi, bp), tbc(pool_p),
            st_out(hsr), st_out(hsi), tbc(pool_s))
```

```python
import functools
import math

import jax
import jax.numpy as jnp
from jax import lax
from jax.experimental import pallas as pl
from jax.experimental.pallas import tpu as pltpu

F32 = jnp.float32
BF16 = jnp.bfloat16

D_MODEL = 1024
W_A = 512
W_B = 512
S5_H = 16
S5_GROUPS = 32
S5_STATE = 64
N_STATE = S5_GROUPS * S5_STATE
N_SLAB = 4
SLAB_GROUPS = S5_GROUPS // N_SLAB
SLAB_U = SLAB_GROUPS * S5_H
SLAB_S = SLAB_GROUPS * S5_STATE
POOL_WINDOWS = (2, 4, 8, 16)
POOL_CH = 128
POOL_HIST = 16
D_FF = 4096
FF_CHUNK = 1024
EPS = 1e-6
PAST_LEN = 16384
GELU_C = math.sqrt(2.0 / math.pi)

VMEM_LIMIT_BYTES = 56 * 1024 * 1024


def _rms_norm(x, g):
    ms = jnp.mean(x * x, axis=-1, keepdims=True)
    return x * lax.rsqrt(ms + EPS) * g


def _full_spec(shape):
    return pl.BlockSpec(shape, lambda *_: (0,) * len(shape))


def _s5_prep_kernel(lam_re_ref, lam_im_ref, log_dt_ref, b_re_ref, b_im_ref,
                    c_re_ref, c_im_ref, wglu_ref, poolw_ref,
                    a_re_ref, a_im_ref, bm_ref, cmt_ref, glu_ref, poolbd_ref):
    lam_re = lam_re_ref[...]
    lam_im = lam_im_ref[...]
    eye = (lax.broadcasted_iota(jnp.int32, (S5_GROUPS, S5_GROUPS), 0)
           == lax.broadcasted_iota(jnp.int32, (S5_GROUPS, S5_GROUPS), 1))
    log_dt = jnp.sum(jnp.where(eye, log_dt_ref[...], 0.0), axis=1, keepdims=True)
    dt = jnp.exp(log_dt)
    mag = jnp.exp(lam_re * dt)
    ang = lam_im * dt
    a_re = mag * jnp.cos(ang)
    a_im = mag * jnp.sin(ang)
    lanes = lambda m: jnp.concatenate([m[g:g + 1, :] for g in range(S5_GROUPS)], axis=1)
    a_re_ref[...] = lanes(a_re)
    a_im_ref[...] = lanes(a_im)
    n_re = a_re - 1.0
    n_im = a_im
    den = lam_re * lam_re + lam_im * lam_im
    k_re = (n_re * lam_re + n_im * lam_im) / den
    k_im = (n_im * lam_re - n_re * lam_im) / den
    per_h = lambda m: jnp.broadcast_to(m[:, None, :], (S5_GROUPS, S5_H, S5_STATE)).reshape(
        S5_GROUPS * S5_H, S5_STATE)
    k_re = per_h(k_re)
    k_im = per_h(k_im)
    b_re = b_re_ref[...]
    b_im = b_im_ref[...]
    bb_re = k_re * b_re - k_im * b_im
    bb_im = k_re * b_im + k_im * b_re
    c_re = c_re_ref[...]
    c_im_neg = -c_im_ref[...]

    rows = lax.broadcasted_iota(jnp.int32, (SLAB_U, SLAB_S), 0) // S5_H
    cols = lax.broadcasted_iota(jnp.int32, (SLAB_U, SLAB_S), 1) // S5_STATE
    diag = rows == cols

    def block_diag(m, j):
        sl = m[j * SLAB_U:(j + 1) * SLAB_U, :]
        tiled = jnp.concatenate([sl] * SLAB_GROUPS, axis=1)
        return jnp.where(diag, tiled, 0.0)

    for j in range(N_SLAB):
        bm_ref[j, :, :SLAB_S] = block_diag(bb_re, j).astype(BF16)
        bm_ref[j, :, SLAB_S:] = block_diag(bb_im, j).astype(BF16)
        cmt_ref[j, :, :SLAB_S] = block_diag(c_re, j).astype(BF16)
        cmt_ref[j, :, SLAB_S:] = block_diag(c_im_neg, j).astype(BF16)

    half = W_A // 2
    g_rows = lax.broadcasted_iota(jnp.int32, (half, half), 0) // S5_H
    g_cols = lax.broadcasted_iota(jnp.int32, (half, half), 1) // S5_H
    for t in range(2):
        blk = wglu_ref[t * half:(t + 1) * half, :]
        tiled = jnp.concatenate([blk] * (half // S5_H), axis=1)
        glu_ref[t] = jnp.where(g_rows == g_cols, tiled, 0.0).astype(BF16)

    zeros = jnp.zeros((POOL_CH, POOL_CH), BF16)
    for t in range(2):
        poolbd_ref[t, :POOL_CH, :POOL_CH] = poolw_ref[2 * t].astype(BF16)
        poolbd_ref[t, :POOL_CH, POOL_CH:] = zeros
        poolbd_ref[t, POOL_CH:, :POOL_CH] = zeros
        poolbd_ref[t, POOL_CH:, POOL_CH:] = poolw_ref[2 * t + 1].astype(BF16)


def _s5_prep(lam_re, lam_im, log_dt, b_re, b_im, c_re, c_im, w_glu, pool_w):
    gh_p = lambda b: jnp.transpose(b, (0, 2, 1)).reshape(S5_GROUPS * S5_H, S5_STATE)
    ins = (lam_re, lam_im, log_dt.reshape(1, S5_GROUPS), gh_p(b_re), gh_p(b_im),
           c_re.reshape(S5_GROUPS * S5_H, S5_STATE),
           c_im.reshape(S5_GROUPS * S5_H, S5_STATE),
           w_glu.reshape(S5_GROUPS * S5_H, S5_H), pool_w)
    out_shape = (
        jax.ShapeDtypeStruct((1, N_STATE), F32),
        jax.ShapeDtypeStruct((1, N_STATE), F32),
        jax.ShapeDtypeStruct((N_SLAB, SLAB_U, 2 * SLAB_S), BF16),
        jax.ShapeDtypeStruct((N_SLAB, SLAB_U, 2 * SLAB_S), BF16),
        jax.ShapeDtypeStruct((2, W_A // 2, W_A // 2), BF16),
        jax.ShapeDtypeStruct((2, 2 * POOL_CH, 2 * POOL_CH), BF16),
    )
    return pl.pallas_call(
        _s5_prep_kernel,
        out_shape=out_shape,
        name="s5_prep",
    )(*ins)


def _mixer_math(x, i, g_pre_ref, g_post_ref, win_ref, a_re_ref, a_im_ref, bm_ref,
                cmt_ref, d_ref, glu_ref, poolw_ref, pscale_ref, wout_ref,
                slab_ref, hr_s, hi_s, hist_s, *, batch, steps, start_pos):
    rows = batch * steps
    xn = _rms_norm(x, g_pre_ref[...]).astype(BF16)
    u = jnp.dot(xn, win_ref[...].reshape(D_MODEL, D_MODEL),
                preferred_element_type=F32)
    ua = u[:, :W_A]
    ub = u[:, W_A:]
    ua_bf = ua.astype(BF16)

    ys = []
    for j in range(N_SLAB):
        st = slice(j * SLAB_S, (j + 1) * SLAB_S)
        sb = slab_ref.at[j % 2]
        sb[...] = jnp.dot(ua_bf[:, j * SLAB_U:(j + 1) * SLAB_U], bm_ref[j],
                          preferred_element_type=F32)
        ar = jnp.broadcast_to(a_re_ref[:, st], (batch, SLAB_S))
        ai = jnp.broadcast_to(a_im_ref[:, st], (batch, SLAB_S))
        hr = hr_s[:, st]
        hi = hi_s[:, st]
        for t in range(steps):
            rt = slice(t * batch, (t + 1) * batch)
            nr = ar * hr - ai * hi + sb[rt, :SLAB_S]
            ni = ar * hi + ai * hr + sb[rt, SLAB_S:]
            sb[rt, :SLAB_S] = nr
            sb[rt, SLAB_S:] = ni
            hr, hi = nr, ni
        hr_s[:, st] = hr
        hi_s[:, st] = hi
        ys.append(lax.dot_general(sb[...].astype(BF16), cmt_ref[j],
                                  (((1,), (1,)), ((), ())),
                                  preferred_element_type=F32))

    y = jnp.concatenate(ys, axis=1) + d_ref[...] * ua
    y = y * (0.5 * (1.0 + jnp.tanh(GELU_C * (y + 0.044715 * (y * y * y)))))
    y_bf = y.astype(BF16)
    half = W_A // 2
    gate = jnp.concatenate(
        [jnp.dot(y_bf[:, :half], glu_ref[0], preferred_element_type=F32),
         jnp.dot(y_bf[:, half:], glu_ref[1], preferred_element_type=F32)], axis=1)
    ya = y * (1.0 / (1.0 + jnp.exp(-gate)))

    hist = hist_s[...].reshape(POOL_HIST * batch, W_B)
    ext = jnp.concatenate([hist, ub], axis=0)
    n_ext = POOL_HIST * batch + rows
    hist_s[...] = ext[n_ext - POOL_HIST * batch:, :].reshape(POOL_HIST, batch, W_B)
    t_loc = lax.broadcasted_iota(jnp.int32, (rows, 1), 0) // batch
    pos1 = t_loc + (start_pos + 1) + i * steps
    pooled = []
    for gi, w in enumerate(POOL_WINDOWS):
        s = ext[:, gi * POOL_CH:(gi + 1) * POOL_CH]
        span = 1
        while span < w:
            n = s.shape[0]
            s = s[span * batch:, :] + s[:n - span * batch, :]
            span *= 2
        win = s[s.shape[0] - rows:, :]
        count = jnp.minimum(pos1, w).astype(F32)
        pooled.append(win / count - ub[:, gi * POOL_CH:(gi + 1) * POOL_CH])
    pooled = jnp.concatenate(pooled, axis=1).astype(BF16)
    halfb = W_B // 2
    yb = jnp.concatenate(
        [jnp.dot(pooled[:, :halfb], poolw_ref[0], preferred_element_type=F32),
         jnp.dot(pooled[:, halfb:], poolw_ref[1], preferred_element_type=F32)], axis=1)
    yb = yb * pscale_ref[...]

    ycat = jnp.concatenate([ya, yb], axis=1).astype(BF16)
    mix = jnp.dot(ycat, wout_ref[...].reshape(D_MODEL, D_MODEL),
                  preferred_element_type=F32)
    return x + _rms_norm(mix, g_post_ref[...])


def _mlp_math(h, g_pre_ref, g_post_ref, wup_ref, wdown_ref):
    hn = _rms_norm(h, g_pre_ref[...]).astype(BF16)
    acc = None
    for j in range(D_FF // FF_CHUNK):
        sl = slice(j * FF_CHUNK, (j + 1) * FF_CHUNK)
        up = jnp.dot(hn, wup_ref[:, sl], preferred_element_type=F32)
        up = jnp.maximum(up, 0.0)
        ff = (up * up).astype(BF16)
        part = jnp.dot(ff, wdown_ref[sl, :], preferred_element_type=F32)
        acc = part if acc is None else acc + part
    return h + _rms_norm(acc, g_post_ref[...])


def _mixer_scratch(batch, steps):
    return [
        pltpu.VMEM((2, batch * steps, 2 * SLAB_S), F32),
        pltpu.VMEM((batch, N_STATE), F32),
        pltpu.VMEM((batch, N_STATE), F32),
        pltpu.VMEM((POOL_HIST, batch, W_B), F32),
    ]


_VMEM_WHOLE = pl.BlockSpec(memory_space=pltpu.VMEM)
_HBM = pl.BlockSpec(memory_space=pl.ANY)
_COMPILER_PARAMS = pltpu.CompilerParams(dimension_semantics=("arbitrary",),
                                        vmem_limit_bytes=VMEM_LIMIT_BYTES)


def _cast_weight(w_hbm, dst_ref, stage_ref, sem, row_chunk):
    n = w_hbm.shape[0] // row_chunk

    def copy(c):
        return pltpu.make_async_copy(w_hbm.at[pl.ds(c * row_chunk, row_chunk)],
                                     stage_ref.at[c % 2], sem.at[c % 2])

    copy(0).start()
    for c in range(n):
        if c + 1 < n:
            copy(c + 1).start()
        copy(c).wait()
        dst_ref[pl.ds(c * row_chunk, row_chunk)] = stage_ref[c % 2].astype(BF16)


N_SEQ_PARAM = 10


def _mixer_kernel(x_hbm, xs_hbm, h0r_ref, h0i_ref, pool0_hbm, win_hbm, wout_hbm,
                  *refs, batch, steps, batch_s, start_pos_s):
    g_pre, g_post, a_re, a_im, bm, cmt, d, glu, poolbd, pscale = refs[:N_SEQ_PARAM]
    (h1_hbm, hr_out, hi_out, pool_out, h1s_out, hrs_out, his_out,
     pools_hbm) = refs[N_SEQ_PARAM:N_SEQ_PARAM + 8]
    (slab, hr_s, hi_s, hist_s, hrs_s, his_s, hists_s, xs, win_bf, wout_bf, xbuf, hbuf,
     in_sem, out_sem, s_sem, w_sem) = refs[N_SEQ_PARAM + 8:]
    params = (g_pre, g_post, win_bf, a_re, a_im, bm, cmt, d, glu, poolbd, pscale, wout_bf)
    i = pl.program_id(0)
    n_chunks = pl.num_programs(0)
    slot = lax.rem(i, 2)

    def x_copy(b, chunk, sl):
        return pltpu.make_async_copy(
            x_hbm.at[b, pl.ds(chunk * steps, steps), :], xbuf.at[sl, :, b, :],
            in_sem.at[sl])

    def h_copy(b, chunk, sl):
        return pltpu.make_async_copy(
            hbuf.at[sl, :, b, :], h1_hbm.at[b, pl.ds(chunk * steps, steps), :],
            out_sem.at[sl])

    xs_copy = pltpu.make_async_copy(xs_hbm.at[:, 0, :], xs, s_sem.at[0])
    pool_in = pltpu.make_async_copy(pool0_hbm, hists_s.at[pl.ds(1, POOL_HIST - 1)],
                                    s_sem.at[1])
    pool_o = pltpu.make_async_copy(hists_s.at[pl.ds(1, POOL_HIST - 1)], pools_hbm,
                                   s_sem.at[1])

    @pl.when(i == 0)
    def _():
        for b in range(batch):
            x_copy(b, 0, 0).start()
        xs_copy.start()
        pool_in.start()
        hr_s[...] = jnp.zeros_like(hr_s)
        hi_s[...] = jnp.zeros_like(hi_s)
        hist_s[...] = jnp.zeros_like(hist_s)
        _cast_weight(win_hbm, win_bf, hbuf, w_sem, steps)
        _cast_weight(wout_hbm, wout_bf, hbuf, w_sem, steps)

    @pl.when(i + 1 < n_chunks)
    def _():
        for b in range(batch):
            x_copy(b, i + 1, 1 - slot).start()

    for b in range(batch):
        x_copy(b, i, slot).wait()
    x = xbuf[slot].reshape(steps * batch, D_MODEL)
    h1 = _mixer_math(x, i, *params, slab, hr_s, hi_s, hist_s,
                     batch=batch, steps=steps, start_pos=0)

    @pl.when(i >= 2)
    def _():
        for b in range(batch):
            h_copy(b, i - 2, slot).wait()

    hbuf[slot] = h1.reshape(steps, batch, D_MODEL)
    for b in range(batch):
        h_copy(b, i, slot).start()

    @pl.when(i == n_chunks - 1)
    def _():
        hr_out[...] = hr_s[...]
        hi_out[...] = hi_s[...]
        pool_out[...] = hist_s[pl.ds(1, POOL_HIST - 1)]

        xs_copy.wait()
        pool_in.wait()
        hists_s[0] = jnp.zeros((batch_s, W_B), F32)
        hrs_s[...] = h0r_ref[...].T
        his_s[...] = h0i_ref[...].T
        h1s_out[...] = _mixer_math(
            xs[...], 0, *params, slab.at[:, pl.ds(0, batch_s), :], hrs_s, his_s, hists_s,
            batch=batch_s, steps=1, start_pos=start_pos_s)
        pool_o.start()
        hrs_out[...] = hrs_s[...].T
        his_out[...] = his_s[...].T

        @pl.when(i >= 1)
        def _():
            for b in range(batch):
                h_copy(b, i - 1, 1 - slot).wait()
        for b in range(batch):
            h_copy(b, i, slot).wait()
        pool_o.wait()


def _mixer(x, xs, h0r_t, h0i_t, pool0_t, w_in, w_out, seq_params, *, steps, start_pos_s):
    batch, seq, _ = x.shape
    batch_s = xs.shape[0]
    assert len(seq_params) == N_SEQ_PARAM and seq % steps == 0 and batch_s <= steps * batch
    f32 = lambda *shape: jax.ShapeDtypeStruct(shape, F32)
    w3 = lambda w: w.reshape(D_MODEL // batch, batch, D_MODEL)
    return pl.pallas_call(
        functools.partial(_mixer_kernel, batch=batch, steps=steps, batch_s=batch_s,
                          start_pos_s=start_pos_s),
        out_shape=(f32(*x.shape), f32(batch, N_STATE), f32(batch, N_STATE),
                   f32(POOL_HIST - 1, batch, W_B),
                   f32(batch_s, D_MODEL), f32(N_STATE, batch_s), f32(N_STATE, batch_s),
                   f32(*pool0_t.shape)),
        grid=(seq // steps,),
        in_specs=[_HBM, _HBM, _VMEM_WHOLE, _VMEM_WHOLE, _HBM, _HBM, _HBM]
        + [_VMEM_WHOLE] * N_SEQ_PARAM,
        out_specs=(_HBM, _full_spec((batch, N_STATE)), _full_spec((batch, N_STATE)),
                   _full_spec((POOL_HIST - 1, batch, W_B)),
                   _full_spec((batch_s, D_MODEL)), _full_spec((N_STATE, batch_s)),
                   _full_spec((N_STATE, batch_s)), _HBM),
        scratch_shapes=_mixer_scratch(batch, steps) + [
            pltpu.VMEM((batch_s, N_STATE), F32),
            pltpu.VMEM((batch_s, N_STATE), F32),
            pltpu.VMEM((POOL_HIST, batch_s, W_B), F32),
            pltpu.VMEM((batch_s, D_MODEL), F32),
            pltpu.VMEM((D_MODEL // batch, batch, D_MODEL), BF16),
            pltpu.VMEM((D_MODEL // batch, batch, D_MODEL), BF16),
            pltpu.VMEM((2, steps, batch, D_MODEL), F32),
            pltpu.VMEM((2, steps, batch, D_MODEL), F32),
            pltpu.SemaphoreType.DMA((2,)),
            pltpu.SemaphoreType.DMA((2,)),
            pltpu.SemaphoreType.DMA((2,)),
            pltpu.SemaphoreType.DMA((2,)),
        ],
        compiler_params=_COMPILER_PARAMS,
        name="mixer",
    )(x, xs, h0r_t, h0i_t, pool0_t, w3(w_in), w3(w_out), *seq_params)


MLP_STAGE_ROWS = 128


def _mlp_kernel(h_ref, hs_ref, g_pre_ref, g_post_ref, wup_hbm, wdown_hbm,
                o_ref, ys_hbm, wup_bf, wdown_bf, stage_up, stage_down, ys, w_sem, y_sem):
    i = pl.program_id(0)

    @pl.when(i == 0)
    def _():
        _cast_weight(wup_hbm, wup_bf, stage_up, w_sem, MLP_STAGE_ROWS)
        _cast_weight(wdown_hbm, wdown_bf, stage_down, w_sem, 4 * MLP_STAGE_ROWS)

    o_ref[...] = _mlp_math(h_ref[...], g_pre_ref, g_post_ref, wup_bf, wdown_bf)

    @pl.when(i == pl.num_programs(0) - 1)
    def _():
        ys[...] = _mlp_math(hs_ref[...], g_pre_ref, g_post_ref, wup_bf, wdown_bf)
        y_out = pltpu.make_async_copy(ys, ys_hbm.at[:, 0, :], y_sem.at[0])
        y_out.start()
        y_out.wait()


def _mlp(h, hs, g_pre, g_post, wup, wdown, *, block_rows):
    n_rows = h.shape[0]
    batch_s = hs.shape[0]
    assert n_rows % block_rows == 0
    return pl.pallas_call(
        _mlp_kernel,
        out_shape=(jax.ShapeDtypeStruct((n_rows, D_MODEL), F32),
                   jax.ShapeDtypeStruct((batch_s, 1, D_MODEL), F32)),
        grid=(n_rows // block_rows,),
        in_specs=[
            pl.BlockSpec((block_rows, D_MODEL), lambda i: (i, 0)),
            _VMEM_WHOLE, _VMEM_WHOLE, _VMEM_WHOLE, _HBM, _HBM,
        ],
        out_specs=(pl.BlockSpec((block_rows, D_MODEL), lambda i: (i, 0)), _HBM),
        scratch_shapes=[
            pltpu.VMEM((D_MODEL, D_FF), BF16),
            pltpu.VMEM((D_FF, D_MODEL), BF16),
            pltpu.VMEM((2, MLP_STAGE_ROWS, D_FF), F32),
            pltpu.VMEM((2, 4 * MLP_STAGE_ROWS, D_MODEL), F32),
            pltpu.VMEM((batch_s, D_MODEL), F32),
            pltpu.SemaphoreType.DMA((2,)),
            pltpu.SemaphoreType.DMA((1,)),
        ],
        compiler_params=_COMPILER_PARAMS,
        name="mlp",
    )(h, hs, g_pre, g_post, wup, wdown)


def kernel(x_prompt, x_sample, state_s5_re, state_s5_im, state_pool, norm_mix_pre, norm_mix_post, norm_mlp_pre, norm_mlp_post, w_in, s5_lambda_re, s5_lambda_im, s5_log_dt, s5_b_re, s5_b_im, s5_c_re, s5_c_im, s5_d, s5_w_glu, pool_w, pool_scale, w_out, w_mlp_up, w_mlp_down):
    bp, seq, _ = x_prompt.shape
    bs = x_sample.shape[0]

    a_re, a_im, bm, cmt, glu, poolbd = _s5_prep(
        s5_lambda_re, s5_lambda_im, s5_log_dt, s5_b_re, s5_b_im, s5_c_re, s5_c_im,
        s5_w_glu, pool_w)
    row = lambda v: v.reshape(1, -1)
    seq_params = [row(norm_mix_pre), row(norm_mix_post), a_re, a_im, bm, cmt, row(s5_d),
                  glu, poolbd, row(pool_scale)]

    st_in = lambda a: jnp.transpose(a, (1, 2, 0)).reshape(N_STATE, bs)
    st_out = lambda a: jnp.transpose(a.reshape(S5_GROUPS, S5_STATE, bs), (2, 0, 1))
    st = lambda a: a.reshape(bp, S5_GROUPS, S5_STATE)
    tbc = lambda a: jnp.transpose(a, (1, 0, 2))

    h1p, hpr, hpi, pool_p, h1s, hsr, hsi, pool_s = _mixer(
        x_prompt, x_sample, st_in(state_s5_re), st_in(state_s5_im), tbc(state_pool),
        w_in, w_out, seq_params, steps=128, start_pos_s=PAST_LEN)
    yp, y_sample = _mlp(h1p.reshape(bp * seq, D_MODEL), h1s, row(norm_mlp_pre),
                        row(norm_mlp_post), w_mlp_up, w_mlp_down, block_rows=1024)

    return (yp.reshape(bp, seq, D_MODEL), y_sample, st(hpr), st(hpi), tbc(pool_p),
            st_out(hsr), st_out(hsi), tbc(pool_s))
```

```python
import functools
import math

import jax
import jax.numpy as jnp
from jax import lax
from jax.experimental import pallas as pl
from jax.experimental.pallas import tpu as pltpu

F32 = jnp.float32
BF16 = jnp.bfloat16

D_MODEL = 1024
W_A = 512
W_B = 512
S5_H = 16
S5_GROUPS = 32
S5_STATE = 64
N_STATE = S5_GROUPS * S5_STATE
N_SLAB = 4
SLAB_GROUPS = S5_GROUPS // N_SLAB
SLAB_U = SLAB_GROUPS * S5_H
SLAB_S = SLAB_GROUPS * S5_STATE
POOL_WINDOWS = (2, 4, 8, 16)
POOL_CH = 128
POOL_HIST = 16
D_FF = 4096
FF_CHUNK = 1024
EPS = 1e-6
PAST_LEN = 16384
GELU_C = math.sqrt(2.0 / math.pi)

VMEM_LIMIT_BYTES = 56 * 1024 * 1024


def _rms_norm(x, g):
    ms = jnp.mean(x * x, axis=-1, keepdims=True)
    return x * lax.rsqrt(ms + EPS) * g


def _full_spec(shape):
    return pl.BlockSpec(shape, lambda *_: (0,) * len(shape))


def _s5_prep_kernel(lam_re_ref, lam_im_ref, log_dt_ref, b_re_ref, b_im_ref,
                    c_re_ref, c_im_ref, wglu_ref, poolw_ref,
                    a_re_ref, a_im_ref, bm_ref, cmt_ref, glu_ref, poolbd_ref):
    lam_re = lam_re_ref[...]
    lam_im = lam_im_ref[...]
    eye = (lax.broadcasted_iota(jnp.int32, (S5_GROUPS, S5_GROUPS), 0)
           == lax.broadcasted_iota(jnp.int32, (S5_GROUPS, S5_GROUPS), 1))
    log_dt = jnp.sum(jnp.where(eye, log_dt_ref[...], 0.0), axis=1, keepdims=True)
    dt = jnp.exp(log_dt)
    mag = jnp.exp(lam_re * dt)
    ang = lam_im * dt
    a_re = mag * jnp.cos(ang)
    a_im = mag * jnp.sin(ang)
    lanes = lambda m: jnp.concatenate([m[g:g + 1, :] for g in range(S5_GROUPS)], axis=1)
    a_re_ref[...] = lanes(a_re)
    a_im_ref[...] = lanes(a_im)
    n_re = a_re - 1.0
    n_im = a_im
    den = lam_re * lam_re + lam_im * lam_im
    k_re = (n_re * lam_re + n_im * lam_im) / den
    k_im = (n_im * lam_re - n_re * lam_im) / den
    per_h = lambda m: jnp.broadcast_to(m[:, None, :], (S5_GROUPS, S5_H, S5_STATE)).reshape(
        S5_GROUPS * S5_H, S5_STATE)
    k_re = per_h(k_re)
    k_im = per_h(k_im)
    b_re = b_re_ref[...]
    b_im = b_im_ref[...]
    bb_re = k_re * b_re - k_im * b_im
    bb_im = k_re * b_im + k_im * b_re
    c_re = c_re_ref[...]
    c_im_neg = -c_im_ref[...]

    rows = lax.broadcasted_iota(jnp.int32, (SLAB_U, SLAB_S), 0) // S5_H
    cols = lax.broadcasted_iota(jnp.int32, (SLAB_U, SLAB_S), 1) // S5_STATE
    diag = rows == cols

    def block_diag(m, j):
        sl = m[j * SLAB_U:(j + 1) * SLAB_U, :]
        tiled = jnp.concatenate([sl] * SLAB_GROUPS, axis=1)
        return jnp.where(diag, tiled, 0.0)

    for j in range(N_SLAB):
        bm_ref[j, :, :SLAB_S] = block_diag(bb_re, j).astype(BF16)
        bm_ref[j, :, SLAB_S:] = block_diag(bb_im, j).astype(BF16)
        cmt_ref[j, :, :SLAB_S] = block_diag(c_re, j).astype(BF16)
        cmt_ref[j, :, SLAB_S:] = block_diag(c_im_neg, j).astype(BF16)

    half = W_A // 2
    g_rows = lax.broadcasted_iota(jnp.int32, (half, half), 0) // S5_H
    g_cols = lax.broadcasted_iota(jnp.int32, (half, half), 1) // S5_H
    for t in range(2):
        blk = wglu_ref[t * half:(t + 1) * half, :]
        tiled = jnp.concatenate([blk] * (half // S5_H), axis=1)
        glu_ref[t] = jnp.where(g_rows == g_cols, tiled, 0.0).astype(BF16)

    zeros = jnp.zeros((POOL_CH, POOL_CH), BF16)
    for t in range(2):
        poolbd_ref[t, :POOL_CH, :POOL_CH] = poolw_ref[2 * t].astype(BF16)
        poolbd_ref[t, :POOL_CH, POOL_CH:] = zeros
        poolbd_ref[t, POOL_CH:, :POOL_CH] = zeros
        poolbd_ref[t, POOL_CH:, POOL_CH:] = poolw_ref[2 * t + 1].astype(BF16)


def _s5_prep(lam_re, lam_im, log_dt, b_re, b_im, c_re, c_im, w_glu, pool_w):
    gh_p = lambda b: jnp.transpose(b, (0, 2, 1)).reshape(S5_GROUPS * S5_H, S5_STATE)
    ins = (lam_re, lam_im, log_dt.reshape(1, S5_GROUPS), gh_p(b_re), gh_p(b_im),
           c_re.reshape(S5_GROUPS * S5_H, S5_STATE),
           c_im.reshape(S5_GROUPS * S5_H, S5_STATE),
           w_glu.reshape(S5_GROUPS * S5_H, S5_H), pool_w)
    out_shape = (
        jax.ShapeDtypeStruct((1, N_STATE), F32),
        jax.ShapeDtypeStruct((1, N_STATE), F32),
        jax.ShapeDtypeStruct((N_SLAB, SLAB_U, 2 * SLAB_S), BF16),
        jax.ShapeDtypeStruct((N_SLAB, SLAB_U, 2 * SLAB_S), BF16),
        jax.ShapeDtypeStruct((2, W_A // 2, W_A // 2), BF16),
        jax.ShapeDtypeStruct((2, 2 * POOL_CH, 2 * POOL_CH), BF16),
    )
    return pl.pallas_call(
        _s5_prep_kernel,
        out_shape=out_shape,
        name="s5_prep",
    )(*ins)


def _mixer_math(x, i, g_pre_ref, g_post_ref, win_ref, a_re_ref, a_im_ref, bm_ref,
                cmt_ref, d_ref, glu_ref, poolw_ref, pscale_ref, wout_ref,
                slab_ref, hsbf_ref, hr_s, hi_s, hist_s, *, batch, steps, start_pos):
    rows = batch * steps
    xn = _rms_norm(x, g_pre_ref[...]).astype(BF16)
    w_in = win_ref[...].reshape(D_MODEL, D_MODEL)
    ua = jnp.dot(xn, w_in[:, :W_A], preferred_element_type=F32)
    ua_bf = ua.astype(BF16)

    pair = max(1, 16 // batch)

    def project_in(j):
        slab_ref[j % 2] = jnp.dot(ua_bf[:, j * SLAB_U:(j + 1) * SLAB_U], bm_ref[j],
                                  preferred_element_type=F32)

    def scan(j):
        st = slice(j * SLAB_S, (j + 1) * SLAB_S)
        sb = slab_ref.at[j % 2]
        hb = hsbf_ref.at[j % 2]
        ar = jnp.broadcast_to(a_re_ref[:, st], (batch, SLAB_S))
        ai = jnp.broadcast_to(a_im_ref[:, st], (batch, SLAB_S))
        hr = hr_s[:, st]
        hi = hi_s[:, st]
        for t0 in range(0, steps, pair):
            res, ims = [], []
            for t in range(t0, min(t0 + pair, steps)):
                rt = slice(t * batch, (t + 1) * batch)
                nr = ar * hr - ai * hi + sb[rt, :SLAB_S]
                ni = ar * hi + ai * hr + sb[rt, SLAB_S:]
                res.append(nr)
                ims.append(ni)
                hr, hi = nr, ni
            rg = slice(t0 * batch, (t0 + len(res)) * batch)
            hb[rg, :SLAB_S] = jnp.concatenate(res, axis=0).astype(BF16)
            hb[rg, SLAB_S:] = jnp.concatenate(ims, axis=0).astype(BF16)
        hr_s[:, st] = hr
        hi_s[:, st] = hi

    def project_out(j):
        return lax.dot_general(hsbf_ref[j % 2], cmt_ref[j], (((1,), (1,)), ((), ())),
                               preferred_element_type=F32)

    ys = []
    project_in(0)
    scan(0)
    ub = jnp.dot(xn, w_in[:, W_A:], preferred_element_type=F32)
    for j in range(1, N_SLAB):
        project_in(j)
        ys.append(project_out(j - 1))
        scan(j)
    ys.append(project_out(N_SLAB - 1))

    y = jnp.concatenate(ys, axis=1) + d_ref[...] * ua
    y = y * (0.5 * (1.0 + jnp.tanh(GELU_C * (y + 0.044715 * (y * y * y)))))
    y_bf = y.astype(BF16)
    half = W_A // 2
    gate = jnp.concatenate(
        [jnp.dot(y_bf[:, :half], glu_ref[0], preferred_element_type=F32),
         jnp.dot(y_bf[:, half:], glu_ref[1], preferred_element_type=F32)], axis=1)
    ya = y * (1.0 / (1.0 + jnp.exp(-gate)))

    hist = hist_s[...].reshape(POOL_HIST * batch, W_B)
    ext = jnp.concatenate([hist, ub], axis=0)
    n_ext = POOL_HIST * batch + rows
    hist_s[...] = ext[n_ext - POOL_HIST * batch:, :].reshape(POOL_HIST, batch, W_B)
    t_loc = lax.broadcasted_iota(jnp.int32, (rows, 1), 0) // batch
    pos1 = t_loc + (start_pos + 1) + i * steps
    pooled = []
    for gi, w in enumerate(POOL_WINDOWS):
        s = ext[:, gi * POOL_CH:(gi + 1) * POOL_CH]
        span = 1
        while span < w:
            n = s.shape[0]
            s = s[span * batch:, :] + s[:n - span * batch, :]
            span *= 2
        win = s[s.shape[0] - rows:, :]
        count = jnp.minimum(pos1, w).astype(F32)
        pooled.append(win / count - ub[:, gi * POOL_CH:(gi + 1) * POOL_CH])
    pooled = jnp.concatenate(pooled, axis=1).astype(BF16)
    halfb = W_B // 2
    yb = jnp.concatenate(
        [jnp.dot(pooled[:, :halfb], poolw_ref[0], preferred_element_type=F32),
         jnp.dot(pooled[:, halfb:], poolw_ref[1], preferred_element_type=F32)], axis=1)
    yb = yb * pscale_ref[...]

    ycat = jnp.concatenate([ya, yb], axis=1).astype(BF16)
    mix = jnp.dot(ycat, wout_ref[...].reshape(D_MODEL, D_MODEL),
                  preferred_element_type=F32)
    return x + _rms_norm(mix, g_post_ref[...])


def _mlp_math(h, g_pre_ref, g_post_ref, wup_ref, wdown_ref):
    hn = _rms_norm(h, g_pre_ref[...]).astype(BF16)
    acc = None
    for j in range(D_FF // FF_CHUNK):
        sl = slice(j * FF_CHUNK, (j + 1) * FF_CHUNK)
        up = jnp.dot(hn, wup_ref[:, sl], preferred_element_type=F32)
        up = jnp.maximum(up, 0.0)
        ff = (up * up).astype(BF16)
        part = jnp.dot(ff, wdown_ref[sl, :], preferred_element_type=F32)
        acc = part if acc is None else acc + part
    return h + _rms_norm(acc, g_post_ref[...])


def _mixer_scratch(batch, steps):
    return [
        pltpu.VMEM((2, batch * steps, 2 * SLAB_S), F32),
        pltpu.VMEM((2, batch * steps, 2 * SLAB_S), BF16),
        pltpu.VMEM((batch, N_STATE), F32),
        pltpu.VMEM((batch, N_STATE), F32),
        pltpu.VMEM((POOL_HIST, batch, W_B), F32),
    ]


_VMEM_WHOLE = pl.BlockSpec(memory_space=pltpu.VMEM)
_HBM = pl.BlockSpec(memory_space=pl.ANY)
_COMPILER_PARAMS = pltpu.CompilerParams(dimension_semantics=("arbitrary",),
                                        vmem_limit_bytes=VMEM_LIMIT_BYTES)


def _cast_weight(w_hbm, dst_ref, stage_ref, sem, row_chunk):
    n = w_hbm.shape[0] // row_chunk

    def copy(c):
        return pltpu.make_async_copy(w_hbm.at[pl.ds(c * row_chunk, row_chunk)],
                                     stage_ref.at[c % 2], sem.at[c % 2])

    copy(0).start()
    for c in range(n):
        if c + 1 < n:
            copy(c + 1).start()
        copy(c).wait()
        dst_ref[pl.ds(c * row_chunk, row_chunk)] = stage_ref[c % 2].astype(BF16)


N_SEQ_PARAM = 10


def _mixer_kernel(x_hbm, xs_hbm, h0r_ref, h0i_ref, pool0_hbm, win_hbm, wout_hbm,
                  *refs, batch, steps, batch_s, start_pos_s):
    g_pre, g_post, a_re, a_im, bm, cmt, d, glu, poolbd, pscale = refs[:N_SEQ_PARAM]
    (h1_hbm, hr_out, hi_out, pool_out, h1s_out, hrs_out, his_out,
     pools_hbm) = refs[N_SEQ_PARAM:N_SEQ_PARAM + 8]
    (slab, hsbf, hr_s, hi_s, hist_s, hrs_s, his_s, hists_s, xs, win_bf, wout_bf, xbuf, hbuf,
     in_sem, out_sem, s_sem, w_sem) = refs[N_SEQ_PARAM + 8:]
    params = (g_pre, g_post, win_bf, a_re, a_im, bm, cmt, d, glu, poolbd, pscale, wout_bf)
    i = pl.program_id(0)
    n_chunks = pl.num_programs(0)
    slot = lax.rem(i, 2)

    def x_copy(b, chunk, sl):
        return pltpu.make_async_copy(
            x_hbm.at[b, pl.ds(chunk * steps, steps), :], xbuf.at[sl, :, b, :],
            in_sem.at[sl])

    def h_copy(b, chunk):
        return pltpu.make_async_copy(
            hbuf.at[0, :, b, :], h1_hbm.at[b, pl.ds(chunk * steps, steps), :],
            out_sem.at[0])

    xs_copy = pltpu.make_async_copy(xs_hbm.at[:, 0, :], xs, s_sem.at[0])
    pool_in = pltpu.make_async_copy(pool0_hbm, hists_s.at[pl.ds(1, POOL_HIST - 1)],
                                    s_sem.at[1])
    pool_o = pltpu.make_async_copy(hists_s.at[pl.ds(1, POOL_HIST - 1)], pools_hbm,
                                   s_sem.at[1])

    @pl.when(i == 0)
    def _():
        for b in range(batch):
            x_copy(b, 0, 0).start()
        xs_copy.start()
        pool_in.start()
        hr_s[...] = jnp.zeros_like(hr_s)
        hi_s[...] = jnp.zeros_like(hi_s)
        hist_s[...] = jnp.zeros_like(hist_s)
        _cast_weight(win_hbm, win_bf, hbuf, w_sem, steps)
        _cast_weight(wout_hbm, wout_bf, hbuf, w_sem, steps)

    @pl.when(i + 1 < n_chunks)
    def _():
        for b in range(batch):
            x_copy(b, i + 1, 1 - slot).start()

    for b in range(batch):
        x_copy(b, i, slot).wait()
    x = xbuf[slot].reshape(steps * batch, D_MODEL)
    h1 = _mixer_math(x, i, *params, slab, hsbf, hr_s, hi_s, hist_s,
                     batch=batch, steps=steps, start_pos=0)

    @pl.when(i >= 1)
    def _():
        for b in range(batch):
            h_copy(b, i - 1).wait()

    hbuf[0] = h1.reshape(steps, batch, D_MODEL)
    for b in range(batch):
        h_copy(b, i).start()

    @pl.when(i == n_chunks - 1)
    def _():
        hr_out[...] = hr_s[...]
        hi_out[...] = hi_s[...]
        pool_out[...] = hist_s[pl.ds(1, POOL_HIST - 1)]

        xs_copy.wait()
        pool_in.wait()
        hists_s[0] = jnp.zeros((batch_s, W_B), F32)
        hrs_s[...] = h0r_ref[...].T
        his_s[...] = h0i_ref[...].T
        h1s_out[...] = _mixer_math(
            xs[...], 0, *params, slab.at[:, pl.ds(0, batch_s), :],
            hsbf.at[:, pl.ds(0, batch_s), :], hrs_s, his_s, hists_s,
            batch=batch_s, steps=1, start_pos=start_pos_s)
        pool_o.start()
        hrs_out[...] = hrs_s[...].T
        his_out[...] = his_s[...].T
        for b in range(batch):
            h_copy(b, i).wait()
        pool_o.wait()


def _mixer(x, xs, h0r_t, h0i_t, pool0_t, w_in, w_out, seq_params, *, steps, start_pos_s):
    batch, seq, _ = x.shape
    batch_s = xs.shape[0]
    assert len(seq_params) == N_SEQ_PARAM and seq % steps == 0 and batch_s <= steps * batch
    f32 = lambda *shape: jax.ShapeDtypeStruct(shape, F32)
    w3 = lambda w: w.reshape(D_MODEL // batch, batch, D_MODEL)
    return pl.pallas_call(
        functools.partial(_mixer_kernel, batch=batch, steps=steps, batch_s=batch_s,
                          start_pos_s=start_pos_s),
        out_shape=(f32(*x.shape), f32(batch, N_STATE), f32(batch, N_STATE),
                   f32(POOL_HIST - 1, batch, W_B),
                   f32(batch_s, D_MODEL), f32(N_STATE, batch_s), f32(N_STATE, batch_s),
                   f32(*pool0_t.shape)),
        grid=(seq // steps,),
        in_specs=[_HBM, _HBM, _VMEM_WHOLE, _VMEM_WHOLE, _HBM, _HBM, _HBM]
        + [_VMEM_WHOLE] * N_SEQ_PARAM,
        out_specs=(_HBM, _full_spec((batch, N_STATE)), _full_spec((batch, N_STATE)),
                   _full_spec((POOL_HIST - 1, batch, W_B)),
                   _full_spec((batch_s, D_MODEL)), _full_spec((N_STATE, batch_s)),
                   _full_spec((N_STATE, batch_s)), _HBM),
        scratch_shapes=_mixer_scratch(batch, steps) + [
            pltpu.VMEM((batch_s, N_STATE), F32),
            pltpu.VMEM((batch_s, N_STATE), F32),
            pltpu.VMEM((POOL_HIST, batch_s, W_B), F32),
            pltpu.VMEM((batch_s, D_MODEL), F32),
            pltpu.VMEM((D_MODEL // batch, batch, D_MODEL), BF16),
            pltpu.VMEM((D_MODEL // batch, batch, D_MODEL), BF16),
            pltpu.VMEM((2, steps, batch, D_MODEL), F32),
            pltpu.VMEM((1, steps, batch, D_MODEL), F32),
            pltpu.SemaphoreType.DMA((2,)),
            pltpu.SemaphoreType.DMA((1,)),
            pltpu.SemaphoreType.DMA((2,)),
            pltpu.SemaphoreType.DMA((2,)),
        ],
        compiler_params=_COMPILER_PARAMS,
        name="mixer",
    )(x, xs, h0r_t, h0i_t, pool0_t, w3(w_in), w3(w_out), *seq_params)


MLP_STAGE_ROWS = 128


def _mlp_kernel(h_ref, hs_ref, g_pre_ref, g_post_ref, wup_hbm, wdown_hbm,
                o_ref, ys_hbm, wup_bf, wdown_bf, stage_up, stage_down, ys, w_sem, y_sem):
    i = pl.program_id(0)

    @pl.when(i == 0)
    def _():
        _cast_weight(wup_hbm, wup_bf, stage_up, w_sem, MLP_STAGE_ROWS)
        _cast_weight(wdown_hbm, wdown_bf, stage_down, w_sem, 4 * MLP_STAGE_ROWS)

    o_ref[...] = _mlp_math(h_ref[...], g_pre_ref, g_post_ref, wup_bf, wdown_bf)

    @pl.when(i == pl.num_programs(0) - 1)
    def _():
        ys[...] = _mlp_math(hs_ref[...], g_pre_ref, g_post_ref, wup_bf, wdown_bf)
        y_out = pltpu.make_async_copy(ys, ys_hbm.at[:, 0, :], y_sem.at[0])
        y_out.start()
        y_out.wait()


def _mlp(h, hs, g_pre, g_post, wup, wdown, *, block_rows):
    n_rows = h.shape[0]
    batch_s = hs.shape[0]
    assert n_rows % block_rows == 0
    return pl.pallas_call(
        _mlp_kernel,
        out_shape=(jax.ShapeDtypeStruct((n_rows, D_MODEL), F32),
                   jax.ShapeDtypeStruct((batch_s, 1, D_MODEL), F32)),
        grid=(n_rows // block_rows,),
        in_specs=[
            pl.BlockSpec((block_rows, D_MODEL), lambda i: (i, 0)),
            _VMEM_WHOLE, _VMEM_WHOLE, _VMEM_WHOLE, _HBM, _HBM,
        ],
        out_specs=(pl.BlockSpec((block_rows, D_MODEL), lambda i: (i, 0)), _HBM),
        scratch_shapes=[
            pltpu.VMEM((D_MODEL, D_FF), BF16),
            pltpu.VMEM((D_FF, D_MODEL), BF16),
            pltpu.VMEM((2, MLP_STAGE_ROWS, D_FF), F32),
            pltpu.VMEM((2, 4 * MLP_STAGE_ROWS, D_MODEL), F32),
            pltpu.VMEM((batch_s, D_MODEL), F32),
            pltpu.SemaphoreType.DMA((2,)),
            pltpu.SemaphoreType.DMA((1,)),
        ],
        compiler_params=_COMPILER_PARAMS,
        name="mlp",
    )(h, hs, g_pre, g_post, wup, wdown)


def kernel(x_prompt, x_sample, state_s5_re, state_s5_im, state_pool, norm_mix_pre, norm_mix_post, norm_mlp_pre, norm_mlp_post, w_in, s5_lambda_re, s5_lambda_im, s5_log_dt, s5_b_re, s5_b_im, s5_c_re, s5_c_im, s5_d, s5_w_glu, pool_w, pool_scale, w_out, w_mlp_up, w_mlp_down):
    bp, seq, _ = x_prompt.shape
    bs = x_sample.shape[0]

    a_re, a_im, bm, cmt, glu, poolbd = _s5_prep(
        s5_lambda_re, s5_lambda_im, s5_log_dt, s5_b_re, s5_b_im, s5_c_re, s5_c_im,
        s5_w_glu, pool_w)
    row = lambda v: v.reshape(1, -1)
    seq_params = [row(norm_mix_pre), row(norm_mix_post), a_re, a_im, bm, cmt, row(s5_d),
                  glu, poolbd, row(pool_scale)]

    st_in = lambda a: jnp.transpose(a, (1, 2, 0)).reshape(N_STATE, bs)
    st_out = lambda a: jnp.transpose(a.reshape(S5_GROUPS, S5_STATE, bs), (2, 0, 1))
    st = lambda a: a.reshape(bp, S5_GROUPS, S5_STATE)
    tbc = lambda a: jnp.transpose(a, (1, 0, 2))

    h1p, hpr, hpi, pool_p, h1s, hsr, hsi, pool_s = _mixer(
        x_prompt, x_sample, st_in(state_s5_re), st_in(state_s5_im), tbc(state_pool),
        w_in, w_out, seq_params, steps=128, start_pos_s=PAST_LEN)
    yp, y_sample = _mlp(h1p.reshape(bp * seq, D_MODEL), h1s, row(norm_mlp_pre),
                        row(norm_mlp_post), w_mlp_up, w_mlp_down, block_rows=1024)

    return (yp.reshape(bp, seq, D_MODEL), y_sample, st(hpr), st(hpi), tbc(pool_p),
            st_out(hsr), st_out(hsi), tbc(pool_s))
```

```python
import functools
import math

import jax
import jax.numpy as jnp
from jax import lax
from jax.experimental import pallas as pl
from jax.experimental.pallas import tpu as pltpu

F32 = jnp.float32
BF16 = jnp.bfloat16

D_MODEL = 1024
W_A = 512
W_B = 512
S5_H = 16
S5_GROUPS = 32
S5_STATE = 64
N_STATE = S5_GROUPS * S5_STATE
N_SLAB = 4
SLAB_GROUPS = S5_GROUPS // N_SLAB
SLAB_U = SLAB_GROUPS * S5_H
SLAB_S = SLAB_GROUPS * S5_STATE
POOL_WINDOWS = (2, 4, 8, 16)
POOL_CH = 128
POOL_HIST = 16
D_FF = 4096
FF_CHUNK = 1024
SPLIT_ROWS = 512
EPS = 1e-6
PAST_LEN = 16384
GELU_C = math.sqrt(2.0 / math.pi)

VMEM_LIMIT_BYTES = 58 * 1024 * 1024


def _rms_norm(x, g):
    ms = jnp.mean(x * x, axis=-1, keepdims=True)
    return x * lax.rsqrt(ms + EPS) * g


def _full_spec(shape):
    return pl.BlockSpec(shape, lambda *_: (0,) * len(shape))


def _s5_prep_kernel(lam_re_ref, lam_im_ref, log_dt_ref, b_re_ref, b_im_ref,
                    c_re_ref, c_im_ref, wglu_ref, poolw_ref,
                    a_re_ref, a_im_ref, bm_ref, cmt_ref, glu_ref, poolbd_ref):
    lam_re = lam_re_ref[...]
    lam_im = lam_im_ref[...]
    eye = (lax.broadcasted_iota(jnp.int32, (S5_GROUPS, S5_GROUPS), 0)
           == lax.broadcasted_iota(jnp.int32, (S5_GROUPS, S5_GROUPS), 1))
    log_dt = jnp.sum(jnp.where(eye, log_dt_ref[...], 0.0), axis=1, keepdims=True)
    dt = jnp.exp(log_dt)
    mag = jnp.exp(lam_re * dt)
    ang = lam_im * dt
    a_re = mag * jnp.cos(ang)
    a_im = mag * jnp.sin(ang)
    lanes = lambda m: jnp.concatenate([m[g:g + 1, :] for g in range(S5_GROUPS)], axis=1)
    a_re_ref[...] = lanes(a_re)
    a_im_ref[...] = lanes(a_im)
    n_re = a_re - 1.0
    n_im = a_im
    den = lam_re * lam_re + lam_im * lam_im
    k_re = (n_re * lam_re + n_im * lam_im) / den
    k_im = (n_im * lam_re - n_re * lam_im) / den
    per_h = lambda m: jnp.broadcast_to(m[:, None, :], (S5_GROUPS, S5_H, S5_STATE)).reshape(
        S5_GROUPS * S5_H, S5_STATE)
    k_re = per_h(k_re)
    k_im = per_h(k_im)
    b_re = b_re_ref[...]
    b_im = b_im_ref[...]
    bb_re = k_re * b_re - k_im * b_im
    bb_im = k_re * b_im + k_im * b_re
    c_re = c_re_ref[...]
    c_im_neg = -c_im_ref[...]

    rows = lax.broadcasted_iota(jnp.int32, (SLAB_U, SLAB_S), 0) // S5_H
    cols = lax.broadcasted_iota(jnp.int32, (SLAB_U, SLAB_S), 1) // S5_STATE
    diag = rows == cols

    def block_diag(m, j):
        sl = m[j * SLAB_U:(j + 1) * SLAB_U, :]
        tiled = jnp.concatenate([sl] * SLAB_GROUPS, axis=1)
        return jnp.where(diag, tiled, 0.0)

    for j in range(N_SLAB):
        bm_ref[j, :, :SLAB_S] = block_diag(bb_re, j).astype(BF16)
        bm_ref[j, :, SLAB_S:] = block_diag(bb_im, j).astype(BF16)
        cmt_ref[j, :, :SLAB_S] = block_diag(c_re, j).astype(BF16)
        cmt_ref[j, :, SLAB_S:] = block_diag(c_im_neg, j).astype(BF16)

    half = W_A // 2
    g_rows = lax.broadcasted_iota(jnp.int32, (half, half), 0) // S5_H
    g_cols = lax.broadcasted_iota(jnp.int32, (half, half), 1) // S5_H
    for t in range(2):
        blk = wglu_ref[t * half:(t + 1) * half, :]
        tiled = jnp.concatenate([blk] * (half // S5_H), axis=1)
        glu_ref[t] = jnp.where(g_rows == g_cols, tiled, 0.0).astype(BF16)

    zeros = jnp.zeros((POOL_CH, POOL_CH), BF16)
    for t in range(2):
        poolbd_ref[t, :POOL_CH, :POOL_CH] = poolw_ref[2 * t].astype(BF16)
        poolbd_ref[t, :POOL_CH, POOL_CH:] = zeros
        poolbd_ref[t, POOL_CH:, :POOL_CH] = zeros
        poolbd_ref[t, POOL_CH:, POOL_CH:] = poolw_ref[2 * t + 1].astype(BF16)


def _s5_prep(lam_re, lam_im, log_dt, b_re, b_im, c_re, c_im, w_glu, pool_w):
    gh_p = lambda b: jnp.transpose(b, (0, 2, 1)).reshape(S5_GROUPS * S5_H, S5_STATE)
    ins = (lam_re, lam_im, log_dt.reshape(1, S5_GROUPS), gh_p(b_re), gh_p(b_im),
           c_re.reshape(S5_GROUPS * S5_H, S5_STATE),
           c_im.reshape(S5_GROUPS * S5_H, S5_STATE),
           w_glu.reshape(S5_GROUPS * S5_H, S5_H), pool_w)
    out_shape = (
        jax.ShapeDtypeStruct((1, N_STATE), F32),
        jax.ShapeDtypeStruct((1, N_STATE), F32),
        jax.ShapeDtypeStruct((N_SLAB, SLAB_U, 2 * SLAB_S), BF16),
        jax.ShapeDtypeStruct((N_SLAB, SLAB_U, 2 * SLAB_S), BF16),
        jax.ShapeDtypeStruct((2, W_A // 2, W_A // 2), BF16),
        jax.ShapeDtypeStruct((2, 2 * POOL_CH, 2 * POOL_CH), BF16),
    )
    return pl.pallas_call(
        _s5_prep_kernel,
        out_shape=out_shape,
        name="s5_prep",
    )(*ins)


def _mixer_math(x, i, g_pre_ref, g_post_ref, win_ref, a_re_ref, a_im_ref, bm_ref,
                cmt_ref, d_ref, glu_ref, poolw_ref, pscale_ref, wout_ref,
                slab_ref, hsbf_ref, hr_s, hi_s, hist_s, *, batch, steps, start_pos):
    rows = batch * steps
    parts = ([slice(0, rows // 2), slice(rows // 2, rows)] if rows >= SPLIT_ROWS
             else [slice(0, rows)])
    w_in = win_ref[...].reshape(D_MODEL, D_MODEL)
    xn_parts = [_rms_norm(x[p], g_pre_ref[...]).astype(BF16) for p in parts]
    ua = jnp.concatenate([jnp.dot(xp, w_in[:, :W_A], preferred_element_type=F32)
                          for xp in xn_parts], axis=0)
    xn = jnp.concatenate(xn_parts, axis=0)
    ua_bf = ua.astype(BF16)

    pair = max(1, 16 // batch)

    def project_in(j):
        slab_ref[j % 2] = jnp.dot(ua_bf[:, j * SLAB_U:(j + 1) * SLAB_U], bm_ref[j],
                                  preferred_element_type=F32)

    def scan(j):
        st = slice(j * SLAB_S, (j + 1) * SLAB_S)
        sb = slab_ref.at[j % 2]
        hb = hsbf_ref.at[j % 2]
        ar = jnp.broadcast_to(a_re_ref[:, st], (batch, SLAB_S))
        ai = jnp.broadcast_to(a_im_ref[:, st], (batch, SLAB_S))
        hr = hr_s[:, st]
        hi = hi_s[:, st]
        for t0 in range(0, steps, pair):
            res, ims = [], []
            for t in range(t0, min(t0 + pair, steps)):
                rt = slice(t * batch, (t + 1) * batch)
                nr = ar * hr - ai * hi + sb[rt, :SLAB_S]
                ni = ar * hi + ai * hr + sb[rt, SLAB_S:]
                res.append(nr)
                ims.append(ni)
                hr, hi = nr, ni
            rg = slice(t0 * batch, (t0 + len(res)) * batch)
            hb[rg, :SLAB_S] = jnp.concatenate(res, axis=0).astype(BF16)
            hb[rg, SLAB_S:] = jnp.concatenate(ims, axis=0).astype(BF16)
        hr_s[:, st] = hr
        hi_s[:, st] = hi

    def project_out(j):
        return lax.dot_general(hsbf_ref[j % 2], cmt_ref[j], (((1,), (1,)), ((), ())),
                               preferred_element_type=F32)

    ys = []
    project_in(0)
    scan(0)
    ub = jnp.dot(xn, w_in[:, W_A:], preferred_element_type=F32)
    for j in range(1, N_SLAB):
        project_in(j)
        ys.append(project_out(j - 1))
        scan(j)
    ys.append(project_out(N_SLAB - 1))

    y = jnp.concatenate(ys, axis=1) + d_ref[...] * ua
    y = y * (0.5 * (1.0 + jnp.tanh(GELU_C * (y + 0.044715 * (y * y * y)))))
    y_bf = y.astype(BF16)
    half = W_A // 2
    gate = jnp.concatenate(
        [jnp.dot(y_bf[:, :half], glu_ref[0], preferred_element_type=F32),
         jnp.dot(y_bf[:, half:], glu_ref[1], preferred_element_type=F32)], axis=1)
    ya = y * (1.0 / (1.0 + jnp.exp(-gate)))

    hist = hist_s[...].reshape(POOL_HIST * batch, W_B)
    ext = jnp.concatenate([hist, ub], axis=0)
    n_ext = POOL_HIST * batch + rows
    hist_s[...] = ext[n_ext - POOL_HIST * batch:, :].reshape(POOL_HIST, batch, W_B)
    t_loc = lax.broadcasted_iota(jnp.int32, (rows, 1), 0) // batch
    pos1 = t_loc + (start_pos + 1) + i * steps
    pooled = []
    for gi, w in enumerate(POOL_WINDOWS):
        s = ext[:, gi * POOL_CH:(gi + 1) * POOL_CH]
        span = 1
        while span < w:
            n = s.shape[0]
            s = s[span * batch:, :] + s[:n - span * batch, :]
            span *= 2
        win = s[s.shape[0] - rows:, :]
        count = jnp.minimum(pos1, w).astype(F32)
        pooled.append(win / count - ub[:, gi * POOL_CH:(gi + 1) * POOL_CH])
    pooled = jnp.concatenate(pooled, axis=1).astype(BF16)
    halfb = W_B // 2
    yb = jnp.concatenate(
        [jnp.dot(pooled[:, :halfb], poolw_ref[0], preferred_element_type=F32),
         jnp.dot(pooled[:, halfb:], poolw_ref[1], preferred_element_type=F32)], axis=1)
    yb = yb * pscale_ref[...]

    ycat = jnp.concatenate([ya, yb], axis=1).astype(BF16)
    w_out = wout_ref[...].reshape(D_MODEL, D_MODEL)
    return jnp.concatenate(
        [x[p] + _rms_norm(jnp.dot(ycat[p], w_out, preferred_element_type=F32),
                          g_post_ref[...]) for p in parts], axis=0)


def _mlp_math(h, g_pre_ref, g_post_ref, wup_ref, wdown_ref):
    hn = _rms_norm(h, g_pre_ref[...]).astype(BF16)
    acc = None
    for j in range(D_FF // FF_CHUNK):
        sl = slice(j * FF_CHUNK, (j + 1) * FF_CHUNK)
        up = jnp.dot(hn, wup_ref[:, sl], preferred_element_type=F32)
        up = jnp.maximum(up, 0.0)
        ff = (up * up).astype(BF16)
        part = jnp.dot(ff, wdown_ref[sl, :], preferred_element_type=F32)
        acc = part if acc is None else acc + part
    return h + _rms_norm(acc, g_post_ref[...])


def _mixer_scratch(batch, steps):
    return [
        pltpu.VMEM((2, batch * steps, 2 * SLAB_S), F32),
        pltpu.VMEM((2, batch * steps, 2 * SLAB_S), BF16),
        pltpu.VMEM((batch, N_STATE), F32),
        pltpu.VMEM((batch, N_STATE), F32),
        pltpu.VMEM((POOL_HIST, batch, W_B), F32),
    ]


_VMEM_WHOLE = pl.BlockSpec(memory_space=pltpu.VMEM)
_HBM = pl.BlockSpec(memory_space=pl.ANY)
_COMPILER_PARAMS = pltpu.CompilerParams(dimension_semantics=("arbitrary",),
                                        vmem_limit_bytes=VMEM_LIMIT_BYTES)


def _cast_weight(w_hbm, dst_ref, stage_ref, sem, row_chunk):
    n = w_hbm.shape[0] // row_chunk

    def copy(c):
        return pltpu.make_async_copy(w_hbm.at[pl.ds(c * row_chunk, row_chunk)],
                                     stage_ref.at[c % 2], sem.at[c % 2])

    copy(0).start()
    for c in range(n):
        if c + 1 < n:
            copy(c + 1).start()
        copy(c).wait()
        dst_ref[pl.ds(c * row_chunk, row_chunk)] = stage_ref[c % 2].astype(BF16)


N_SEQ_PARAM = 10


def _mixer_kernel(x_hbm, xs_hbm, h0r_ref, h0i_ref, pool0_hbm, win_hbm, wout_hbm,
                  *refs, batch, steps, batch_s, start_pos_s):
    g_pre, g_post, a_re, a_im, bm, cmt, d, glu, poolbd, pscale = refs[:N_SEQ_PARAM]
    (h1_hbm, hr_out, hi_out, pool_out, h1s_out, hrs_out, his_out,
     pools_hbm) = refs[N_SEQ_PARAM:N_SEQ_PARAM + 8]
    (slab, hsbf, hr_s, hi_s, hist_s, hrs_s, his_s, hists_s, xs, win_bf, wout_bf, xbuf, hbuf,
     in_sem, out_sem, s_sem, w_sem) = refs[N_SEQ_PARAM + 8:]
    params = (g_pre, g_post, win_bf, a_re, a_im, bm, cmt, d, glu, poolbd, pscale, wout_bf)
    i = pl.program_id(0)
    n_chunks = pl.num_programs(0)
    slot = lax.rem(i, 2)

    def x_copy(b, chunk, sl):
        return pltpu.make_async_copy(
            x_hbm.at[b, pl.ds(chunk * steps, steps), :], xbuf.at[sl, :, b, :],
            in_sem.at[sl])

    def h_copy(b, chunk):
        return pltpu.make_async_copy(
            hbuf.at[0, :, b, :], h1_hbm.at[b, pl.ds(chunk * steps, steps), :],
            out_sem.at[0])

    xs_copy = pltpu.make_async_copy(xs_hbm.at[:, 0, :], xs, s_sem.at[0])
    pool_in = pltpu.make_async_copy(pool0_hbm, hists_s.at[pl.ds(1, POOL_HIST - 1)],
                                    s_sem.at[1])
    pool_o = pltpu.make_async_copy(hists_s.at[pl.ds(1, POOL_HIST - 1)], pools_hbm,
                                   s_sem.at[1])

    @pl.when(i == 0)
    def _():
        for b in range(batch):
            x_copy(b, 0, 0).start()
        xs_copy.start()
        pool_in.start()
        hr_s[...] = jnp.zeros_like(hr_s)
        hi_s[...] = jnp.zeros_like(hi_s)
        hist_s[...] = jnp.zeros_like(hist_s)
        _cast_weight(win_hbm, win_bf, hbuf, w_sem, steps)
        _cast_weight(wout_hbm, wout_bf, hbuf, w_sem, steps)

    @pl.when(i + 1 < n_chunks)
    def _():
        for b in range(batch):
            x_copy(b, i + 1, 1 - slot).start()

    for b in range(batch):
        x_copy(b, i, slot).wait()
    x = xbuf[slot].reshape(steps * batch, D_MODEL)
    h1 = _mixer_math(x, i, *params, slab, hsbf, hr_s, hi_s, hist_s,
                     batch=batch, steps=steps, start_pos=0)

    @pl.when(i >= 1)
    def _():
        for b in range(batch):
            h_copy(b, i - 1).wait()

    hbuf[0] = h1.reshape(steps, batch, D_MODEL)
    for b in range(batch):
        h_copy(b, i).start()

    @pl.when(i == n_chunks - 1)
    def _():
        for g in range(S5_GROUPS):
            hr_out[:, g, :] = hr_s[:, g * S5_STATE:(g + 1) * S5_STATE]
            hi_out[:, g, :] = hi_s[:, g * S5_STATE:(g + 1) * S5_STATE]
        pool_out[...] = hist_s[pl.ds(1, POOL_HIST - 1)]

        xs_copy.wait()
        pool_in.wait()
        hists_s[0] = jnp.zeros((batch_s, W_B), F32)
        hrs_s[...] = h0r_ref[...].T
        his_s[...] = h0i_ref[...].T
        h1s_out[...] = _mixer_math(
            xs[...], 0, *params, slab.at[:, pl.ds(0, batch_s), :],
            hsbf.at[:, pl.ds(0, batch_s), :], hrs_s, his_s, hists_s,
            batch=batch_s, steps=1, start_pos=start_pos_s)
        pool_o.start()
        hrs_out[...] = hrs_s[...].T
        his_out[...] = his_s[...].T
        for b in range(batch):
            h_copy(b, i).wait()
        pool_o.wait()


def _mixer(x, xs, h0r_t, h0i_t, pool0_t, w_in, w_out, seq_params, *, steps, start_pos_s):
    batch, seq, _ = x.shape
    batch_s = xs.shape[0]
    assert len(seq_params) == N_SEQ_PARAM and seq % steps == 0 and batch_s <= steps * batch
    f32 = lambda *shape: jax.ShapeDtypeStruct(shape, F32)
    w3 = lambda w: w.reshape(D_MODEL // batch, batch, D_MODEL)
    return pl.pallas_call(
        functools.partial(_mixer_kernel, batch=batch, steps=steps, batch_s=batch_s,
                          start_pos_s=start_pos_s),
        out_shape=(f32(*x.shape), f32(batch, S5_GROUPS, S5_STATE),
                   f32(batch, S5_GROUPS, S5_STATE), f32(POOL_HIST - 1, batch, W_B),
                   f32(batch_s, D_MODEL), f32(N_STATE, batch_s), f32(N_STATE, batch_s),
                   f32(*pool0_t.shape)),
        grid=(seq // steps,),
        in_specs=[_HBM, _HBM, _VMEM_WHOLE, _VMEM_WHOLE, _HBM, _HBM, _HBM]
        + [_VMEM_WHOLE] * N_SEQ_PARAM,
        out_specs=(_HBM, _full_spec((batch, S5_GROUPS, S5_STATE)),
                   _full_spec((batch, S5_GROUPS, S5_STATE)),
                   _full_spec((POOL_HIST - 1, batch, W_B)),
                   _full_spec((batch_s, D_MODEL)), _full_spec((N_STATE, batch_s)),
                   _full_spec((N_STATE, batch_s)), _HBM),
        scratch_shapes=_mixer_scratch(batch, steps) + [
            pltpu.VMEM((batch_s, N_STATE), F32),
            pltpu.VMEM((batch_s, N_STATE), F32),
            pltpu.VMEM((POOL_HIST, batch_s, W_B), F32),
            pltpu.VMEM((batch_s, D_MODEL), F32),
            pltpu.VMEM((D_MODEL // batch, batch, D_MODEL), BF16),
            pltpu.VMEM((D_MODEL // batch, batch, D_MODEL), BF16),
            pltpu.VMEM((2, steps, batch, D_MODEL), F32),
            pltpu.VMEM((1, steps, batch, D_MODEL), F32),
            pltpu.SemaphoreType.DMA((2,)),
            pltpu.SemaphoreType.DMA((1,)),
            pltpu.SemaphoreType.DMA((2,)),
            pltpu.SemaphoreType.DMA((2,)),
        ],
        compiler_params=_COMPILER_PARAMS,
        name="mixer",
    )(x, xs, h0r_t, h0i_t, pool0_t, w3(w_in), w3(w_out), *seq_params)


MLP_STAGE_ROWS = 128


def _mlp_kernel(h_ref, hs_ref, g_pre_ref, g_post_ref, wup_hbm, wdown_hbm,
                o_ref, ys_hbm, wup_bf, wdown_bf, stage_up, stage_down, ys, w_sem, y_sem):
    i = pl.program_id(0)

    @pl.when(i == 0)
    def _():
        _cast_weight(wup_hbm, wup_bf, stage_up, w_sem, MLP_STAGE_ROWS)
        _cast_weight(wdown_hbm, wdown_bf, stage_down, w_sem, 4 * MLP_STAGE_ROWS)

    o_ref[...] = _mlp_math(h_ref[...], g_pre_ref, g_post_ref, wup_bf, wdown_bf)

    @pl.when(i == pl.num_programs(0) - 1)
    def _():
        ys[...] = _mlp_math(hs_ref[...], g_pre_ref, g_post_ref, wup_bf, wdown_bf)
        y_out = pltpu.make_async_copy(ys, ys_hbm.at[:, 0, :], y_sem.at[0])
        y_out.start()
        y_out.wait()


def _mlp(h, hs, g_pre, g_post, wup, wdown, *, block_rows):
    n_rows = h.shape[0]
    batch_s = hs.shape[0]
    assert n_rows % block_rows == 0
    return pl.pallas_call(
        _mlp_kernel,
        out_shape=(jax.ShapeDtypeStruct((n_rows, D_MODEL), F32),
                   jax.ShapeDtypeStruct((batch_s, 1, D_MODEL), F32)),
        grid=(n_rows // block_rows,),
        in_specs=[
            pl.BlockSpec((block_rows, D_MODEL), lambda i: (i, 0)),
            _VMEM_WHOLE, _VMEM_WHOLE, _VMEM_WHOLE, _HBM, _HBM,
        ],
        out_specs=(pl.BlockSpec((block_rows, D_MODEL), lambda i: (i, 0)), _HBM),
        scratch_shapes=[
            pltpu.VMEM((D_MODEL, D_FF), BF16),
            pltpu.VMEM((D_FF, D_MODEL), BF16),
            pltpu.VMEM((2, MLP_STAGE_ROWS, D_FF), F32),
            pltpu.VMEM((2, 4 * MLP_STAGE_ROWS, D_MODEL), F32),
            pltpu.VMEM((batch_s, D_MODEL), F32),
            pltpu.SemaphoreType.DMA((2,)),
            pltpu.SemaphoreType.DMA((1,)),
        ],
        compiler_params=_COMPILER_PARAMS,
        name="mlp",
    )(h, hs, g_pre, g_post, wup, wdown)


def kernel(x_prompt, x_sample, state_s5_re, state_s5_im, state_pool, norm_mix_pre, norm_mix_post, norm_mlp_pre, norm_mlp_post, w_in, s5_lambda_re, s5_lambda_im, s5_log_dt, s5_b_re, s5_b_im, s5_c_re, s5_c_im, s5_d, s5_w_glu, pool_w, pool_scale, w_out, w_mlp_up, w_mlp_down):
    bp, seq, _ = x_prompt.shape
    bs = x_sample.shape[0]

    a_re, a_im, bm, cmt, glu, poolbd = _s5_prep(
        s5_lambda_re, s5_lambda_im, s5_log_dt, s5_b_re, s5_b_im, s5_c_re, s5_c_im,
        s5_w_glu, pool_w)
    row = lambda v: v.reshape(1, -1)
    seq_params = [row(norm_mix_pre), row(norm_mix_post), a_re, a_im, bm, cmt, row(s5_d),
                  glu, poolbd, row(pool_scale)]

    st_in = lambda a: jnp.transpose(a, (1, 2, 0)).reshape(N_STATE, bs)
    st_out = lambda a: jnp.transpose(a.reshape(S5_GROUPS, S5_STATE, bs), (2, 0, 1))
    tbc = lambda a: jnp.transpose(a, (1, 0, 2))

    h1p, hpr, hpi, pool_p, h1s, hsr, hsi, pool_s = _mixer(
        x_prompt, x_sample, st_in(state_s5_re), st_in(state_s5_im), tbc(state_pool),
        w_in, w_out, seq_params, steps=128, start_pos_s=PAST_LEN)
    yp, y_sample = _mlp(h1p.reshape(bp * seq, D_MODEL), h1s, row(norm_mlp_pre),
                        row(norm_mlp_post), w_mlp_up, w_mlp_down, block_rows=1024)

    return (yp.reshape(bp, seq, D_MODEL), y_sample, hpr, hpi, tbc(pool_p),
            st_out(hsr), st_out(hsi), tbc(pool_s))
```

```python
import functools
import math

import jax
import jax.numpy as jnp
from jax import lax
from jax.experimental import pallas as pl
from jax.experimental.pallas import tpu as pltpu

F32 = jnp.float32
BF16 = jnp.bfloat16

D_MODEL = 1024
W_A = 512
W_B = 512
S5_H = 16
S5_GROUPS = 32
S5_STATE = 64
N_STATE = S5_GROUPS * S5_STATE
N_SLAB = 4
SLAB_GROUPS = S5_GROUPS // N_SLAB
SLAB_U = SLAB_GROUPS * S5_H
SLAB_S = SLAB_GROUPS * S5_STATE
POOL_WINDOWS = (2, 4, 8, 16)
POOL_CH = 128
POOL_HIST = 16
D_FF = 4096
FF_CHUNK = 1024
EPS = 1e-6
PAST_LEN = 16384
GELU_C = math.sqrt(2.0 / math.pi)

VMEM_LIMIT_BYTES = 58 * 1024 * 1024


def _rms_norm(x, g):
    ms = jnp.mean(x * x, axis=-1, keepdims=True)
    return x * lax.rsqrt(ms + EPS) * g


def _full_spec(shape):
    return pl.BlockSpec(shape, lambda *_: (0,) * len(shape))


def _s5_prep_kernel(lam_re_ref, lam_im_ref, log_dt_ref, b_re_ref, b_im_ref,
                    c_re_ref, c_im_ref, wglu_ref, poolw_ref,
                    a_re_ref, a_im_ref, bm_ref, cmt_ref, glu_ref, poolbd_ref):
    lam_re = lam_re_ref[...]
    lam_im = lam_im_ref[...]
    eye = (lax.broadcasted_iota(jnp.int32, (S5_GROUPS, S5_GROUPS), 0)
           == lax.broadcasted_iota(jnp.int32, (S5_GROUPS, S5_GROUPS), 1))
    log_dt = jnp.sum(jnp.where(eye, log_dt_ref[...], 0.0), axis=1, keepdims=True)
    dt = jnp.exp(log_dt)
    mag = jnp.exp(lam_re * dt)
    ang = lam_im * dt
    a_re = mag * jnp.cos(ang)
    a_im = mag * jnp.sin(ang)
    lanes = lambda m: jnp.concatenate([m[g:g + 1, :] for g in range(S5_GROUPS)], axis=1)
    a_re_ref[...] = lanes(a_re)
    a_im_ref[...] = lanes(a_im)
    n_re = a_re - 1.0
    n_im = a_im
    den = lam_re * lam_re + lam_im * lam_im
    k_re = (n_re * lam_re + n_im * lam_im) / den
    k_im = (n_im * lam_re - n_re * lam_im) / den
    per_h = lambda m: jnp.broadcast_to(m[:, None, :], (S5_GROUPS, S5_H, S5_STATE)).reshape(
        S5_GROUPS * S5_H, S5_STATE)
    k_re = per_h(k_re)
    k_im = per_h(k_im)
    b_re = b_re_ref[...]
    b_im = b_im_ref[...]
    bb_re = k_re * b_re - k_im * b_im
    bb_im = k_re * b_im + k_im * b_re
    c_re = c_re_ref[...]
    c_im_neg = -c_im_ref[...]

    rows = lax.broadcasted_iota(jnp.int32, (SLAB_U, SLAB_S), 0) // S5_H
    cols = lax.broadcasted_iota(jnp.int32, (SLAB_U, SLAB_S), 1) // S5_STATE
    diag = rows == cols

    def block_diag(m, j):
        sl = m[j * SLAB_U:(j + 1) * SLAB_U, :]
        tiled = jnp.concatenate([sl] * SLAB_GROUPS, axis=1)
        return jnp.where(diag, tiled, 0.0)

    for j in range(N_SLAB):
        bm_ref[j, :, :SLAB_S] = block_diag(bb_re, j).astype(BF16)
        bm_ref[j, :, SLAB_S:] = block_diag(bb_im, j).astype(BF16)
        cmt_ref[j, :, :SLAB_S] = block_diag(c_re, j).astype(BF16)
        cmt_ref[j, :, SLAB_S:] = block_diag(c_im_neg, j).astype(BF16)

    half = W_A // 2
    g_rows = lax.broadcasted_iota(jnp.int32, (half, half), 0) // S5_H
    g_cols = lax.broadcasted_iota(jnp.int32, (half, half), 1) // S5_H
    for t in range(2):
        blk = wglu_ref[t * half:(t + 1) * half, :]
        tiled = jnp.concatenate([blk] * (half // S5_H), axis=1)
        glu_ref[t] = jnp.where(g_rows == g_cols, tiled, 0.0).astype(BF16)

    zeros = jnp.zeros((POOL_CH, POOL_CH), BF16)
    for t in range(2):
        poolbd_ref[t, :POOL_CH, :POOL_CH] = poolw_ref[2 * t].astype(BF16)
        poolbd_ref[t, :POOL_CH, POOL_CH:] = zeros
        poolbd_ref[t, POOL_CH:, :POOL_CH] = zeros
        poolbd_ref[t, POOL_CH:, POOL_CH:] = poolw_ref[2 * t + 1].astype(BF16)


def _s5_prep(lam_re, lam_im, log_dt, b_re, b_im, c_re, c_im, w_glu, pool_w):
    gh_p = lambda b: jnp.transpose(b, (0, 2, 1)).reshape(S5_GROUPS * S5_H, S5_STATE)
    ins = (lam_re, lam_im, log_dt.reshape(1, S5_GROUPS), gh_p(b_re), gh_p(b_im),
           c_re.reshape(S5_GROUPS * S5_H, S5_STATE),
           c_im.reshape(S5_GROUPS * S5_H, S5_STATE),
           w_glu.reshape(S5_GROUPS * S5_H, S5_H), pool_w)
    out_shape = (
        jax.ShapeDtypeStruct((1, N_STATE), F32),
        jax.ShapeDtypeStruct((1, N_STATE), F32),
        jax.ShapeDtypeStruct((N_SLAB, SLAB_U, 2 * SLAB_S), BF16),
        jax.ShapeDtypeStruct((N_SLAB, SLAB_U, 2 * SLAB_S), BF16),
        jax.ShapeDtypeStruct((2, W_A // 2, W_A // 2), BF16),
        jax.ShapeDtypeStruct((2, 2 * POOL_CH, 2 * POOL_CH), BF16),
    )
    return pl.pallas_call(
        _s5_prep_kernel,
        out_shape=out_shape,
        name="s5_prep",
    )(*ins)


def _mixer_math(x, i, g_pre_ref, g_post_ref, win_ref, a_re_ref, a_im_ref, bm_ref,
                cmt_ref, d_ref, glu_ref, poolw_ref, pscale_ref, wout_ref,
                slab_ref, hsbf_ref, hr_s, hi_s, hist_s, *, batch, steps, start_pos):
    rows = batch * steps
    xn = _rms_norm(x, g_pre_ref[...]).astype(BF16)
    w_in = win_ref[...].reshape(D_MODEL, D_MODEL)
    ua = jnp.dot(xn, w_in[:, :W_A], preferred_element_type=F32)
    ua_bf = ua.astype(BF16)

    pair = max(1, 16 // batch)

    def project_in(j):
        slab_ref[j % 2] = jnp.dot(ua_bf[:, j * SLAB_U:(j + 1) * SLAB_U], bm_ref[j],
                                  preferred_element_type=F32)

    def scan(j):
        st = slice(j * SLAB_S, (j + 1) * SLAB_S)
        sb = slab_ref.at[j % 2]
        hb = hsbf_ref.at[j % 2]
        ar = jnp.broadcast_to(a_re_ref[:, st], (batch, SLAB_S))
        ai = jnp.broadcast_to(a_im_ref[:, st], (batch, SLAB_S))
        hr = hr_s[:, st]
        hi = hi_s[:, st]
        for t0 in range(0, steps, pair):
            res, ims = [], []
            for t in range(t0, min(t0 + pair, steps)):
                rt = slice(t * batch, (t + 1) * batch)
                nr = ar * hr - ai * hi + sb[rt, :SLAB_S]
                ni = ar * hi + ai * hr + sb[rt, SLAB_S:]
                res.append(nr)
                ims.append(ni)
                hr, hi = nr, ni
            rg = slice(t0 * batch, (t0 + len(res)) * batch)
            hb[rg, :SLAB_S] = jnp.concatenate(res, axis=0).astype(BF16)
            hb[rg, SLAB_S:] = jnp.concatenate(ims, axis=0).astype(BF16)
        hr_s[:, st] = hr
        hi_s[:, st] = hi

    def project_out(j):
        return lax.dot_general(hsbf_ref[j % 2], cmt_ref[j], (((1,), (1,)), ((), ())),
                               preferred_element_type=F32)

    ys = []
    project_in(0)
    scan(0)
    ub = jnp.dot(xn, w_in[:, W_A:], preferred_element_type=F32)
    for j in range(1, N_SLAB):
        project_in(j)
        ys.append(project_out(j - 1))
        scan(j)
    ys.append(project_out(N_SLAB - 1))

    y = jnp.concatenate(ys, axis=1) + d_ref[...] * ua
    y = y * (0.5 * (1.0 + jnp.tanh(GELU_C * (y + 0.044715 * (y * y * y)))))
    y_bf = y.astype(BF16)
    half = W_A // 2
    gate = jnp.concatenate(
        [jnp.dot(y_bf[:, :half], glu_ref[0], preferred_element_type=F32),
         jnp.dot(y_bf[:, half:], glu_ref[1], preferred_element_type=F32)], axis=1)
    ya = y * (1.0 / (1.0 + jnp.exp(-gate)))

    hist = hist_s[...].reshape(POOL_HIST * batch, W_B)
    ext = jnp.concatenate([hist, ub], axis=0)
    n_ext = POOL_HIST * batch + rows
    hist_s[...] = ext[n_ext - POOL_HIST * batch:, :].reshape(POOL_HIST, batch, W_B)
    t_loc = lax.broadcasted_iota(jnp.int32, (rows, 1), 0) // batch
    pos1 = t_loc + (start_pos + 1) + i * steps
    pooled = []
    for gi, w in enumerate(POOL_WINDOWS):
        s = ext[:, gi * POOL_CH:(gi + 1) * POOL_CH]
        span = 1
        while span < w:
            n = s.shape[0]
            s = s[span * batch:, :] + s[:n - span * batch, :]
            span *= 2
        win = s[s.shape[0] - rows:, :]
        count = jnp.minimum(pos1, w).astype(F32)
        pooled.append(win / count - ub[:, gi * POOL_CH:(gi + 1) * POOL_CH])
    pooled = jnp.concatenate(pooled, axis=1).astype(BF16)
    halfb = W_B // 2
    yb = jnp.concatenate(
        [jnp.dot(pooled[:, :halfb], poolw_ref[0], preferred_element_type=F32),
         jnp.dot(pooled[:, halfb:], poolw_ref[1], preferred_element_type=F32)], axis=1)
    yb = yb * pscale_ref[...]

    ycat = jnp.concatenate([ya, yb], axis=1).astype(BF16)
    mix = jnp.dot(ycat, wout_ref[...].reshape(D_MODEL, D_MODEL),
                  preferred_element_type=F32)
    return x + _rms_norm(mix, g_post_ref[...])


def _mlp_math(h, g_pre_ref, g_post_ref, wup_ref, wdown_ref):
    hn = _rms_norm(h, g_pre_ref[...]).astype(BF16)
    acc = None
    for j in range(D_FF // FF_CHUNK):
        sl = slice(j * FF_CHUNK, (j + 1) * FF_CHUNK)
        up = jnp.dot(hn, wup_ref[:, sl], preferred_element_type=F32)
        up = jnp.maximum(up, 0.0)
        ff = (up * up).astype(BF16)
        part = jnp.dot(ff, wdown_ref[sl, :], preferred_element_type=F32)
        acc = part if acc is None else acc + part
    return h + _rms_norm(acc, g_post_ref[...])


def _mixer_scratch(batch, steps):
    return [
        pltpu.VMEM((2, batch * steps, 2 * SLAB_S), F32),
        pltpu.VMEM((2, batch * steps, 2 * SLAB_S), BF16),
        pltpu.VMEM((batch, N_STATE), F32),
        pltpu.VMEM((batch, N_STATE), F32),
        pltpu.VMEM((POOL_HIST, batch, W_B), F32),
    ]


_VMEM_WHOLE = pl.BlockSpec(memory_space=pltpu.VMEM)
_HBM = pl.BlockSpec(memory_space=pl.ANY)
_COMPILER_PARAMS = pltpu.CompilerParams(dimension_semantics=("arbitrary",),
                                        vmem_limit_bytes=VMEM_LIMIT_BYTES)


def _cast_weight(w_hbm, dst_ref, stage_ref, sem, row_chunk):
    n = w_hbm.shape[0] // row_chunk

    def copy(c):
        return pltpu.make_async_copy(w_hbm.at[pl.ds(c * row_chunk, row_chunk)],
                                     stage_ref.at[c % 2], sem.at[c % 2])

    copy(0).start()
    for c in range(n):
        if c + 1 < n:
            copy(c + 1).start()
        copy(c).wait()
        dst_ref[pl.ds(c * row_chunk, row_chunk)] = stage_ref[c % 2].astype(BF16)


N_SEQ_PARAM = 10


def _mixer_kernel(x_hbm, xs_hbm, h0r_ref, h0i_ref, pool0_hbm, win_hbm, wout_hbm,
                  *refs, batch, steps, batch_s, start_pos_s):
    g_pre, g_post, a_re, a_im, bm, cmt, d, glu, poolbd, pscale = refs[:N_SEQ_PARAM]
    (h1_hbm, hr_out, hi_out, pool_out, h1s_out, hrs_out, his_out,
     pools_hbm) = refs[N_SEQ_PARAM:N_SEQ_PARAM + 8]
    (slab, hsbf, hr_s, hi_s, hist_s, hrs_s, his_s, hists_s, xs, win_bf, wout_bf, xbuf, hbuf,
     in_sem, out_sem, s_sem, w_sem) = refs[N_SEQ_PARAM + 8:]
    params = (g_pre, g_post, win_bf, a_re, a_im, bm, cmt, d, glu, poolbd, pscale, wout_bf)
    i = pl.program_id(0)
    n_chunks = pl.num_programs(0)
    slot = lax.rem(i, 2)

    def x_copy(b, chunk, sl):
        return pltpu.make_async_copy(
            x_hbm.at[b, pl.ds(chunk * steps, steps), :], xbuf.at[sl, :, b, :],
            in_sem.at[sl])

    def h_copy(b, chunk):
        return pltpu.make_async_copy(
            hbuf.at[0, :, b, :], h1_hbm.at[b, pl.ds(chunk * steps, steps), :],
            out_sem.at[0])

    xs_copy = pltpu.make_async_copy(xs_hbm.at[:, 0, :], xs, s_sem.at[0])
    pool_in = pltpu.make_async_copy(pool0_hbm, hists_s.at[pl.ds(1, POOL_HIST - 1)],
                                    s_sem.at[1])
    pool_o = pltpu.make_async_copy(hists_s.at[pl.ds(1, POOL_HIST - 1)], pools_hbm,
                                   s_sem.at[1])

    @pl.when(i == 0)
    def _():
        for b in range(batch):
            x_copy(b, 0, 0).start()
        xs_copy.start()
        pool_in.start()
        hr_s[...] = jnp.zeros_like(hr_s)
        hi_s[...] = jnp.zeros_like(hi_s)
        hist_s[...] = jnp.zeros_like(hist_s)
        _cast_weight(win_hbm, win_bf, hbuf, w_sem, steps)
        _cast_weight(wout_hbm, wout_bf, hbuf, w_sem, steps)

    @pl.when(i + 1 < n_chunks)
    def _():
        for b in range(batch):
            x_copy(b, i + 1, 1 - slot).start()

    for b in range(batch):
        x_copy(b, i, slot).wait()
    x = xbuf[slot].reshape(steps * batch, D_MODEL)
    h1 = _mixer_math(x, i, *params, slab, hsbf, hr_s, hi_s, hist_s,
                     batch=batch, steps=steps, start_pos=0)

    @pl.when(i >= 1)
    def _():
        for b in range(batch):
            h_copy(b, i - 1).wait()

    hbuf[0] = h1.reshape(steps, batch, D_MODEL)
    for b in range(batch):
        h_copy(b, i).start()

    @pl.when(i == n_chunks - 1)
    def _():
        for g in range(S5_GROUPS):
            hr_out[:, g, :] = hr_s[:, g * S5_STATE:(g + 1) * S5_STATE]
            hi_out[:, g, :] = hi_s[:, g * S5_STATE:(g + 1) * S5_STATE]
        pool_out[...] = hist_s[pl.ds(1, POOL_HIST - 1)]

        xs_copy.wait()
        pool_in.wait()
        hists_s[0] = jnp.zeros((batch_s, W_B), F32)
        hrs_s[...] = h0r_ref[...].T
        his_s[...] = h0i_ref[...].T
        h1s_out[...] = _mixer_math(
            xs[...], 0, *params, slab.at[:, pl.ds(0, batch_s), :],
            hsbf.at[:, pl.ds(0, batch_s), :], hrs_s, his_s, hists_s,
            batch=batch_s, steps=1, start_pos=start_pos_s)
        pool_o.start()
        hrs_out[...] = hrs_s[...].T
        his_out[...] = his_s[...].T
        for b in range(batch):
            h_copy(b, i).wait()
        pool_o.wait()


def _mixer(x, xs, h0r_t, h0i_t, pool0_t, w_in, w_out, seq_params, *, steps, start_pos_s):
    batch, seq, _ = x.shape
    batch_s = xs.shape[0]
    assert len(seq_params) == N_SEQ_PARAM and seq % steps == 0 and batch_s <= steps * batch
    f32 = lambda *shape: jax.ShapeDtypeStruct(shape, F32)
    w3 = lambda w: w.reshape(D_MODEL // batch, batch, D_MODEL)
    return pl.pallas_call(
        functools.partial(_mixer_kernel, batch=batch, steps=steps, batch_s=batch_s,
                          start_pos_s=start_pos_s),
        out_shape=(f32(*x.shape), f32(batch, S5_GROUPS, S5_STATE),
                   f32(batch, S5_GROUPS, S5_STATE), f32(POOL_HIST - 1, batch, W_B),
                   f32(batch_s, D_MODEL), f32(N_STATE, batch_s), f32(N_STATE, batch_s),
                   f32(*pool0_t.shape)),
        grid=(seq // steps,),
        in_specs=[_HBM, _HBM, _VMEM_WHOLE, _VMEM_WHOLE, _HBM, _HBM, _HBM]
        + [_VMEM_WHOLE] * N_SEQ_PARAM,
        out_specs=(_HBM, _full_spec((batch, S5_GROUPS, S5_STATE)),
                   _full_spec((batch, S5_GROUPS, S5_STATE)),
                   _full_spec((POOL_HIST - 1, batch, W_B)),
                   _full_spec((batch_s, D_MODEL)), _full_spec((N_STATE, batch_s)),
                   _full_spec((N_STATE, batch_s)), _HBM),
        scratch_shapes=_mixer_scratch(batch, steps) + [
            pltpu.VMEM((batch_s, N_STATE), F32),
            pltpu.VMEM((batch_s, N_STATE), F32),
            pltpu.VMEM((POOL_HIST, batch_s, W_B), F32),
            pltpu.VMEM((batch_s, D_MODEL), F32),
            pltpu.VMEM((D_MODEL // batch, batch, D_MODEL), BF16),
            pltpu.VMEM((D_MODEL // batch, batch, D_MODEL), BF16),
            pltpu.VMEM((2, steps, batch, D_MODEL), F32),
            pltpu.VMEM((1, steps, batch, D_MODEL), F32),
            pltpu.SemaphoreType.DMA((2,)),
            pltpu.SemaphoreType.DMA((1,)),
            pltpu.SemaphoreType.DMA((2,)),
            pltpu.SemaphoreType.DMA((2,)),
        ],
        compiler_params=_COMPILER_PARAMS,
        name="mixer",
    )(x, xs, h0r_t, h0i_t, pool0_t, w3(w_in), w3(w_out), *seq_params)


MLP_STAGE_ROWS = 128


def _mlp_kernel(h_ref, hs_ref, g_pre_ref, g_post_ref, wup_hbm, wdown_hbm,
                o_ref, ys_hbm, wup_bf, wdown_bf, stage_up, stage_down, ys, w_sem, y_sem):
    i = pl.program_id(0)

    @pl.when(i == 0)
    def _():
        _cast_weight(wup_hbm, wup_bf, stage_up, w_sem, MLP_STAGE_ROWS)
        _cast_weight(wdown_hbm, wdown_bf, stage_down, w_sem, 4 * MLP_STAGE_ROWS)

    o_ref[...] = _mlp_math(h_ref[...], g_pre_ref, g_post_ref, wup_bf, wdown_bf)

    @pl.when(i == pl.num_programs(0) - 1)
    def _():
        ys[...] = _mlp_math(hs_ref[...], g_pre_ref, g_post_ref, wup_bf, wdown_bf)
        y_out = pltpu.make_async_copy(ys, ys_hbm.at[:, 0, :], y_sem.at[0])
        y_out.start()
        y_out.wait()


def _mlp(h, hs, g_pre, g_post, wup, wdown, *, block_rows):
    n_rows = h.shape[0]
    batch_s = hs.shape[0]
    assert n_rows % block_rows == 0
    return pl.pallas_call(
        _mlp_kernel,
        out_shape=(jax.ShapeDtypeStruct((n_rows, D_MODEL), F32),
                   jax.ShapeDtypeStruct((batch_s, 1, D_MODEL), F32)),
        grid=(n_rows // block_rows,),
        in_specs=[
            pl.BlockSpec((block_rows, D_MODEL), lambda i: (i, 0)),
            _VMEM_WHOLE, _VMEM_WHOLE, _VMEM_WHOLE, _HBM, _HBM,
        ],
        out_specs=(pl.BlockSpec((block_rows, D_MODEL), lambda i: (i, 0)), _HBM),
        scratch_shapes=[
            pltpu.VMEM((D_MODEL, D_FF), BF16),
            pltpu.VMEM((D_FF, D_MODEL), BF16),
            pltpu.VMEM((2, MLP_STAGE_ROWS, D_FF), F32),
            pltpu.VMEM((2, 4 * MLP_STAGE_ROWS, D_MODEL), F32),
            pltpu.VMEM((batch_s, D_MODEL), F32),
            pltpu.SemaphoreType.DMA((2,)),
            pltpu.SemaphoreType.DMA((1,)),
        ],
        compiler_params=_COMPILER_PARAMS,
        name="mlp",
    )(h, hs, g_pre, g_post, wup, wdown)


def kernel(x_prompt, x_sample, state_s5_re, state_s5_im, state_pool, norm_mix_pre, norm_mix_post, norm_mlp_pre, norm_mlp_post, w_in, s5_lambda_re, s5_lambda_im, s5_log_dt, s5_b_re, s5_b_im, s5_c_re, s5_c_im, s5_d, s5_w_glu, pool_w, pool_scale, w_out, w_mlp_up, w_mlp_down):
    bp, seq, _ = x_prompt.shape
    bs = x_sample.shape[0]

    a_re, a_im, bm, cmt, glu, poolbd = _s5_prep(
        s5_lambda_re, s5_lambda_im, s5_log_dt, s5_b_re, s5_b_im, s5_c_re, s5_c_im,
        s5_w_glu, pool_w)
    row = lambda v: v.reshape(1, -1)
    seq_params = [row(norm_mix_pre), row(norm_mix_post), a_re, a_im, bm, cmt, row(s5_d),
                  glu, poolbd, row(pool_scale)]

    st_in = lambda a: jnp.transpose(a, (1, 2, 0)).reshape(N_STATE, bs)
    st_out = lambda a: jnp.transpose(a.reshape(S5_GROUPS, S5_STATE, bs), (2, 0, 1))
    tbc = lambda a: jnp.transpose(a, (1, 0, 2))

    h1p, hpr, hpi, pool_p, h1s, hsr, hsi, pool_s = _mixer(
        x_prompt, x_sample, st_in(state_s5_re), st_in(state_s5_im), tbc(state_pool),
        w_in, w_out, seq_params, steps=128, start_pos_s=PAST_LEN)
    yp, y_sample = _mlp(h1p.reshape(bp * seq, D_MODEL), h1s, row(norm_mlp_pre),
                        row(norm_mlp_post), w_mlp_up, w_mlp_down, block_rows=1024)

    return (yp.reshape(bp, seq, D_MODEL), y_sample, hpr, hpi, tbc(pool_p),
            st_out(hsr), st_out(hsi), tbc(pool_s))
```

```python
import functools
import math

import jax
import jax.numpy as jnp
from jax import lax
from jax.experimental import pallas as pl
from jax.experimental.pallas import tpu as pltpu

F32 = jnp.float32
BF16 = jnp.bfloat16

D_MODEL = 1024
W_A = 512
W_B = 512
S5_H = 16
S5_GROUPS = 32
S5_STATE = 64
N_STATE = S5_GROUPS * S5_STATE
N_SLAB = 4
SLAB_GROUPS = S5_GROUPS // N_SLAB
SLAB_U = SLAB_GROUPS * S5_H
SLAB_S = SLAB_GROUPS * S5_STATE
POOL_WINDOWS = (2, 4, 8, 16)
POOL_CH = 128
POOL_HIST = 16
D_FF = 4096
FF_CHUNK = 1024
EPS = 1e-6
PAST_LEN = 16384
GELU_C = math.sqrt(2.0 / math.pi)

VMEM_LIMIT_BYTES = 62 * 1024 * 1024


def _rms_norm(x, g):
    ms = jnp.mean(x * x, axis=-1, keepdims=True)
    return x * lax.rsqrt(ms + EPS) * g


def _full_spec(shape):
    return pl.BlockSpec(shape, lambda *_: (0,) * len(shape))


def _s5_prep_kernel(lam_re_ref, lam_im_ref, log_dt_ref, b_re_ref, b_im_ref,
                    c_re_ref, c_im_ref, wglu_ref, poolw_ref,
                    a_re_ref, a_im_ref, bm_ref, cmt_ref, glu_ref, poolbd_ref):
    lam_re = lam_re_ref[...]
    lam_im = lam_im_ref[...]
    eye = (lax.broadcasted_iota(jnp.int32, (S5_GROUPS, S5_GROUPS), 0)
           == lax.broadcasted_iota(jnp.int32, (S5_GROUPS, S5_GROUPS), 1))
    log_dt = jnp.sum(jnp.where(eye, log_dt_ref[...], 0.0), axis=1, keepdims=True)
    dt = jnp.exp(log_dt)
    mag = jnp.exp(lam_re * dt)
    ang = lam_im * dt
    a_re = mag * jnp.cos(ang)
    a_im = mag * jnp.sin(ang)
    lanes = lambda m: jnp.concatenate([m[g:g + 1, :] for g in range(S5_GROUPS)], axis=1)
    a_re_ref[...] = lanes(a_re)
    a_im_ref[...] = lanes(a_im)
    n_re = a_re - 1.0
    n_im = a_im
    den = lam_re * lam_re + lam_im * lam_im
    k_re = (n_re * lam_re + n_im * lam_im) / den
    k_im = (n_im * lam_re - n_re * lam_im) / den
    per_h = lambda m: jnp.broadcast_to(m[:, None, :], (S5_GROUPS, S5_H, S5_STATE)).reshape(
        S5_GROUPS * S5_H, S5_STATE)
    k_re = per_h(k_re)
    k_im = per_h(k_im)
    b_re = b_re_ref[...]
    b_im = b_im_ref[...]
    bb_re = k_re * b_re - k_im * b_im
    bb_im = k_re * b_im + k_im * b_re
    c_re = c_re_ref[...]
    c_im_neg = -c_im_ref[...]

    rows = lax.broadcasted_iota(jnp.int32, (SLAB_U, SLAB_S), 0) // S5_H
    cols = lax.broadcasted_iota(jnp.int32, (SLAB_U, SLAB_S), 1) // S5_STATE
    diag = rows == cols

    def block_diag(m, j):
        sl = m[j * SLAB_U:(j + 1) * SLAB_U, :]
        tiled = jnp.concatenate([sl] * SLAB_GROUPS, axis=1)
        return jnp.where(diag, tiled, 0.0)

    for j in range(N_SLAB):
        bm_ref[j, :, :SLAB_S] = block_diag(bb_re, j).astype(BF16)
        bm_ref[j, :, SLAB_S:] = block_diag(bb_im, j).astype(BF16)
        cmt_ref[j, :, :SLAB_S] = block_diag(c_re, j).astype(BF16)
        cmt_ref[j, :, SLAB_S:] = block_diag(c_im_neg, j).astype(BF16)

    half = W_A // 2
    g_rows = lax.broadcasted_iota(jnp.int32, (half, half), 0) // S5_H
    g_cols = lax.broadcasted_iota(jnp.int32, (half, half), 1) // S5_H
    for t in range(2):
        blk = wglu_ref[t * half:(t + 1) * half, :]
        tiled = jnp.concatenate([blk] * (half // S5_H), axis=1)
        glu_ref[t] = jnp.where(g_rows == g_cols, tiled, 0.0).astype(BF16)

    zeros = jnp.zeros((POOL_CH, POOL_CH), BF16)
    for t in range(2):
        poolbd_ref[t, :POOL_CH, :POOL_CH] = poolw_ref[2 * t].astype(BF16)
        poolbd_ref[t, :POOL_CH, POOL_CH:] = zeros
        poolbd_ref[t, POOL_CH:, :POOL_CH] = zeros
        poolbd_ref[t, POOL_CH:, POOL_CH:] = poolw_ref[2 * t + 1].astype(BF16)


def _s5_prep(lam_re, lam_im, log_dt, b_re, b_im, c_re, c_im, w_glu, pool_w):
    gh_p = lambda b: jnp.transpose(b, (0, 2, 1)).reshape(S5_GROUPS * S5_H, S5_STATE)
    ins = (lam_re, lam_im, log_dt.reshape(1, S5_GROUPS), gh_p(b_re), gh_p(b_im),
           c_re.reshape(S5_GROUPS * S5_H, S5_STATE),
           c_im.reshape(S5_GROUPS * S5_H, S5_STATE),
           w_glu.reshape(S5_GROUPS * S5_H, S5_H), pool_w)
    out_shape = (
        jax.ShapeDtypeStruct((1, N_STATE), F32),
        jax.ShapeDtypeStruct((1, N_STATE), F32),
        jax.ShapeDtypeStruct((N_SLAB, SLAB_U, 2 * SLAB_S), BF16),
        jax.ShapeDtypeStruct((N_SLAB, SLAB_U, 2 * SLAB_S), BF16),
        jax.ShapeDtypeStruct((2, W_A // 2, W_A // 2), BF16),
        jax.ShapeDtypeStruct((2, 2 * POOL_CH, 2 * POOL_CH), BF16),
    )
    return pl.pallas_call(
        _s5_prep_kernel,
        out_shape=out_shape,
        name="s5_prep",
    )(*ins)


def _mixer_math(x, i, g_pre_ref, g_post_ref, win_ref, a_re_ref, a_im_ref, bm_ref,
                cmt_ref, d_ref, glu_ref, poolw_ref, pscale_ref, wout_ref,
                slab_ref, hsbf_ref, hr_s, hi_s, hist_s, *, batch, steps, start_pos,
                before_output_projection=None):
    rows = batch * steps
    xn = _rms_norm(x, g_pre_ref[...]).astype(BF16)
    w_in = win_ref[...].reshape(D_MODEL, D_MODEL)
    ua = jnp.dot(xn, w_in[:, :W_A], preferred_element_type=F32)
    ua_bf = ua.astype(BF16)

    pair = max(1, 16 // batch)

    def project_in(j):
        slab_ref[j % 2] = jnp.dot(ua_bf[:, j * SLAB_U:(j + 1) * SLAB_U], bm_ref[j],
                                  preferred_element_type=F32)

    def scan(j):
        st = slice(j * SLAB_S, (j + 1) * SLAB_S)
        sb = slab_ref.at[j % 2]
        hb = hsbf_ref.at[j % 2]
        ar = jnp.broadcast_to(a_re_ref[:, st], (batch, SLAB_S))
        ai = jnp.broadcast_to(a_im_ref[:, st], (batch, SLAB_S))
        hr = hr_s[:, st]
        hi = hi_s[:, st]
        for t0 in range(0, steps, pair):
            res, ims = [], []
            for t in range(t0, min(t0 + pair, steps)):
                rt = slice(t * batch, (t + 1) * batch)
                nr = ar * hr - ai * hi + sb[rt, :SLAB_S]
                ni = ar * hi + ai * hr + sb[rt, SLAB_S:]
                res.append(nr)
                ims.append(ni)
                hr, hi = nr, ni
            rg = slice(t0 * batch, (t0 + len(res)) * batch)
            hb[rg, :SLAB_S] = jnp.concatenate(res, axis=0).astype(BF16)
            hb[rg, SLAB_S:] = jnp.concatenate(ims, axis=0).astype(BF16)
        hr_s[:, st] = hr
        hi_s[:, st] = hi

    def project_out(j):
        return lax.dot_general(hsbf_ref[j % 2], cmt_ref[j], (((1,), (1,)), ((), ())),
                               preferred_element_type=F32)

    ys = []
    project_in(0)
    scan(0)
    ub = jnp.dot(xn, w_in[:, W_A:], preferred_element_type=F32)
    for j in range(1, N_SLAB):
        project_in(j)
        ys.append(project_out(j - 1))
        scan(j)
    ys.append(project_out(N_SLAB - 1))

    y = jnp.concatenate(ys, axis=1) + d_ref[...] * ua
    y = y * (0.5 * (1.0 + jnp.tanh(GELU_C * (y + 0.044715 * (y * y * y)))))
    y_bf = y.astype(BF16)
    half = W_A // 2
    gate = jnp.concatenate(
        [jnp.dot(y_bf[:, :half], glu_ref[0], preferred_element_type=F32),
         jnp.dot(y_bf[:, half:], glu_ref[1], preferred_element_type=F32)], axis=1)
    ya = y * (1.0 / (1.0 + jnp.exp(-gate)))

    hist = hist_s[...].reshape(POOL_HIST * batch, W_B)
    ext = jnp.concatenate([hist, ub], axis=0)
    n_ext = POOL_HIST * batch + rows
    hist_s[...] = ext[n_ext - POOL_HIST * batch:, :].reshape(POOL_HIST, batch, W_B)
    t_loc = lax.broadcasted_iota(jnp.int32, (rows, 1), 0) // batch
    pos1 = t_loc + (start_pos + 1) + i * steps
    pooled = []
    for gi, w in enumerate(POOL_WINDOWS):
        s = ext[:, gi * POOL_CH:(gi + 1) * POOL_CH]
        span = 1
        while span < w:
            n = s.shape[0]
            s = s[span * batch:, :] + s[:n - span * batch, :]
            span *= 2
        win = s[s.shape[0] - rows:, :]
        count = jnp.minimum(pos1, w).astype(F32)
        pooled.append(win / count - ub[:, gi * POOL_CH:(gi + 1) * POOL_CH])
    pooled = jnp.concatenate(pooled, axis=1).astype(BF16)
    halfb = W_B // 2
    yb = jnp.concatenate(
        [jnp.dot(pooled[:, :halfb], poolw_ref[0], preferred_element_type=F32),
         jnp.dot(pooled[:, halfb:], poolw_ref[1], preferred_element_type=F32)], axis=1)
    yb = yb * pscale_ref[...]

    ycat = jnp.concatenate([ya, yb], axis=1).astype(BF16)
    if before_output_projection is not None:
        before_output_projection()
    mix =jnp.dot(ycat, wout_ref[...].reshape(D_MODEL, D_MODEL),
                  preferred_element_type=F32)
    return x + _rms_norm(mix, g_post_ref[...])


def _mlp_math(h, g_pre_ref, g_post_ref, wup_ref, wdown_ref):
    hn = _rms_norm(h, g_pre_ref[...]).astype(BF16)
    acc = None
    for j in range(D_FF // FF_CHUNK):
        sl = slice(j * FF_CHUNK, (j + 1) * FF_CHUNK)
        up = jnp.dot(hn, wup_ref[:, sl], preferred_element_type=F32)
        up = jnp.maximum(up, 0.0)
        ff = (up * up).astype(BF16)
        part = jnp.dot(ff, wdown_ref[sl, :], preferred_element_type=F32)
        acc = part if acc is None else acc + part
    return h + _rms_norm(acc, g_post_ref[...])


def _mixer_scratch(batch, steps):
    return [
        pltpu.VMEM((2, batch * steps, 2 * SLAB_S), F32),
        pltpu.VMEM((2, batch * steps, 2 * SLAB_S), BF16),
        pltpu.VMEM((batch, N_STATE), F32),
        pltpu.VMEM((batch, N_STATE), F32),
        pltpu.VMEM((POOL_HIST, batch, W_B), F32),
    ]


_VMEM_WHOLE = pl.BlockSpec(memory_space=pltpu.VMEM)
_HBM = pl.BlockSpec(memory_space=pl.ANY)
_COMPILER_PARAMS = pltpu.CompilerParams(dimension_semantics=("arbitrary",),
                                        vmem_limit_bytes=VMEM_LIMIT_BYTES)


def _cast_weight(w_hbm, dst_ref, stage_ref, sem, row_chunk):
    n = w_hbm.shape[0] // row_chunk

    def copy(c):
        return pltpu.make_async_copy(w_hbm.at[pl.ds(c * row_chunk, row_chunk)],
                                     stage_ref.at[c % 2], sem.at[c % 2])

    copy(0).start()
    for c in range(n):
        if c + 1 < n:
            copy(c + 1).start()
        copy(c).wait()
        dst_ref[pl.ds(c * row_chunk, row_chunk)] = stage_ref[c % 2].astype(BF16)


N_SEQ_PARAM = 10


def _mixer_kernel(x_hbm, xs_hbm, h0r_ref, h0i_ref, pool0_hbm, win_hbm, wout_hbm,
                  *refs, batch, steps, batch_s, start_pos_s):
    g_pre, g_post, a_re, a_im, bm, cmt, d, glu, poolbd, pscale = refs[:N_SEQ_PARAM]
    (h1_hbm, hr_out, hi_out, pool_out, h1s_out, hrs_out, his_out,
     pools_hbm) = refs[N_SEQ_PARAM:N_SEQ_PARAM + 8]
    (slab, hsbf, hr_s, hi_s, hist_s, hrs_s, his_s, hists_s, xs, win_bf, wout_bf, xbuf, hbuf,
     in_sem, out_sem, s_sem, w_sem) = refs[N_SEQ_PARAM + 8:]
    params = (g_pre, g_post, win_bf, a_re, a_im, bm, cmt, d, glu, poolbd, pscale, wout_bf)
    i = pl.program_id(0)
    n_chunks = pl.num_programs(0)
    slot = lax.rem(i, 2)

    def x_copy(b, chunk, sl):
        return pltpu.make_async_copy(
            x_hbm.at[b, pl.ds(chunk * steps, steps), :], xbuf.at[sl, :, b, :],
            in_sem.at[sl])

    def h_copy(b, chunk):
        return pltpu.make_async_copy(
            hbuf.at[0, :, b, :], h1_hbm.at[b, pl.ds(chunk * steps, steps), :],
            out_sem.at[0])

    xs_copy = pltpu.make_async_copy(xs_hbm.at[:, 0, :], xs, s_sem.at[0])
    pool_in = pltpu.make_async_copy(pool0_hbm, hists_s.at[pl.ds(1, POOL_HIST - 1)],
                                    s_sem.at[1])
    pool_o = pltpu.make_async_copy(hists_s.at[pl.ds(1, POOL_HIST - 1)], pools_hbm,
                                   s_sem.at[1])

    @pl.when(i == 0)
    def _():
        for b in range(batch):
            x_copy(b, 0, 0).start()
        xs_copy.start()
        pool_in.start()
        hr_s[...] = jnp.zeros_like(hr_s)
        hi_s[...] = jnp.zeros_like(hi_s)
        hist_s[...] = jnp.zeros_like(hist_s)
        _cast_weight(win_hbm, win_bf, hbuf, w_sem, steps)
        _cast_weight(wout_hbm, wout_bf, hbuf, w_sem, steps)

    @pl.when(i + 1 < n_chunks)
    def _():
        for b in range(batch):
            x_copy(b, i + 1, 1 - slot).start()

    for b in range(batch):
        x_copy(b, i, slot).wait()
    x = xbuf[slot].reshape(steps * batch, D_MODEL)
    def wait_previous_output():
        @pl.when(i >= 1)
        def _():
            for b in range(batch):
                h_copy(b, i - 1).wait()

    h1 = _mixer_math(x, i, *params, slab, hsbf, hr_s, hi_s, hist_s,
                     batch=batch, steps=steps, start_pos=0,
                     before_output_projection=wait_previous_output)
    hbuf[0] = h1.reshape(steps, batch, D_MODEL)
    for b in range(batch):
        h_copy(b, i).start()

    @pl.when(i == n_chunks - 1)
    def _():
        for g in range(S5_GROUPS):
            hr_out[:, g, :] = hr_s[:, g * S5_STATE:(g + 1) * S5_STATE]
            hi_out[:, g, :] = hi_s[:, g * S5_STATE:(g + 1) * S5_STATE]
        pool_out[...] = hist_s[pl.ds(1, POOL_HIST - 1)]

        xs_copy.wait()
        pool_in.wait()
        hists_s[0] = jnp.zeros((batch_s, W_B), F32)
        hrs_s[...] = h0r_ref[...].T
        his_s[...] = h0i_ref[...].T
        h1s_out[...] = _mixer_math(
            xs[...], 0, *params, slab.at[:, pl.ds(0, batch_s), :],
            hsbf.at[:, pl.ds(0, batch_s), :], hrs_s, his_s, hists_s,
            batch=batch_s, steps=1, start_pos=start_pos_s)
        pool_o.start()
        hrs_out[...] = hrs_s[...].T
        his_out[...] = his_s[...].T
        for b in range(batch):
            h_copy(b, i).wait()
        pool_o.wait()


def _mixer(x, xs, h0r_t, h0i_t, pool0_t, w_in, w_out, seq_params, *, steps, start_pos_s):
    batch, seq, _ = x.shape
    batch_s = xs.shape[0]
    assert len(seq_params) == N_SEQ_PARAM and seq % steps == 0 and batch_s <= steps * batch
    f32 = lambda *shape: jax.ShapeDtypeStruct(shape, F32)
    w3 = lambda w: w.reshape(D_MODEL // batch, batch, D_MODEL)
    return pl.pallas_call(
        functools.partial(_mixer_kernel, batch=batch, steps=steps, batch_s=batch_s,
                          start_pos_s=start_pos_s),
        out_shape=(f32(*x.shape), f32(batch, S5_GROUPS, S5_STATE),
                   f32(batch, S5_GROUPS, S5_STATE), f32(POOL_HIST - 1, batch, W_B),
                   f32(batch_s, D_MODEL), f32(N_STATE, batch_s), f32(N_STATE, batch_s),
                   f32(*pool0_t.shape)),
        grid=(seq // steps,),
        in_specs=[_HBM, _HBM, _VMEM_WHOLE, _VMEM_WHOLE, _HBM, _HBM, _HBM]
        + [_VMEM_WHOLE] * N_SEQ_PARAM,
        out_specs=(_HBM, _full_spec((batch, S5_GROUPS, S5_STATE)),
                   _full_spec((batch, S5_GROUPS, S5_STATE)),
                   _full_spec((POOL_HIST - 1, batch, W_B)),
                   _full_spec((batch_s, D_MODEL)), _full_spec((N_STATE, batch_s)),
                   _full_spec((N_STATE, batch_s)), _HBM),
        scratch_shapes=_mixer_scratch(batch, steps) + [
            pltpu.VMEM((batch_s, N_STATE), F32),
            pltpu.VMEM((batch_s, N_STATE), F32),
            pltpu.VMEM((POOL_HIST, batch_s, W_B), F32),
            pltpu.VMEM((batch_s, D_MODEL), F32),
            pltpu.VMEM((D_MODEL // batch, batch, D_MODEL), BF16),
            pltpu.VMEM((D_MODEL // batch, batch, D_MODEL), BF16),
            pltpu.VMEM((2, steps, batch, D_MODEL), F32),
            pltpu.VMEM((1, steps, batch, D_MODEL), F32),
            pltpu.SemaphoreType.DMA((2,)),
            pltpu.SemaphoreType.DMA((1,)),
            pltpu.SemaphoreType.DMA((2,)),
            pltpu.SemaphoreType.DMA((2,)),
        ],
        compiler_params=_COMPILER_PARAMS,
        name="mixer",
    )(x, xs, h0r_t, h0i_t, pool0_t, w3(w_in), w3(w_out), *seq_params)


MLP_STAGE_ROWS = 128


def _mlp_kernel(h_ref, hs_ref, g_pre_ref, g_post_ref, wup_hbm, wdown_hbm,
                o_ref, ys_hbm, wup_bf, wdown_bf, stage_up, stage_down, ys, w_sem, y_sem):
    i = pl.program_id(0)

    @pl.when(i == 0)
    def _():
        _cast_weight(wup_hbm, wup_bf, stage_up, w_sem, MLP_STAGE_ROWS)
        _cast_weight(wdown_hbm, wdown_bf, stage_down, w_sem, 4 * MLP_STAGE_ROWS)

    o_ref[...] = _mlp_math(h_ref[...], g_pre_ref, g_post_ref, wup_bf, wdown_bf)

    @pl.when(i == pl.num_programs(0) - 1)
    def _():
        ys[...] = _mlp_math(hs_ref[...], g_pre_ref, g_post_ref, wup_bf, wdown_bf)
        y_out = pltpu.make_async_copy(ys, ys_hbm.at[:, 0, :], y_sem.at[0])
        y_out.start()
        y_out.wait()


def _mlp(h, hs, g_pre, g_post, wup, wdown, *, block_rows):
    n_rows = h.shape[0]
    batch_s = hs.shape[0]
    assert n_rows % block_rows == 0
    return pl.pallas_call(
        _mlp_kernel,
        out_shape=(jax.ShapeDtypeStruct((n_rows, D_MODEL), F32),
                   jax.ShapeDtypeStruct((batch_s, 1, D_MODEL), F32)),
        grid=(n_rows // block_rows,),
        in_specs=[
            pl.BlockSpec((block_rows, D_MODEL), lambda i: (i, 0)),
            _VMEM_WHOLE, _VMEM_WHOLE, _VMEM_WHOLE, _HBM, _HBM,
        ],
        out_specs=(pl.BlockSpec((block_rows, D_MODEL), lambda i: (i, 0)), _HBM),
        scratch_shapes=[
            pltpu.VMEM((D_MODEL, D_FF), BF16),
            pltpu.VMEM((D_FF, D_MODEL), BF16),
            pltpu.VMEM((2, MLP_STAGE_ROWS, D_FF), F32),
            pltpu.VMEM((2, 4 * MLP_STAGE_ROWS, D_MODEL), F32),
            pltpu.VMEM((batch_s, D_MODEL), F32),
            pltpu.SemaphoreType.DMA((2,)),
            pltpu.SemaphoreType.DMA((1,)),
        ],
        compiler_params=_COMPILER_PARAMS,
        name="mlp",
    )(h, hs, g_pre, g_post, wup, wdown)


def kernel(x_prompt, x_sample, state_s5_re, state_s5_im, state_pool, norm_mix_pre, norm_mix_post, norm_mlp_pre, norm_mlp_post, w_in, s5_lambda_re, s5_lambda_im, s5_log_dt, s5_b_re, s5_b_im, s5_c_re, s5_c_im, s5_d, s5_w_glu, pool_w, pool_scale, w_out, w_mlp_up, w_mlp_down):
    bp, seq, _ = x_prompt.shape
    bs = x_sample.shape[0]

    a_re, a_im, bm, cmt, glu, poolbd = _s5_prep(
        s5_lambda_re, s5_lambda_im, s5_log_dt, s5_b_re, s5_b_im, s5_c_re, s5_c_im,
        s5_w_glu, pool_w)
    row = lambda v: v.reshape(1, -1)
    seq_params = [row(norm_mix_pre), row(norm_mix_post), a_re, a_im, bm, cmt, row(s5_d),
                  glu, poolbd, row(pool_scale)]

    st_in = lambda a: jnp.transpose(a, (1, 2, 0)).reshape(N_STATE, bs)
    st_out = lambda a: jnp.transpose(a.reshape(S5_GROUPS, S5_STATE, bs), (2, 0, 1))
    tbc = lambda a: jnp.transpose(a, (1, 0, 2))

    h1p, hpr, hpi, pool_p, h1s, hsr, hsi, pool_s = _mixer(
        x_prompt, x_sample, st_in(state_s5_re), st_in(state_s5_im), tbc(state_pool),
        w_in, w_out, seq_params, steps=128, start_pos_s=PAST_LEN)
    yp, y_sample = _mlp(h1p.reshape(bp * seq, D_MODEL), h1s, row(norm_mlp_pre),
                        row(norm_mlp_post), w_mlp_up, w_mlp_down, block_rows=1024)

    return (yp.reshape(bp, seq, D_MODEL), y_sample, hpr, hpi, tbc(pool_p),
            st_out(hsr), st_out(hsi), tbc(pool_s))
```

```python
import functools
import math

import jax
import jax.numpy as jnp
from jax import lax
from jax.experimental import pallas as pl
from jax.experimental.pallas import tpu as pltpu

F32 = jnp.float32
BF16 = jnp.bfloat16

D_MODEL = 1024
W_A = 512
W_B = 512
S5_H = 16
S5_GROUPS = 32
S5_STATE = 64
N_STATE = S5_GROUPS * S5_STATE
N_SLAB = 4
SLAB_GROUPS = S5_GROUPS // N_SLAB
SLAB_U = SLAB_GROUPS * S5_H
SLAB_S = SLAB_GROUPS * S5_STATE
POOL_WINDOWS = (2, 4, 8, 16)
POOL_CH = 128
POOL_HIST = 16
D_FF = 4096
FF_CHUNK = 1024
EPS = 1e-6
PAST_LEN = 16384
PROMPT_STEPS = 128
GELU_C = math.sqrt(2.0 / math.pi)

VMEM_LIMIT_BYTES = 58 * 1024 * 1024


def _rms_norm(x, g):
    ms = jnp.mean(x * x, axis=-1, keepdims=True)
    return x * lax.rsqrt(ms + EPS) * g


def _full_spec(shape):
    return pl.BlockSpec(shape, lambda *_: (0,) * len(shape))


def _s5_prep_kernel(lam_re_ref, lam_im_ref, log_dt_ref, b_re_ref, b_im_ref,
                    c_re_ref, c_im_ref, wglu_ref, poolw_ref,
                    a_re_ref, a_im_ref, bm_ref, cmt_ref, glu_ref, poolbd_ref):
    lam_re = lam_re_ref[...]
    lam_im = lam_im_ref[...]
    eye = (lax.broadcasted_iota(jnp.int32, (S5_GROUPS, S5_GROUPS), 0)
           == lax.broadcasted_iota(jnp.int32, (S5_GROUPS, S5_GROUPS), 1))
    log_dt = jnp.sum(jnp.where(eye, log_dt_ref[...], 0.0), axis=1, keepdims=True)
    dt = jnp.exp(log_dt)
    mag = jnp.exp(lam_re * dt)
    ang = lam_im * dt
    a_re = mag * jnp.cos(ang)
    a_im = mag * jnp.sin(ang)
    lanes = lambda m: jnp.concatenate([m[g:g + 1, :] for g in range(S5_GROUPS)], axis=1)
    a_re_ref[...] = lanes(a_re)
    a_im_ref[...] = lanes(a_im)
    n_re = a_re - 1.0
    n_im = a_im
    den = lam_re * lam_re + lam_im * lam_im
    k_re = (n_re * lam_re + n_im * lam_im) / den
    k_im = (n_im * lam_re - n_re * lam_im) / den
    per_h = lambda m: jnp.broadcast_to(m[:, None, :], (S5_GROUPS, S5_H, S5_STATE)).reshape(
        S5_GROUPS * S5_H, S5_STATE)
    k_re = per_h(k_re)
    k_im = per_h(k_im)
    b_re = b_re_ref[...]
    b_im = b_im_ref[...]
    bb_re = k_re * b_re - k_im * b_im
    bb_im = k_re * b_im + k_im * b_re
    c_re = c_re_ref[...]
    c_im_neg = -c_im_ref[...]

    rows = lax.broadcasted_iota(jnp.int32, (SLAB_U, SLAB_S), 0) // S5_H
    cols = lax.broadcasted_iota(jnp.int32, (SLAB_U, SLAB_S), 1) // S5_STATE
    diag = rows == cols

    def block_diag(m, j):
        sl = m[j * SLAB_U:(j + 1) * SLAB_U, :]
        tiled = jnp.concatenate([sl] * SLAB_GROUPS, axis=1)
        return jnp.where(diag, tiled, 0.0)

    for j in range(N_SLAB):
        bm_ref[j, :, :SLAB_S] = block_diag(bb_re, j).astype(BF16)
        bm_ref[j, :, SLAB_S:] = block_diag(bb_im, j).astype(BF16)
        cmt_ref[j, :, :SLAB_S] = block_diag(c_re, j).astype(BF16)
        cmt_ref[j, :, SLAB_S:] = block_diag(c_im_neg, j).astype(BF16)

    half = W_A // 2
    g_rows = lax.broadcasted_iota(jnp.int32, (half, half), 0) // S5_H
    g_cols = lax.broadcasted_iota(jnp.int32, (half, half), 1) // S5_H
    for t in range(2):
        blk = wglu_ref[t * half:(t + 1) * half, :]
        tiled = jnp.concatenate([blk] * (half // S5_H), axis=1)
        glu_ref[t] = jnp.where(g_rows == g_cols, tiled, 0.0).astype(BF16)

    zeros = jnp.zeros((POOL_CH, POOL_CH), BF16)
    for t in range(2):
        poolbd_ref[t, :POOL_CH, :POOL_CH] = poolw_ref[2 * t].astype(BF16)
        poolbd_ref[t, :POOL_CH, POOL_CH:] = zeros
        poolbd_ref[t, POOL_CH:, :POOL_CH] = zeros
        poolbd_ref[t, POOL_CH:, POOL_CH:] = poolw_ref[2 * t + 1].astype(BF16)


def _s5_prep(lam_re, lam_im, log_dt, b_re, b_im, c_re, c_im, w_glu, pool_w):
    gh_p = lambda b: jnp.transpose(b, (0, 2, 1)).reshape(S5_GROUPS * S5_H, S5_STATE)
    ins = (lam_re, lam_im, log_dt.reshape(1, S5_GROUPS), gh_p(b_re), gh_p(b_im),
           c_re.reshape(S5_GROUPS * S5_H, S5_STATE),
           c_im.reshape(S5_GROUPS * S5_H, S5_STATE),
           w_glu.reshape(S5_GROUPS * S5_H, S5_H), pool_w)
    out_shape = (
        jax.ShapeDtypeStruct((1, N_STATE), F32),
        jax.ShapeDtypeStruct((1, N_STATE), F32),
        jax.ShapeDtypeStruct((N_SLAB, SLAB_U, 2 * SLAB_S), BF16),
        jax.ShapeDtypeStruct((N_SLAB, SLAB_U, 2 * SLAB_S), BF16),
        jax.ShapeDtypeStruct((2, W_A // 2, W_A // 2), BF16),
        jax.ShapeDtypeStruct((2, 2 * POOL_CH, 2 * POOL_CH), BF16),
    )
    return pl.pallas_call(
        _s5_prep_kernel,
        out_shape=out_shape,
        name="s5_prep",
    )(*ins)


def _mixer_math(x, i, g_pre_ref, g_post_ref, win_ref, a_re_ref, a_im_ref, bm_ref,
                cmt_ref, d_ref, glu_ref, poolw_ref, pscale_ref, wout_ref,
                slab_ref, hsbf_ref, hr_s, hi_s, hist_s, *, batch, steps, start_pos):
    rows = batch * steps
    xn = _rms_norm(x, g_pre_ref[...]).astype(BF16)
    w_in = win_ref[...].reshape(D_MODEL, D_MODEL)
    ua = jnp.dot(xn, w_in[:, :W_A], preferred_element_type=F32)
    ua_bf = ua.astype(BF16)

    pair = max(1, 16 // batch)

    def project_in(j):
        slab_ref[j % 2] = jnp.dot(ua_bf[:, j * SLAB_U:(j + 1) * SLAB_U], bm_ref[j],
                                  preferred_element_type=F32)

    def scan(j):
        st = slice(j * SLAB_S, (j + 1) * SLAB_S)
        sb = slab_ref.at[j % 2]
        hb = hsbf_ref.at[j % 2]
        ar = jnp.broadcast_to(a_re_ref[:, st], (batch, SLAB_S))
        ai = jnp.broadcast_to(a_im_ref[:, st], (batch, SLAB_S))
        hr = hr_s[:, st]
        hi = hi_s[:, st]
        for t0 in range(0, steps, pair):
            res, ims = [], []
            for t in range(t0, min(t0 + pair, steps)):
                rt = slice(t * batch, (t + 1) * batch)
                nr = ar * hr - ai * hi + sb[rt, :SLAB_S]
                ni = ar * hi + ai * hr + sb[rt, SLAB_S:]
                res.append(nr)
                ims.append(ni)
                hr, hi = nr, ni
            rg = slice(t0 * batch, (t0 + len(res)) * batch)
            hb[rg, :SLAB_S] = jnp.concatenate(res, axis=0).astype(BF16)
            hb[rg, SLAB_S:] = jnp.concatenate(ims, axis=0).astype(BF16)
        hr_s[:, st] = hr
        hi_s[:, st] = hi

    def project_out(j):
        return lax.dot_general(hsbf_ref[j % 2], cmt_ref[j], (((1,), (1,)), ((), ())),
                               preferred_element_type=F32)

    ys = []
    project_in(0)
    scan(0)
    ub = jnp.dot(xn, w_in[:, W_A:], preferred_element_type=F32)
    for j in range(1, N_SLAB):
        project_in(j)
        ys.append(project_out(j - 1))
        scan(j)
    ys.append(project_out(N_SLAB - 1))

    y = jnp.concatenate(ys, axis=1) + d_ref[...] * ua
    y = y * (0.5 * (1.0 + jnp.tanh(GELU_C * (y + 0.044715 * (y * y * y)))))
    y_bf = y.astype(BF16)
    half = W_A // 2
    gate = jnp.concatenate(
        [jnp.dot(y_bf[:, :half], glu_ref[0], preferred_element_type=F32),
         jnp.dot(y_bf[:, half:], glu_ref[1], preferred_element_type=F32)], axis=1)
    ya = y * (1.0 / (1.0 + jnp.exp(-gate)))

    hist = hist_s[...].reshape(POOL_HIST * batch, W_B)
    ext = jnp.concatenate([hist, ub], axis=0)
    n_ext = POOL_HIST * batch + rows
    hist_s[...] = ext[n_ext - POOL_HIST * batch:, :].reshape(POOL_HIST, batch, W_B)
    t_loc = lax.broadcasted_iota(jnp.int32, (rows, 1), 0) // batch
    pos1 = t_loc + (start_pos + 1) + i * steps
    pooled = []
    for gi, w in enumerate(POOL_WINDOWS):
        s = ext[:, gi * POOL_CH:(gi + 1) * POOL_CH]
        span = 1
        while span < w:
            n = s.shape[0]
            s = s[span * batch:, :] + s[:n - span * batch, :]
            span *= 2
        win = s[s.shape[0] - rows:, :]
        count = jnp.minimum(pos1, w).astype(F32)
        pooled.append(win / count - ub[:, gi * POOL_CH:(gi + 1) * POOL_CH])
    pooled = jnp.concatenate(pooled, axis=1).astype(BF16)
    halfb = W_B // 2
    yb = jnp.concatenate(
        [jnp.dot(pooled[:, :halfb], poolw_ref[0], preferred_element_type=F32),
         jnp.dot(pooled[:, halfb:], poolw_ref[1], preferred_element_type=F32)], axis=1)
    yb = yb * pscale_ref[...]

    ycat = jnp.concatenate([ya, yb], axis=1).astype(BF16)
    mix = jnp.dot(ycat, wout_ref[...].reshape(D_MODEL, D_MODEL),
                  preferred_element_type=F32)
    return x + _rms_norm(mix, g_post_ref[...])


def _mlp_math(h, g_pre_ref, g_post_ref, wup_ref, wdown_ref):
    hn = _rms_norm(h, g_pre_ref[...]).astype(BF16)
    acc = None
    for j in range(D_FF // FF_CHUNK):
        sl = slice(j * FF_CHUNK, (j + 1) * FF_CHUNK)
        up = jnp.dot(hn, wup_ref[:, sl], preferred_element_type=F32)
        up = jnp.maximum(up, 0.0)
        ff = (up * up).astype(BF16)
        part = jnp.dot(ff, wdown_ref[sl, :], preferred_element_type=F32)
        acc = part if acc is None else acc + part
    return h + _rms_norm(acc, g_post_ref[...])


def _mixer_scratch(batch, steps):
    return [
        pltpu.VMEM((2, batch * steps, 2 * SLAB_S), F32),
        pltpu.VMEM((2, batch * steps, 2 * SLAB_S), BF16),
        pltpu.VMEM((batch, N_STATE), F32),
        pltpu.VMEM((batch, N_STATE), F32),
        pltpu.VMEM((POOL_HIST, batch, W_B), F32),
    ]


_VMEM_WHOLE = pl.BlockSpec(memory_space=pltpu.VMEM)
_HBM = pl.BlockSpec(memory_space=pl.ANY)
_COMPILER_PARAMS = pltpu.CompilerParams(dimension_semantics=("arbitrary",),
                                        vmem_limit_bytes=VMEM_LIMIT_BYTES)


def _cast_weight(w_hbm, dst_ref, stage_ref, sem, row_chunk):
    n = w_hbm.shape[0] // row_chunk

    def copy(c):
        return pltpu.make_async_copy(w_hbm.at[pl.ds(c * row_chunk, row_chunk)],
                                     stage_ref.at[c % 2], sem.at[c % 2])

    copy(0).start()
    for c in range(n):
        if c + 1 < n:
            copy(c + 1).start()
        copy(c).wait()
        dst_ref[pl.ds(c * row_chunk, row_chunk)] = stage_ref[c % 2].astype(BF16)


N_SEQ_PARAM = 10


def _mixer_kernel(x_hbm, xs_hbm, h0r_ref, h0i_ref, pool0_hbm, win_hbm, wout_hbm,
                  *refs, batch, steps, batch_s, start_pos_s):
    g_pre, g_post, a_re, a_im, bm, cmt, d, glu, poolbd, pscale = refs[:N_SEQ_PARAM]
    (h1_hbm, hr_out, hi_out, pool_out, h1s_out, hrs_out, his_out,
     pools_hbm) = refs[N_SEQ_PARAM:N_SEQ_PARAM + 8]
    (slab, hsbf, hr_s, hi_s, hist_s, hrs_s, his_s, hists_s, xs, win_bf, wout_bf, xbuf, hbuf,
     in_sem, out_sem, s_sem, w_sem) = refs[N_SEQ_PARAM + 8:]
    params = (g_pre, g_post, win_bf, a_re, a_im, bm, cmt, d, glu, poolbd, pscale, wout_bf)
    i = pl.program_id(0)
    n_chunks = pl.num_programs(0)
    slot = lax.rem(i, 2)

    def x_copy(b, chunk, sl):
        return pltpu.make_async_copy(
            x_hbm.at[b, pl.ds(chunk * steps, steps), :], xbuf.at[sl, :, b, :],
            in_sem.at[sl])

    def h_copy(chunk):
        return pltpu.make_async_copy(
            hbuf.at[0], h1_hbm.at[pl.ds(chunk * steps, steps)], out_sem.at[0])

    xs_copy = pltpu.make_async_copy(xs_hbm.at[:, 0, :], xs, s_sem.at[0])
    pool_in = pltpu.make_async_copy(pool0_hbm, hists_s.at[pl.ds(1, POOL_HIST - 1)],
                                    s_sem.at[1])
    pool_o = pltpu.make_async_copy(hists_s.at[pl.ds(1, POOL_HIST - 1)], pools_hbm,
                                   s_sem.at[1])

    @pl.when(i == 0)
    def _():
        for b in range(batch):
            x_copy(b, 0, 0).start()
        xs_copy.start()
        pool_in.start()
        hr_s[...] = jnp.zeros_like(hr_s)
        hi_s[...] = jnp.zeros_like(hi_s)
        hist_s[...] = jnp.zeros_like(hist_s)
        _cast_weight(win_hbm, win_bf, hbuf, w_sem, steps)
        _cast_weight(wout_hbm, wout_bf, hbuf, w_sem, steps)

    @pl.when(i + 1 < n_chunks)
    def _():
        for b in range(batch):
            x_copy(b, i + 1, 1 - slot).start()

    for b in range(batch):
        x_copy(b, i, slot).wait()
    x = xbuf[slot].reshape(steps * batch, D_MODEL)
    h1 = _mixer_math(x, i, *params, slab, hsbf, hr_s, hi_s, hist_s,
                     batch=batch, steps=steps, start_pos=0)

    @pl.when(i >= 1)
    def _():
        h_copy(i - 1).wait()

    hbuf[0] = h1.reshape(steps, batch, D_MODEL)
    h_copy(i).start()

    @pl.when(i == n_chunks - 1)
    def _():
        for g in range(S5_GROUPS):
            hr_out[:, g, :] = hr_s[:, g * S5_STATE:(g + 1) * S5_STATE]
            hi_out[:, g, :] = hi_s[:, g * S5_STATE:(g + 1) * S5_STATE]
        pool_out[...] = hist_s[pl.ds(1, POOL_HIST - 1)]

        xs_copy.wait()
        pool_in.wait()
        hists_s[0] = jnp.zeros((batch_s, W_B), F32)
        hrs_s[...] = h0r_ref[...].T
        his_s[...] = h0i_ref[...].T
        h1s_out[...] = _mixer_math(
            xs[...], 0, *params, slab.at[:, pl.ds(0, batch_s), :],
            hsbf.at[:, pl.ds(0, batch_s), :], hrs_s, his_s, hists_s,
            batch=batch_s, steps=1, start_pos=start_pos_s)
        pool_o.start()
        hrs_out[...] = hrs_s[...].T
        his_out[...] = his_s[...].T
        h_copy(i).wait()
        pool_o.wait()


def _mixer(x, xs, h0r_t, h0i_t, pool0_t, w_in, w_out, seq_params, *, steps, start_pos_s):
    batch, seq, _ = x.shape
    batch_s = xs.shape[0]
    assert len(seq_params) == N_SEQ_PARAM and seq % steps == 0 and batch_s <= steps * batch
    f32 = lambda *shape: jax.ShapeDtypeStruct(shape, F32)
    w3 = lambda w: w.reshape(D_MODEL // batch, batch, D_MODEL)
    return pl.pallas_call(
        functools.partial(_mixer_kernel, batch=batch, steps=steps, batch_s=batch_s,
                          start_pos_s=start_pos_s),
        out_shape=(f32(seq, batch, D_MODEL), f32(batch, S5_GROUPS, S5_STATE),
                   f32(batch, S5_GROUPS, S5_STATE), f32(POOL_HIST - 1, batch, W_B),
                   f32(batch_s, D_MODEL), f32(N_STATE, batch_s), f32(N_STATE, batch_s),
                   f32(*pool0_t.shape)),
        grid=(seq // steps,),
        in_specs=[_HBM, _HBM, _VMEM_WHOLE, _VMEM_WHOLE, _HBM, _HBM, _HBM]
        + [_VMEM_WHOLE] * N_SEQ_PARAM,
        out_specs=(_HBM, _full_spec((batch, S5_GROUPS, S5_STATE)),
                   _full_spec((batch, S5_GROUPS, S5_STATE)),
                   _full_spec((POOL_HIST - 1, batch, W_B)),
                   _full_spec((batch_s, D_MODEL)), _full_spec((N_STATE, batch_s)),
                   _full_spec((N_STATE, batch_s)), _HBM),
        scratch_shapes=_mixer_scratch(batch, steps) + [
            pltpu.VMEM((batch_s, N_STATE), F32),
            pltpu.VMEM((batch_s, N_STATE), F32),
            pltpu.VMEM((POOL_HIST, batch_s, W_B), F32),
            pltpu.VMEM((batch_s, D_MODEL), F32),
            pltpu.VMEM((D_MODEL // batch, batch, D_MODEL), BF16),
            pltpu.VMEM((D_MODEL // batch, batch, D_MODEL), BF16),
            pltpu.VMEM((2, steps, batch, D_MODEL), F32),
            pltpu.VMEM((1, steps, batch, D_MODEL), F32),
            pltpu.SemaphoreType.DMA((2,)),
            pltpu.SemaphoreType.DMA((1,)),
            pltpu.SemaphoreType.DMA((2,)),
            pltpu.SemaphoreType.DMA((2,)),
        ],
        compiler_params=_COMPILER_PARAMS,
        name="mixer",
    )(x, xs, h0r_t, h0i_t, pool0_t, w3(w_in), w3(w_out), *seq_params)


MLP_STAGE_ROWS = 128


def _mlp_kernel(h_ref, hs_ref, g_pre_ref, g_post_ref, wup_hbm, wdown_hbm,
                y_hbm, ys_hbm, wup_bf, wdown_bf, stage_up, stage_down, ybuf, ys,
                w_sem, o_sem, y_sem, *, batch, steps):
    i = pl.program_id(0)
    n_chunks = pl.num_programs(0)
    slot = lax.rem(i, 2)

    def y_copy(b, chunk, sl):
        return pltpu.make_async_copy(
            ybuf.at[sl, :, b, :], y_hbm.at[b, pl.ds(chunk * steps, steps), :],
            o_sem.at[sl])

    @pl.when(i == 0)
    def _():
        _cast_weight(wup_hbm, wup_bf, stage_up, w_sem, MLP_STAGE_ROWS)
        _cast_weight(wdown_hbm, wdown_bf, stage_down, w_sem, 4 * MLP_STAGE_ROWS)

    @pl.when(i >= 2)
    def _():
        for b in range(batch):
            y_copy(b, i - 2, slot).wait()

    y = _mlp_math(h_ref[...], g_pre_ref, g_post_ref, wup_bf, wdown_bf)
    ybuf[slot] = y.reshape(steps, batch, D_MODEL)
    for b in range(batch):
        y_copy(b, i, slot).start()

    @pl.when(i == n_chunks - 1)
    def _():
        ys[...] = _mlp_math(hs_ref[...], g_pre_ref, g_post_ref, wup_bf, wdown_bf)
        y_out = pltpu.make_async_copy(ys, ys_hbm.at[:, 0, :], y_sem.at[0])
        y_out.start()

        @pl.when(i >= 1)
        def _():
            for b in range(batch):
                y_copy(b, i - 1, 1 - slot).wait()

        for b in range(batch):
            y_copy(b, i, slot).wait()
        y_out.wait()


def _mlp(h, hs, g_pre, g_post, wup, wdown, *, batch, steps):
    block_rows = steps * batch
    n_rows = h.shape[0]
    batch_s = hs.shape[0]
    assert n_rows % block_rows == 0
    seq = n_rows // batch
    return pl.pallas_call(
        functools.partial(_mlp_kernel, batch=batch, steps=steps),
        out_shape=(jax.ShapeDtypeStruct((batch, seq, D_MODEL), F32),
                   jax.ShapeDtypeStruct((batch_s, 1, D_MODEL), F32)),
        grid=(n_rows // block_rows,),
        in_specs=[
            pl.BlockSpec((block_rows, D_MODEL), lambda i: (i, 0)),
            _VMEM_WHOLE, _VMEM_WHOLE, _VMEM_WHOLE, _HBM, _HBM,
        ],
        out_specs=(_HBM, _HBM),
        scratch_shapes=[
            pltpu.VMEM((D_MODEL, D_FF), BF16),
            pltpu.VMEM((D_FF, D_MODEL), BF16),
            pltpu.VMEM((2, MLP_STAGE_ROWS, D_FF), F32),
            pltpu.VMEM((2, 4 * MLP_STAGE_ROWS, D_MODEL), F32),
            pltpu.VMEM((2, steps, batch, D_MODEL), F32),
            pltpu.VMEM((batch_s, D_MODEL), F32),
            pltpu.SemaphoreType.DMA((2,)),
            pltpu.SemaphoreType.DMA((2,)),
            pltpu.SemaphoreType.DMA((1,)),
        ],
        compiler_params=_COMPILER_PARAMS,
        name="mlp",
    )(h, hs, g_pre, g_post, wup, wdown)


def kernel(x_prompt, x_sample, state_s5_re, state_s5_im, state_pool, norm_mix_pre, norm_mix_post, norm_mlp_pre, norm_mlp_post, w_in, s5_lambda_re, s5_lambda_im, s5_log_dt, s5_b_re, s5_b_im, s5_c_re, s5_c_im, s5_d, s5_w_glu, pool_w, pool_scale, w_out, w_mlp_up, w_mlp_down):
    bp, seq, _ = x_prompt.shape
    bs = x_sample.shape[0]

    a_re, a_im, bm, cmt, glu, poolbd = _s5_prep(
        s5_lambda_re, s5_lambda_im, s5_log_dt, s5_b_re, s5_b_im, s5_c_re, s5_c_im,
        s5_w_glu, pool_w)
    row = lambda v: v.reshape(1, -1)
    seq_params = [row(norm_mix_pre), row(norm_mix_post), a_re, a_im, bm, cmt, row(s5_d),
                  glu, poolbd, row(pool_scale)]

    st_in = lambda a: jnp.transpose(a, (1, 2, 0)).reshape(N_STATE, bs)
    st_out = lambda a: jnp.transpose(a.reshape(S5_GROUPS, S5_STATE, bs), (2, 0, 1))
    tbc = lambda a: jnp.transpose(a, (1, 0, 2))

    h1p, hpr, hpi, pool_p, h1s, hsr, hsi, pool_s = _mixer(
        x_prompt, x_sample, st_in(state_s5_re), st_in(state_s5_im), tbc(state_pool),
        w_in, w_out, seq_params, steps=PROMPT_STEPS, start_pos_s=PAST_LEN)
    yp, y_sample = _mlp(h1p.reshape(seq * bp, D_MODEL), h1s, row(norm_mlp_pre),
                        row(norm_mlp_post), w_mlp_up, w_mlp_down, batch=bp,
                        steps=PROMPT_STEPS)

    return (yp, y_sample, hpr, hpi, tbc(pool_p),
            st_out(hsr), st_out(hsi), tbc(pool_s))
```

```python
import functools
import math

import jax
import jax.numpy as jnp
from jax import lax
from jax.experimental import pallas as pl
from jax.experimental.pallas import tpu as pltpu

F32 = jnp.float32
BF16 = jnp.bfloat16

D_MODEL = 1024
W_A = 512
W_B = 512
S5_H = 16
S5_GROUPS = 32
S5_STATE = 64
N_STATE = S5_GROUPS * S5_STATE
N_SLAB = 4
SLAB_GROUPS = S5_GROUPS // N_SLAB
SLAB_U = SLAB_GROUPS * S5_H
SLAB_S = SLAB_GROUPS * S5_STATE
POOL_WINDOWS = (2, 4, 8, 16)
POOL_CH = 128
POOL_HIST = 16
D_FF = 4096
FF_CHUNK = 1024
EPS = 1e-6
PAST_LEN = 16384
PROMPT_STEPS = 128
GELU_C = math.sqrt(2.0 / math.pi)

VMEM_LIMIT_BYTES = 62 * 1024 * 1024


def _rms_norm(x, g):
    ms = jnp.mean(x * x, axis=-1, keepdims=True)
    return x * lax.rsqrt(ms + EPS) * g


def _full_spec(shape):
    return pl.BlockSpec(shape, lambda *_: (0,) * len(shape))


def _s5_prep_kernel(lam_re_ref, lam_im_ref, log_dt_ref, b_re_ref, b_im_ref,
                    c_re_ref, c_im_ref, wglu_ref, poolw_ref,
                    a_re_ref, a_im_ref, bm_ref, cmt_ref, glu_ref, poolbd_ref):
    lam_re = lam_re_ref[...]
    lam_im = lam_im_ref[...]
    eye = (lax.broadcasted_iota(jnp.int32, (S5_GROUPS, S5_GROUPS), 0)
           == lax.broadcasted_iota(jnp.int32, (S5_GROUPS, S5_GROUPS), 1))
    log_dt = jnp.sum(jnp.where(eye, log_dt_ref[...], 0.0), axis=1, keepdims=True)
    dt = jnp.exp(log_dt)
    mag = jnp.exp(lam_re * dt)
    ang = lam_im * dt
    a_re = mag * jnp.cos(ang)
    a_im = mag * jnp.sin(ang)
    lanes = lambda m: jnp.concatenate([m[g:g + 1, :] for g in range(S5_GROUPS)], axis=1)
    a_re_ref[...] = lanes(a_re)
    a_im_ref[...] = lanes(a_im)
    n_re = a_re - 1.0
    n_im = a_im
    den = lam_re * lam_re + lam_im * lam_im
    k_re = (n_re * lam_re + n_im * lam_im) / den
    k_im = (n_im * lam_re - n_re * lam_im) / den
    per_h = lambda m: jnp.broadcast_to(m[:, None, :], (S5_GROUPS, S5_H, S5_STATE)).reshape(
        S5_GROUPS * S5_H, S5_STATE)
    k_re = per_h(k_re)
    k_im = per_h(k_im)
    b_re = b_re_ref[...]
    b_im = b_im_ref[...]
    bb_re = k_re * b_re - k_im * b_im
    bb_im = k_re * b_im + k_im * b_re
    c_re = c_re_ref[...]
    c_im_neg = -c_im_ref[...]

    rows = lax.broadcasted_iota(jnp.int32, (SLAB_U, SLAB_S), 0) // S5_H
    cols = lax.broadcasted_iota(jnp.int32, (SLAB_U, SLAB_S), 1) // S5_STATE
    diag = rows == cols

    def block_diag(m, j):
        sl = m[j * SLAB_U:(j + 1) * SLAB_U, :]
        tiled = jnp.concatenate([sl] * SLAB_GROUPS, axis=1)
        return jnp.where(diag, tiled, 0.0)

    for j in range(N_SLAB):
        bm_ref[j, :, :SLAB_S] = block_diag(bb_re, j).astype(BF16)
        bm_ref[j, :, SLAB_S:] = block_diag(bb_im, j).astype(BF16)
        cmt_ref[j, :, :SLAB_S] = block_diag(c_re, j).astype(BF16)
        cmt_ref[j, :, SLAB_S:] = block_diag(c_im_neg, j).astype(BF16)

    half = W_A // 2
    g_rows = lax.broadcasted_iota(jnp.int32, (half, half), 0) // S5_H
    g_cols = lax.broadcasted_iota(jnp.int32, (half, half), 1) // S5_H
    for t in range(2):
        blk = wglu_ref[t * half:(t + 1) * half, :]
        tiled = jnp.concatenate([blk] * (half // S5_H), axis=1)
        glu_ref[t] = jnp.where(g_rows == g_cols, tiled, 0.0).astype(BF16)

    zeros = jnp.zeros((POOL_CH, POOL_CH), BF16)
    for t in range(2):
        poolbd_ref[t, :POOL_CH, :POOL_CH] = poolw_ref[2 * t].astype(BF16)
        poolbd_ref[t, :POOL_CH, POOL_CH:] = zeros
        poolbd_ref[t, POOL_CH:, :POOL_CH] = zeros
        poolbd_ref[t, POOL_CH:, POOL_CH:] = poolw_ref[2 * t + 1].astype(BF16)


def _s5_prep(lam_re, lam_im, log_dt, b_re, b_im, c_re, c_im, w_glu, pool_w):
    gh_p = lambda b: jnp.transpose(b, (0, 2, 1)).reshape(S5_GROUPS * S5_H, S5_STATE)
    ins = (lam_re, lam_im, log_dt.reshape(1, S5_GROUPS), gh_p(b_re), gh_p(b_im),
           c_re.reshape(S5_GROUPS * S5_H, S5_STATE),
           c_im.reshape(S5_GROUPS * S5_H, S5_STATE),
           w_glu.reshape(S5_GROUPS * S5_H, S5_H), pool_w)
    out_shape = (
        jax.ShapeDtypeStruct((1, N_STATE), F32),
        jax.ShapeDtypeStruct((1, N_STATE), F32),
        jax.ShapeDtypeStruct((N_SLAB, SLAB_U, 2 * SLAB_S), BF16),
        jax.ShapeDtypeStruct((N_SLAB, SLAB_U, 2 * SLAB_S), BF16),
        jax.ShapeDtypeStruct((2, W_A // 2, W_A // 2), BF16),
        jax.ShapeDtypeStruct((2, 2 * POOL_CH, 2 * POOL_CH), BF16),
    )
    return pl.pallas_call(
        _s5_prep_kernel,
        out_shape=out_shape,
        name="s5_prep",
    )(*ins)


def _mixer_math(x, i, g_pre_ref, g_post_ref, win_ref, a_re_ref, a_im_ref, bm_ref,
                cmt_ref, d_ref, glu_ref, poolw_ref, pscale_ref, wout_ref,
                slab_ref, hsbf_ref, hr_s, hi_s, hist_s, *, batch, steps, start_pos,
                before_output_projection=None):
    rows = batch * steps
    xn = _rms_norm(x, g_pre_ref[...]).astype(BF16)
    w_in = win_ref[...].reshape(D_MODEL, D_MODEL)
    ua = jnp.dot(xn, w_in[:, :W_A], preferred_element_type=F32)
    ua_bf = ua.astype(BF16)

    pair = max(1, 16 // batch)

    def project_in(j):
        slab_ref[j % 2] = jnp.dot(ua_bf[:, j * SLAB_U:(j + 1) * SLAB_U], bm_ref[j],
                                  preferred_element_type=F32)

    def scan(j):
        st = slice(j * SLAB_S, (j + 1) * SLAB_S)
        sb = slab_ref.at[j % 2]
        hb = hsbf_ref.at[j % 2]
        ar = jnp.broadcast_to(a_re_ref[:, st], (batch, SLAB_S))
        ai = jnp.broadcast_to(a_im_ref[:, st], (batch, SLAB_S))
        hr = hr_s[:, st]
        hi = hi_s[:, st]
        for t0 in range(0, steps, pair):
            res, ims = [], []
            for t in range(t0, min(t0 + pair, steps)):
                rt = slice(t * batch, (t + 1) * batch)
                nr = ar * hr - ai * hi + sb[rt, :SLAB_S]
                ni = ar * hi + ai * hr + sb[rt, SLAB_S:]
                res.append(nr)
                ims.append(ni)
                hr, hi = nr, ni
            rg = slice(t0 * batch, (t0 + len(res)) * batch)
            hb[rg, :SLAB_S] = jnp.concatenate(res, axis=0).astype(BF16)
            hb[rg, SLAB_S:] = jnp.concatenate(ims, axis=0).astype(BF16)
        hr_s[:, st] = hr
        hi_s[:, st] = hi

    def project_out(j):
        return lax.dot_general(hsbf_ref[j % 2], cmt_ref[j], (((1,), (1,)), ((), ())),
                               preferred_element_type=F32)

    ys = []
    project_in(0)
    scan(0)
    ub = jnp.dot(xn, w_in[:, W_A:], preferred_element_type=F32)
    for j in range(1, N_SLAB):
        project_in(j)
        ys.append(project_out(j - 1))
        scan(j)
    ys.append(project_out(N_SLAB - 1))

    y = jnp.concatenate(ys, axis=1) + d_ref[...] * ua
    y = y * (0.5 * (1.0 + jnp.tanh(GELU_C * (y + 0.044715 * (y * y * y)))))
    y_bf = y.astype(BF16)
    half = W_A // 2
    gate = jnp.concatenate(
        [jnp.dot(y_bf[:, :half], glu_ref[0], preferred_element_type=F32),
         jnp.dot(y_bf[:, half:], glu_ref[1], preferred_element_type=F32)], axis=1)
    ya = y * (1.0 / (1.0 + jnp.exp(-gate)))

    hist = hist_s[...].reshape(POOL_HIST * batch, W_B)
    ext = jnp.concatenate([hist, ub], axis=0)
    n_ext = POOL_HIST * batch + rows
    hist_s[...] = ext[n_ext - POOL_HIST * batch:, :].reshape(POOL_HIST, batch, W_B)
    t_loc = lax.broadcasted_iota(jnp.int32, (rows, 1), 0) // batch
    pos1 = t_loc + (start_pos + 1) + i * steps
    pooled = []
    for gi, w in enumerate(POOL_WINDOWS):
        s = ext[:, gi * POOL_CH:(gi + 1) * POOL_CH]
        span = 1
        while span < w:
            n = s.shape[0]
            s = s[span * batch:, :] + s[:n - span * batch, :]
            span *= 2
        win = s[s.shape[0] - rows:, :]
        count = jnp.minimum(pos1, w).astype(F32)
        pooled.append(win / count - ub[:, gi * POOL_CH:(gi + 1) * POOL_CH])
    pooled = jnp.concatenate(pooled, axis=1).astype(BF16)
    halfb = W_B // 2
    yb = jnp.concatenate(
        [jnp.dot(pooled[:, :halfb], poolw_ref[0], preferred_element_type=F32),
         jnp.dot(pooled[:, halfb:], poolw_ref[1], preferred_element_type=F32)], axis=1)
    yb = yb * pscale_ref[...]

    ycat = jnp.concatenate([ya, yb], axis=1).astype(BF16)
    if before_output_projection is not None:
        before_output_projection()
    mix = jnp.dot(ycat, wout_ref[...].reshape(D_MODEL, D_MODEL),
                  preferred_element_type=F32)
    return x + _rms_norm(mix, g_post_ref[...])


def _mlp_math(h, g_pre_ref, g_post_ref, wup_ref, wdown_ref):
    hn = _rms_norm(h, g_pre_ref[...]).astype(BF16)
    acc = None
    for j in range(D_FF // FF_CHUNK):
        sl = slice(j * FF_CHUNK, (j + 1) * FF_CHUNK)
        up = jnp.dot(hn, wup_ref[:, sl], preferred_element_type=F32)
        up = jnp.maximum(up, 0.0)
        ff = (up * up).astype(BF16)
        part = jnp.dot(ff, wdown_ref[sl, :], preferred_element_type=F32)
        acc = part if acc is None else acc + part
    return h + _rms_norm(acc, g_post_ref[...])


def _mixer_scratch(batch, steps):
    return [
        pltpu.VMEM((2, batch * steps, 2 * SLAB_S), F32),
        pltpu.VMEM((2, batch * steps, 2 * SLAB_S), BF16),
        pltpu.VMEM((batch, N_STATE), F32),
        pltpu.VMEM((batch, N_STATE), F32),
        pltpu.VMEM((POOL_HIST, batch, W_B), F32),
    ]


_VMEM_WHOLE = pl.BlockSpec(memory_space=pltpu.VMEM)
_HBM = pl.BlockSpec(memory_space=pl.ANY)
_COMPILER_PARAMS = pltpu.CompilerParams(dimension_semantics=("arbitrary",),
                                        vmem_limit_bytes=VMEM_LIMIT_BYTES)


def _cast_weight(w_hbm, dst_ref, stage_ref, sem, row_chunk):
    n = w_hbm.shape[0] // row_chunk

    def copy(c):
        return pltpu.make_async_copy(w_hbm.at[pl.ds(c * row_chunk, row_chunk)],
                                     stage_ref.at[c % 2], sem.at[c % 2])

    copy(0).start()
    for c in range(n):
        if c + 1 < n:
            copy(c + 1).start()
        copy(c).wait()
        dst_ref[pl.ds(c * row_chunk, row_chunk)] = stage_ref[c % 2].astype(BF16)


N_SEQ_PARAM = 10


def _mixer_kernel(x_hbm, xs_hbm, h0r_ref, h0i_ref, pool0_hbm, win_hbm, wout_hbm,
                  *refs, batch, steps, batch_s, start_pos_s):
    g_pre, g_post, a_re, a_im, bm, cmt, d, glu, poolbd, pscale = refs[:N_SEQ_PARAM]
    (h1_hbm, hr_out, hi_out, pool_out, h1s_out, hrs_out, his_out,
     pools_hbm) = refs[N_SEQ_PARAM:N_SEQ_PARAM + 8]
    (slab, hsbf, hr_s, hi_s, hist_s, hrs_s, his_s, hists_s, xs, win_bf, wout_bf, xbuf, hbuf,
     in_sem, out_sem, s_sem, w_sem) = refs[N_SEQ_PARAM + 8:]
    params = (g_pre, g_post, win_bf, a_re, a_im, bm, cmt, d, glu, poolbd, pscale, wout_bf)
    i = pl.program_id(0)
    n_chunks = pl.num_programs(0)
    slot = lax.rem(i, 2)

    def x_copy(b, chunk, sl):
        return pltpu.make_async_copy(
            x_hbm.at[b, pl.ds(chunk * steps, steps), :], xbuf.at[sl, :, b, :],
            in_sem.at[sl])

    def h_copy(chunk):
        return pltpu.make_async_copy(
            hbuf.at[0], h1_hbm.at[pl.ds(chunk * steps, steps)], out_sem.at[0])

    xs_copy = pltpu.make_async_copy(xs_hbm.at[:, 0, :], xs, s_sem.at[0])
    pool_in = pltpu.make_async_copy(pool0_hbm, hists_s.at[pl.ds(1, POOL_HIST - 1)],
                                    s_sem.at[1])
    pool_o = pltpu.make_async_copy(hists_s.at[pl.ds(1, POOL_HIST - 1)], pools_hbm,
                                   s_sem.at[1])

    @pl.when(i == 0)
    def _():
        for b in range(batch):
            x_copy(b, 0, 0).start()
        xs_copy.start()
        pool_in.start()
        hr_s[...] = jnp.zeros_like(hr_s)
        hi_s[...] = jnp.zeros_like(hi_s)
        hist_s[...] = jnp.zeros_like(hist_s)
        _cast_weight(win_hbm, win_bf, hbuf, w_sem, steps)
        _cast_weight(wout_hbm, wout_bf, hbuf, w_sem, steps)

    @pl.when(i + 1 < n_chunks)
    def _():
        for b in range(batch):
            x_copy(b, i + 1, 1 - slot).start()

    for b in range(batch):
        x_copy(b, i, slot).wait()
    x = xbuf[slot].reshape(steps * batch, D_MODEL)
    def wait_previous_output():
        @pl.when(i >= 1)
        def _():
            h_copy(i - 1).wait()

    h1 = _mixer_math(x, i, *params, slab, hsbf, hr_s, hi_s, hist_s,
                     batch=batch, steps=steps, start_pos=0,
                     before_output_projection=wait_previous_output)
    hbuf[0] = h1.reshape(steps, batch, D_MODEL)
    h_copy(i).start()

    @pl.when(i == n_chunks - 1)
    def _():
        for g in range(S5_GROUPS):
            hr_out[:, g, :] = hr_s[:, g * S5_STATE:(g + 1) * S5_STATE]
            hi_out[:, g, :] = hi_s[:, g * S5_STATE:(g + 1) * S5_STATE]
        pool_out[...] = hist_s[pl.ds(1, POOL_HIST - 1)]

        xs_copy.wait()
        pool_in.wait()
        hists_s[0] = jnp.zeros((batch_s, W_B), F32)
        hrs_s[...] = h0r_ref[...].T
        his_s[...] = h0i_ref[...].T
        h1s_out[...] = _mixer_math(
            xs[...], 0, *params, slab.at[:, pl.ds(0, batch_s), :],
            hsbf.at[:, pl.ds(0, batch_s), :], hrs_s, his_s, hists_s,
            batch=batch_s, steps=1, start_pos=start_pos_s)
        pool_o.start()
        hrs_out[...] = hrs_s[...].T
        his_out[...] = his_s[...].T
        h_copy(i).wait()
        pool_o.wait()


def _mixer(x, xs, h0r_t, h0i_t, pool0_t, w_in, w_out, seq_params, *, steps, start_pos_s):
    batch, seq, _ = x.shape
    batch_s = xs.shape[0]
    assert len(seq_params) == N_SEQ_PARAM and seq % steps == 0 and batch_s <= steps * batch
    f32 = lambda *shape: jax.ShapeDtypeStruct(shape, F32)
    w3 = lambda w: w.reshape(D_MODEL // batch, batch, D_MODEL)
    return pl.pallas_call(
        functools.partial(_mixer_kernel, batch=batch, steps=steps, batch_s=batch_s,
                          start_pos_s=start_pos_s),
        out_shape=(f32(seq, batch, D_MODEL), f32(batch, S5_GROUPS, S5_STATE),
                   f32(batch, S5_GROUPS, S5_STATE), f32(POOL_HIST - 1, batch, W_B),
                   f32(batch_s, D_MODEL), f32(N_STATE, batch_s), f32(N_STATE, batch_s),
                   f32(*pool0_t.shape)),
        grid=(seq // steps,),
        in_specs=[_HBM, _HBM, _VMEM_WHOLE, _VMEM_WHOLE, _HBM, _HBM, _HBM]
        + [_VMEM_WHOLE] * N_SEQ_PARAM,
        out_specs=(_HBM, _full_spec((batch, S5_GROUPS, S5_STATE)),
                   _full_spec((batch, S5_GROUPS, S5_STATE)),
                   _full_spec((POOL_HIST - 1, batch, W_B)),
                   _full_spec((batch_s, D_MODEL)), _full_spec((N_STATE, batch_s)),
                   _full_spec((N_STATE, batch_s)), _HBM),
        scratch_shapes=_mixer_scratch(batch, steps) + [
            pltpu.VMEM((batch_s, N_STATE), F32),
            pltpu.VMEM((batch_s, N_STATE), F32),
            pltpu.VMEM((POOL_HIST, batch_s, W_B), F32),
            pltpu.VMEM((batch_s, D_MODEL), F32),
            pltpu.VMEM((D_MODEL // batch, batch, D_MODEL), BF16),
            pltpu.VMEM((D_MODEL // batch, batch, D_MODEL), BF16),
            pltpu.VMEM((2, steps, batch, D_MODEL), F32),
            pltpu.VMEM((1, steps, batch, D_MODEL), F32),
            pltpu.SemaphoreType.DMA((2,)),
            pltpu.SemaphoreType.DMA((1,)),
            pltpu.SemaphoreType.DMA((2,)),
            pltpu.SemaphoreType.DMA((2,)),
        ],
        compiler_params=_COMPILER_PARAMS,
        name="mixer",
    )(x, xs, h0r_t, h0i_t, pool0_t, w3(w_in), w3(w_out), *seq_params)


MLP_STAGE_ROWS = 128


def _mlp_kernel(h_ref, hs_ref, g_pre_ref, g_post_ref, wup_hbm, wdown_hbm,
                y_hbm, ys_hbm, wup_bf, wdown_bf, stage_up, stage_down, ybuf, ys,
                w_sem, o_sem, y_sem, *, batch, steps):
    i = pl.program_id(0)
    n_chunks = pl.num_programs(0)
    slot = lax.rem(i, 2)

    def y_copy(b, chunk, sl):
        return pltpu.make_async_copy(
            ybuf.at[sl, :, b, :], y_hbm.at[b, pl.ds(chunk * steps, steps), :],
            o_sem.at[sl])

    @pl.when(i == 0)
    def _():
        _cast_weight(wup_hbm, wup_bf, stage_up, w_sem, MLP_STAGE_ROWS)
        _cast_weight(wdown_hbm, wdown_bf, stage_down, w_sem, 4 * MLP_STAGE_ROWS)

    @pl.when(i >= 2)
    def _():
        for b in range(batch):
            y_copy(b, i - 2, slot).wait()

    y = _mlp_math(h_ref[...], g_pre_ref, g_post_ref, wup_bf, wdown_bf)
    ybuf[slot] = y.reshape(steps, batch, D_MODEL)
    for b in range(batch):
        y_copy(b, i, slot).start()

    @pl.when(i == n_chunks - 1)
    def _():
        ys[...] = _mlp_math(hs_ref[...], g_pre_ref, g_post_ref, wup_bf, wdown_bf)
        y_out = pltpu.make_async_copy(ys, ys_hbm.at[:, 0, :], y_sem.at[0])
        y_out.start()

        @pl.when(i >= 1)
        def _():
            for b in range(batch):
                y_copy(b, i - 1, 1 - slot).wait()

        for b in range(batch):
            y_copy(b, i, slot).wait()
        y_out.wait()


def _mlp(h, hs, g_pre, g_post, wup, wdown, *, batch, steps):
    block_rows = steps * batch
    n_rows = h.shape[0]
    batch_s = hs.shape[0]
    assert n_rows % block_rows == 0
    seq = n_rows // batch
    return pl.pallas_call(
        functools.partial(_mlp_kernel, batch=batch, steps=steps),
        out_shape=(jax.ShapeDtypeStruct((batch, seq, D_MODEL), F32),
                   jax.ShapeDtypeStruct((batch_s, 1, D_MODEL), F32)),
        grid=(n_rows // block_rows,),
        in_specs=[
            pl.BlockSpec((block_rows, D_MODEL), lambda i: (i, 0)),
            _VMEM_WHOLE, _VMEM_WHOLE, _VMEM_WHOLE, _HBM, _HBM,
        ],
        out_specs=(_HBM, _HBM),
        scratch_shapes=[
            pltpu.VMEM((D_MODEL, D_FF), BF16),
            pltpu.VMEM((D_FF, D_MODEL), BF16),
            pltpu.VMEM((2, MLP_STAGE_ROWS, D_FF), F32),
            pltpu.VMEM((2, 4 * MLP_STAGE_ROWS, D_MODEL), F32),
            pltpu.VMEM((2, steps, batch, D_MODEL), F32),
            pltpu.VMEM((batch_s, D_MODEL), F32),
            pltpu.SemaphoreType.DMA((2,)),
            pltpu.SemaphoreType.DMA((2,)),
            pltpu.SemaphoreType.DMA((1,)),
        ],
        compiler_params=_COMPILER_PARAMS,
        name="mlp",
    )(h, hs, g_pre, g_post, wup, wdown)


def kernel(x_prompt, x_sample, state_s5_re, state_s5_im, state_pool, norm_mix_pre, norm_mix_post, norm_mlp_pre, norm_mlp_post, w_in, s5_lambda_re, s5_lambda_im, s5_log_dt, s5_b_re, s5_b_im, s5_c_re, s5_c_im, s5_d, s5_w_glu, pool_w, pool_scale, w_out, w_mlp_up, w_mlp_down):
    bp, seq, _ = x_prompt.shape
    bs = x_sample.shape[0]

    a_re, a_im, bm, cmt, glu, poolbd = _s5_prep(
        s5_lambda_re, s5_lambda_im, s5_log_dt, s5_b_re, s5_b_im, s5_c_re, s5_c_im,
        s5_w_glu, pool_w)
    row = lambda v: v.reshape(1, -1)
    seq_params = [row(norm_mix_pre), row(norm_mix_post), a_re, a_im, bm, cmt, row(s5_d),
                  glu, poolbd, row(pool_scale)]

    st_in = lambda a: jnp.transpose(a, (1, 2, 0)).reshape(N_STATE, bs)
    st_out = lambda a: jnp.transpose(a.reshape(S5_GROUPS, S5_STATE, bs), (2, 0, 1))
    tbc = lambda a: jnp.transpose(a, (1, 0, 2))

    h1p, hpr, hpi, pool_p, h1s, hsr, hsi, pool_s = _mixer(
        x_prompt, x_sample, st_in(state_s5_re), st_in(state_s5_im), tbc(state_pool),
        w_in, w_out, seq_params, steps=PROMPT_STEPS, start_pos_s=PAST_LEN)
    yp, y_sample = _mlp(h1p.reshape(seq * bp, D_MODEL), h1s, row(norm_mlp_pre),
                        row(norm_mlp_post), w_mlp_up, w_mlp_down, batch=bp,
                        steps=PROMPT_STEPS)

    return (yp, y_sample, hpr, hpi, tbc(pool_p),
            st_out(hsr), st_out(hsi), tbc(pool_s))
```

```python
import functools
import math

import jax
import jax.numpy as jnp
from jax import lax
from jax.experimental import pallas as pl
from jax.experimental.pallas import tpu as pltpu

F32 = jnp.float32
BF16 = jnp.bfloat16

D_MODEL = 1024
W_A = 512
W_B = 512
S5_H = 16
S5_GROUPS = 32
S5_STATE = 64
N_STATE = S5_GROUPS * S5_STATE
N_SLAB = 4
SLAB_GROUPS = S5_GROUPS // N_SLAB
SLAB_U = SLAB_GROUPS * S5_H
SLAB_S = SLAB_GROUPS * S5_STATE
POOL_WINDOWS = (2, 4, 8, 16)
POOL_CH = 128
POOL_HIST = 16
D_FF = 4096
FF_CHUNK = 1024
EPS = 1e-6
PAST_LEN = 16384
PROMPT_STEPS = 128
GELU_C = math.sqrt(2.0 / math.pi)

VMEM_LIMIT_BYTES = 58 * 1024 * 1024


def _rms_norm(x, g):
    ms = jnp.mean(x * x, axis=-1, keepdims=True)
    return x * lax.rsqrt(ms + EPS) * g


def _full_spec(shape):
    return pl.BlockSpec(shape, lambda *_: (0,) * len(shape))


def _s5_prep_kernel(lam_re_ref, lam_im_ref, log_dt_ref, b_re_ref, b_im_ref,
                    c_re_ref, c_im_ref, wglu_ref, poolw_ref,
                    a_re_ref, a_im_ref, bm_ref, cm_ref, glu_ref, poolbd_ref):
    lam_re = lam_re_ref[...]
    lam_im = lam_im_ref[...]
    eye = (lax.broadcasted_iota(jnp.int32, (S5_GROUPS, S5_GROUPS), 0)
           == lax.broadcasted_iota(jnp.int32, (S5_GROUPS, S5_GROUPS), 1))
    log_dt = jnp.sum(jnp.where(eye, log_dt_ref[...], 0.0), axis=1, keepdims=True)
    dt = jnp.exp(log_dt)
    mag = jnp.exp(lam_re * dt)
    ang = lam_im * dt
    a_re = mag * jnp.cos(ang)
    a_im = mag * jnp.sin(ang)
    lanes = lambda m: jnp.concatenate([m[g:g + 1, :] for g in range(S5_GROUPS)], axis=1)
    a_re_ref[...] = lanes(a_re)
    a_im_ref[...] = lanes(a_im)
    n_re = a_re - 1.0
    n_im = a_im
    den = lam_re * lam_re + lam_im * lam_im
    k_re = (n_re * lam_re + n_im * lam_im) / den
    k_im = (n_im * lam_re - n_re * lam_im) / den
    per_h = lambda m: jnp.broadcast_to(m[:, None, :], (S5_GROUPS, S5_H, S5_STATE)).reshape(
        S5_GROUPS * S5_H, S5_STATE)
    k_re = per_h(k_re)
    k_im = per_h(k_im)
    b_re = b_re_ref[...]
    b_im = b_im_ref[...]
    bb_re = k_re * b_re - k_im * b_im
    bb_im = k_re * b_im + k_im * b_re
    c_re = c_re_ref[...]
    c_im_neg = -c_im_ref[...]

    rows = lax.broadcasted_iota(jnp.int32, (SLAB_U, SLAB_S), 0) // S5_H
    cols = lax.broadcasted_iota(jnp.int32, (SLAB_U, SLAB_S), 1) // S5_STATE
    diag = rows == cols

    def block_diag(m, j):
        sl = m[j * SLAB_U:(j + 1) * SLAB_U, :]
        tiled = jnp.concatenate([sl] * SLAB_GROUPS, axis=1)
        return jnp.where(diag, tiled, 0.0)

    for j in range(N_SLAB):
        bm_ref[j, :, :SLAB_S] = block_diag(bb_re, j).astype(BF16)
        bm_ref[j, :, SLAB_S:] = block_diag(bb_im, j).astype(BF16)
        cm_ref[j, :SLAB_S, :] = block_diag(c_re, j).T.astype(BF16)
        cm_ref[j, SLAB_S:, :] = block_diag(c_im_neg, j).T.astype(BF16)

    half = W_A // 2
    g_rows = lax.broadcasted_iota(jnp.int32, (half, half), 0) // S5_H
    g_cols = lax.broadcasted_iota(jnp.int32, (half, half), 1) // S5_H
    for t in range(2):
        blk = wglu_ref[t * half:(t + 1) * half, :]
        tiled = jnp.concatenate([blk] * (half // S5_H), axis=1)
        glu_ref[t] = jnp.where(g_rows == g_cols, tiled, 0.0).astype(BF16)

    zeros = jnp.zeros((POOL_CH, POOL_CH), BF16)
    for t in range(2):
        poolbd_ref[t, :POOL_CH, :POOL_CH] = poolw_ref[2 * t].astype(BF16)
        poolbd_ref[t, :POOL_CH, POOL_CH:] = zeros
        poolbd_ref[t, POOL_CH:, :POOL_CH] = zeros
        poolbd_ref[t, POOL_CH:, POOL_CH:] = poolw_ref[2 * t + 1].astype(BF16)


def _s5_prep(lam_re, lam_im, log_dt, b_re, b_im, c_re, c_im, w_glu, pool_w):
    gh_p = lambda b: jnp.transpose(b, (0, 2, 1)).reshape(S5_GROUPS * S5_H, S5_STATE)
    ins = (lam_re, lam_im, log_dt.reshape(1, S5_GROUPS), gh_p(b_re), gh_p(b_im),
           c_re.reshape(S5_GROUPS * S5_H, S5_STATE),
           c_im.reshape(S5_GROUPS * S5_H, S5_STATE),
           w_glu.reshape(S5_GROUPS * S5_H, S5_H), pool_w)
    out_shape = (
        jax.ShapeDtypeStruct((1, N_STATE), F32),
        jax.ShapeDtypeStruct((1, N_STATE), F32),
        jax.ShapeDtypeStruct((N_SLAB, SLAB_U, 2 * SLAB_S), BF16),
        jax.ShapeDtypeStruct((N_SLAB, 2 * SLAB_S, SLAB_U), BF16),
        jax.ShapeDtypeStruct((2, W_A // 2, W_A // 2), BF16),
        jax.ShapeDtypeStruct((2, 2 * POOL_CH, 2 * POOL_CH), BF16),
    )
    return pl.pallas_call(
        _s5_prep_kernel,
        out_shape=out_shape,
        name="s5_prep",
    )(*ins)


def _mixer_math(x, i, g_pre_ref, g_post_ref, win_ref, a_re_ref, a_im_ref, bm_ref,
                cm_ref, d_ref, glu_ref, poolw_ref, pscale_ref, wout_ref,
                slab_ref, hsbf_ref, hr_s, hi_s, hist_s, *, batch, steps, start_pos):
    rows = batch * steps
    xn = _rms_norm(x, g_pre_ref[...]).astype(BF16)
    w_in = win_ref[...].reshape(D_MODEL, D_MODEL)
    ua = jnp.dot(xn, w_in[:, :W_A], preferred_element_type=F32)
    ua_bf = ua.astype(BF16)

    pair = max(1, 16 // batch)

    def project_in(j):
        slab_ref[j % 2] = jnp.dot(ua_bf[:, j * SLAB_U:(j + 1) * SLAB_U], bm_ref[j],
                                  preferred_element_type=F32)

    def scan(j):
        st = slice(j * SLAB_S, (j + 1) * SLAB_S)
        sb = slab_ref.at[j % 2]
        hb = hsbf_ref.at[j % 2]
        ar = jnp.broadcast_to(a_re_ref[:, st], (batch, SLAB_S))
        ai = jnp.broadcast_to(a_im_ref[:, st], (batch, SLAB_S))
        hr = hr_s[:, st]
        hi = hi_s[:, st]
        for t0 in range(0, steps, pair):
            res, ims = [], []
            for t in range(t0, min(t0 + pair, steps)):
                rt = slice(t * batch, (t + 1) * batch)
                nr = ar * hr - ai * hi + sb[rt, :SLAB_S]
                ni = ar * hi + ai * hr + sb[rt, SLAB_S:]
                res.append(nr)
                ims.append(ni)
                hr, hi = nr, ni
            rg = slice(t0 * batch, (t0 + len(res)) * batch)
            hb[rg, :SLAB_S] = jnp.concatenate(res, axis=0).astype(BF16)
            hb[rg, SLAB_S:] = jnp.concatenate(ims, axis=0).astype(BF16)
        hr_s[:, st] = hr
        hi_s[:, st] = hi

    def project_out(j):
        return jnp.dot(hsbf_ref[j % 2], cm_ref[j],
                       preferred_element_type=F32)

    ys = []
    project_in(0)
    scan(0)
    ub = jnp.dot(xn, w_in[:, W_A:], preferred_element_type=F32)
    for j in range(1, N_SLAB):
        project_in(j)
        ys.append(project_out(j - 1))
        scan(j)
    ys.append(project_out(N_SLAB - 1))

    y = jnp.concatenate(ys, axis=1) + d_ref[...] * ua
    y = y * (0.5 * (1.0 + jnp.tanh(GELU_C * (y + 0.044715 * (y * y * y)))))
    y_bf = y.astype(BF16)
    half = W_A // 2
    gate = jnp.concatenate(
        [jnp.dot(y_bf[:, :half], glu_ref[0], preferred_element_type=F32),
         jnp.dot(y_bf[:, half:], glu_ref[1], preferred_element_type=F32)], axis=1)
    ya = y * (1.0 / (1.0 + jnp.exp(-gate)))

    hist = hist_s[...].reshape(POOL_HIST * batch, W_B)
    ext = jnp.concatenate([hist, ub], axis=0)
    n_ext = POOL_HIST * batch + rows
    hist_s[...] = ext[n_ext - POOL_HIST * batch:, :].reshape(POOL_HIST, batch, W_B)
    t_loc = lax.broadcasted_iota(jnp.int32, (rows, 1), 0) // batch
    pos1 = t_loc + (start_pos + 1) + i * steps
    pooled = []
    for gi, w in enumerate(POOL_WINDOWS):
        s = ext[:, gi * POOL_CH:(gi + 1) * POOL_CH]
        span = 1
        while span < w:
            n = s.shape[0]
            s = s[span * batch:, :] + s[:n - span * batch, :]
            span *= 2
        win = s[s.shape[0] - rows:, :]
        count = jnp.minimum(pos1, w).astype(F32)
        pooled.append(win / count - ub[:, gi * POOL_CH:(gi + 1) * POOL_CH])
    pooled = jnp.concatenate(pooled, axis=1).astype(BF16)
    halfb = W_B // 2
    yb = jnp.concatenate(
        [jnp.dot(pooled[:, :halfb], poolw_ref[0], preferred_element_type=F32),
         jnp.dot(pooled[:, halfb:], poolw_ref[1], preferred_element_type=F32)], axis=1)
    yb = yb * pscale_ref[...]

    ycat = jnp.concatenate([ya, yb], axis=1).astype(BF16)
    mix = jnp.dot(ycat, wout_ref[...].reshape(D_MODEL, D_MODEL),
                  preferred_element_type=F32)
    return x + _rms_norm(mix, g_post_ref[...])


def _mlp_math(h, g_pre_ref, g_post_ref, wup_ref, wdown_ref):
    hn = _rms_norm(h, g_pre_ref[...]).astype(BF16)
    acc = None
    for j in range(D_FF // FF_CHUNK):
        sl = slice(j * FF_CHUNK, (j + 1) * FF_CHUNK)
        up = jnp.dot(hn, wup_ref[:, sl], preferred_element_type=F32)
        up = jnp.maximum(up, 0.0)
        ff = (up * up).astype(BF16)
        part = jnp.dot(ff, wdown_ref[sl, :], preferred_element_type=F32)
        acc = part if acc is None else acc + part
    return h + _rms_norm(acc, g_post_ref[...])


def _mixer_scratch(batch, steps):
    return [
        pltpu.VMEM((2, batch * steps, 2 * SLAB_S), F32),
        pltpu.VMEM((2, batch * steps, 2 * SLAB_S), BF16),
        pltpu.VMEM((batch, N_STATE), F32),
        pltpu.VMEM((batch, N_STATE), F32),
        pltpu.VMEM((POOL_HIST, batch, W_B), F32),
    ]


_VMEM_WHOLE = pl.BlockSpec(memory_space=pltpu.VMEM)
_HBM = pl.BlockSpec(memory_space=pl.ANY)
_COMPILER_PARAMS = pltpu.CompilerParams(dimension_semantics=("arbitrary",),
                                        vmem_limit_bytes=VMEM_LIMIT_BYTES)


def _cast_weight(w_hbm, dst_ref, stage_ref, sem, row_chunk):
    n = w_hbm.shape[0] // row_chunk

    def copy(c):
        return pltpu.make_async_copy(w_hbm.at[pl.ds(c * row_chunk, row_chunk)],
                                     stage_ref.at[c % 2], sem.at[c % 2])

    copy(0).start()
    for c in range(n):
        if c + 1 < n:
            copy(c + 1).start()
        copy(c).wait()
        dst_ref[pl.ds(c * row_chunk, row_chunk)] = stage_ref[c % 2].astype(BF16)


N_SEQ_PARAM = 10


def _mixer_kernel(x_hbm, xs_hbm, h0r_ref, h0i_ref, pool0_hbm, win_hbm, wout_hbm,
                  *refs, batch, steps, batch_s, start_pos_s):
    g_pre, g_post, a_re, a_im, bm, cm, d, glu, poolbd, pscale = refs[:N_SEQ_PARAM]
    (h1_hbm, hr_out, hi_out, pool_out, h1s_out, hrs_out, his_out,
     pools_hbm) = refs[N_SEQ_PARAM:N_SEQ_PARAM + 8]
    (slab, hsbf, hr_s, hi_s, hist_s, hrs_s, his_s, hists_s, xs, win_bf, wout_bf, xbuf, hbuf,
     in_sem, out_sem, s_sem, w_sem) = refs[N_SEQ_PARAM + 8:]
    params = (g_pre, g_post, win_bf, a_re, a_im, bm, cm, d, glu, poolbd, pscale, wout_bf)
    i = pl.program_id(0)
    n_chunks = pl.num_programs(0)
    slot = lax.rem(i, 2)

    def x_copy(b, chunk, sl):
        return pltpu.make_async_copy(
            x_hbm.at[b, pl.ds(chunk * steps, steps), :], xbuf.at[sl, :, b, :],
            in_sem.at[sl])

    def h_copy(chunk):
        return pltpu.make_async_copy(
            hbuf.at[0], h1_hbm.at[pl.ds(chunk * steps, steps)], out_sem.at[0])

    xs_copy = pltpu.make_async_copy(xs_hbm.at[:, 0, :], xs, s_sem.at[0])
    pool_in = pltpu.make_async_copy(pool0_hbm, hists_s.at[pl.ds(1, POOL_HIST - 1)],
                                    s_sem.at[1])
    pool_o = pltpu.make_async_copy(hists_s.at[pl.ds(1, POOL_HIST - 1)], pools_hbm,
                                   s_sem.at[1])

    @pl.when(i == 0)
    def _():
        for b in range(batch):
            x_copy(b, 0, 0).start()
        xs_copy.start()
        pool_in.start()
        hr_s[...] = jnp.zeros_like(hr_s)
        hi_s[...] = jnp.zeros_like(hi_s)
        hist_s[...] = jnp.zeros_like(hist_s)
        _cast_weight(win_hbm, win_bf, hbuf, w_sem, steps)
        _cast_weight(wout_hbm, wout_bf, hbuf, w_sem, steps)

    @pl.when(i + 1 < n_chunks)
    def _():
        for b in range(batch):
            x_copy(b, i + 1, 1 - slot).start()

    for b in range(batch):
        x_copy(b, i, slot).wait()
    x = xbuf[slot].reshape(steps * batch, D_MODEL)
    h1 = _mixer_math(x, i, *params, slab, hsbf, hr_s, hi_s, hist_s,
                     batch=batch, steps=steps, start_pos=0)

    @pl.when(i >= 1)
    def _():
        h_copy(i - 1).wait()

    hbuf[0] = h1.reshape(steps, batch, D_MODEL)
    h_copy(i).start()

    @pl.when(i == n_chunks - 1)
    def _():
        for g in range(S5_GROUPS):
            hr_out[:, g, :] = hr_s[:, g * S5_STATE:(g + 1) * S5_STATE]
            hi_out[:, g, :] = hi_s[:, g * S5_STATE:(g + 1) * S5_STATE]
        pool_out[...] = hist_s[pl.ds(1, POOL_HIST - 1)]

        xs_copy.wait()
        pool_in.wait()
        hists_s[0] = jnp.zeros((batch_s, W_B), F32)
        hrs_s[...] = h0r_ref[...].T
        his_s[...] = h0i_ref[...].T
        h1s_out[...] = _mixer_math(
            xs[...], 0, *params, slab.at[:, pl.ds(0, batch_s), :],
            hsbf.at[:, pl.ds(0, batch_s), :], hrs_s, his_s, hists_s,
            batch=batch_s, steps=1, start_pos=start_pos_s)
        pool_o.start()
        hrs_out[...] = hrs_s[...].T
        his_out[...] = his_s[...].T
        h_copy(i).wait()
        pool_o.wait()


def _mixer(x, xs, h0r_t, h0i_t, pool0_t, w_in, w_out, seq_params, *, steps, start_pos_s):
    batch, seq, _ = x.shape
    batch_s = xs.shape[0]
    assert len(seq_params) == N_SEQ_PARAM and seq % steps == 0 and batch_s <= steps * batch
    f32 = lambda *shape: jax.ShapeDtypeStruct(shape, F32)
    w3 = lambda w: w.reshape(D_MODEL // batch, batch, D_MODEL)
    return pl.pallas_call(
        functools.partial(_mixer_kernel, batch=batch, steps=steps, batch_s=batch_s,
                          start_pos_s=start_pos_s),
        out_shape=(f32(seq, batch, D_MODEL), f32(batch, S5_GROUPS, S5_STATE),
                   f32(batch, S5_GROUPS, S5_STATE), f32(POOL_HIST - 1, batch, W_B),
                   f32(batch_s, D_MODEL), f32(N_STATE, batch_s), f32(N_STATE, batch_s),
                   f32(*pool0_t.shape)),
        grid=(seq // steps,),
        in_specs=[_HBM, _HBM, _VMEM_WHOLE, _VMEM_WHOLE, _HBM, _HBM, _HBM]
        + [_VMEM_WHOLE] * N_SEQ_PARAM,
        out_specs=(_HBM, _full_spec((batch, S5_GROUPS, S5_STATE)),
                   _full_spec((batch, S5_GROUPS, S5_STATE)),
                   _full_spec((POOL_HIST - 1, batch, W_B)),
                   _full_spec((batch_s, D_MODEL)), _full_spec((N_STATE, batch_s)),
                   _full_spec((N_STATE, batch_s)), _HBM),
        scratch_shapes=_mixer_scratch(batch, steps) + [
            pltpu.VMEM((batch_s, N_STATE), F32),
            pltpu.VMEM((batch_s, N_STATE), F32),
            pltpu.VMEM((POOL_HIST, batch_s, W_B), F32),
            pltpu.VMEM((batch_s, D_MODEL), F32),
            pltpu.VMEM((D_MODEL // batch, batch, D_MODEL), BF16),
            pltpu.VMEM((D_MODEL // batch, batch, D_MODEL), BF16),
            pltpu.VMEM((2, steps, batch, D_MODEL), F32),
            pltpu.VMEM((1, steps, batch, D_MODEL), F32),
            pltpu.SemaphoreType.DMA((2,)),
            pltpu.SemaphoreType.DMA((1,)),
            pltpu.SemaphoreType.DMA((2,)),
            pltpu.SemaphoreType.DMA((2,)),
        ],
        compiler_params=_COMPILER_PARAMS,
        name="mixer",
    )(x, xs, h0r_t, h0i_t, pool0_t, w3(w_in), w3(w_out), *seq_params)


MLP_STAGE_ROWS = 128


def _mlp_kernel(h_ref, hs_ref, g_pre_ref, g_post_ref, wup_hbm, wdown_hbm,
                y_hbm, ys_hbm, wup_bf, wdown_bf, stage_up, stage_down, ybuf, ys,
                w_sem, o_sem, y_sem, *, batch, steps):
    i = pl.program_id(0)
    n_chunks = pl.num_programs(0)
    slot = lax.rem(i, 2)

    def y_copy(b, chunk, sl):
        return pltpu.make_async_copy(
            ybuf.at[sl, :, b, :], y_hbm.at[b, pl.ds(chunk * steps, steps), :],
            o_sem.at[sl])

    @pl.when(i == 0)
    def _():
        _cast_weight(wup_hbm, wup_bf, stage_up, w_sem, MLP_STAGE_ROWS)
        _cast_weight(wdown_hbm, wdown_bf, stage_down, w_sem, 4 * MLP_STAGE_ROWS)

    @pl.when(i >= 2)
    def _():
        for b in range(batch):
            y_copy(b, i - 2, slot).wait()

    y = _mlp_math(h_ref[...], g_pre_ref, g_post_ref, wup_bf, wdown_bf)
    ybuf[slot] = y.reshape(steps, batch, D_MODEL)
    for b in range(batch):
        y_copy(b, i, slot).start()

    @pl.when(i == n_chunks - 1)
    def _():
        ys[...] = _mlp_math(hs_ref[...], g_pre_ref, g_post_ref, wup_bf, wdown_bf)
        y_out = pltpu.make_async_copy(ys, ys_hbm.at[:, 0, :], y_sem.at[0])
        y_out.start()

        @pl.when(i >= 1)
        def _():
            for b in range(batch):
                y_copy(b, i - 1, 1 - slot).wait()

        for b in range(batch):
            y_copy(b, i, slot).wait()
        y_out.wait()


def _mlp(h, hs, g_pre, g_post, wup, wdown, *, batch, steps):
    block_rows = steps * batch
    n_rows = h.shape[0]
    batch_s = hs.shape[0]
    assert n_rows % block_rows == 0
    seq = n_rows // batch
    return pl.pallas_call(
        functools.partial(_mlp_kernel, batch=batch, steps=steps),
        out_shape=(jax.ShapeDtypeStruct((batch, seq, D_MODEL), F32),
                   jax.ShapeDtypeStruct((batch_s, 1, D_MODEL), F32)),
        grid=(n_rows // block_rows,),
        in_specs=[
            pl.BlockSpec((block_rows, D_MODEL), lambda i: (i, 0)),
            _VMEM_WHOLE, _VMEM_WHOLE, _VMEM_WHOLE, _HBM, _HBM,
        ],
        out_specs=(_HBM, _HBM),
        scratch_shapes=[
            pltpu.VMEM((D_MODEL, D_FF), BF16),
            pltpu.VMEM((D_FF, D_MODEL), BF16),
            pltpu.VMEM((2, MLP_STAGE_ROWS, D_FF), F32),
            pltpu.VMEM((2, 4 * MLP_STAGE_ROWS, D_MODEL), F32),
            pltpu.VMEM((2, steps, batch, D_MODEL), F32),
            pltpu.VMEM((batch_s, D_MODEL), F32),
            pltpu.SemaphoreType.DMA((2,)),
            pltpu.SemaphoreType.DMA((2,)),
            pltpu.SemaphoreType.DMA((1,)),
        ],
        compiler_params=_COMPILER_PARAMS,
        name="mlp",
    )(h, hs, g_pre, g_post, wup, wdown)


def kernel(x_prompt, x_sample, state_s5_re, state_s5_im, state_pool, norm_mix_pre, norm_mix_post, norm_mlp_pre, norm_mlp_post, w_in, s5_lambda_re, s5_lambda_im, s5_log_dt, s5_b_re, s5_b_im, s5_c_re, s5_c_im, s5_d, s5_w_glu, pool_w, pool_scale, w_out, w_mlp_up, w_mlp_down):
    bp, seq, _ = x_prompt.shape
    bs = x_sample.shape[0]

    a_re, a_im, bm, cm, glu, poolbd = _s5_prep(
        s5_lambda_re, s5_lambda_im, s5_log_dt, s5_b_re, s5_b_im, s5_c_re, s5_c_im,
        s5_w_glu, pool_w)
    row = lambda v: v.reshape(1, -1)
    seq_params = [row(norm_mix_pre), row(norm_mix_post), a_re, a_im, bm, cm, row(s5_d),
                  glu, poolbd, row(pool_scale)]

    st_in = lambda a: jnp.transpose(a, (1, 2, 0)).reshape(N_STATE, bs)
    st_out = lambda a: jnp.transpose(a.reshape(S5_GROUPS, S5_STATE, bs), (2, 0, 1))
    tbc = lambda a: jnp.transpose(a, (1, 0, 2))

    h1p, hpr, hpi, pool_p, h1s, hsr, hsi, pool_s = _mixer(
        x_prompt, x_sample, st_in(state_s5_re), st_in(state_s5_im), tbc(state_pool),
        w_in, w_out, seq_params, steps=PROMPT_STEPS, start_pos_s=PAST_LEN)
    yp, y_sample = _mlp(h1p.reshape(seq * bp, D_MODEL), h1s, row(norm_mlp_pre),
                        row(norm_mlp_post), w_mlp_up, w_mlp_down, batch=bp,
                        steps=PROMPT_STEPS)

    return (yp, y_sample, hpr, hpi, tbc(pool_p),
            st_out(hsr), st_out(hsi), tbc(pool_s))
```

```python
import functools
import math

import jax
import jax.numpy as jnp
from jax import lax
from jax.experimental import pallas as pl
from jax.experimental.pallas import tpu as pltpu

F32 = jnp.float32
BF16 = jnp.bfloat16

D_MODEL = 1024
W_A = 512
W_B = 512
S5_H = 16
S5_GROUPS = 32
S5_STATE = 64
N_STATE = S5_GROUPS * S5_STATE
N_SLAB = 4
SLAB_GROUPS = S5_GROUPS // N_SLAB
SLAB_U = SLAB_GROUPS * S5_H
SLAB_S = SLAB_GROUPS * S5_STATE
POOL_WINDOWS = (2, 4, 8, 16)
POOL_CH = 128
POOL_HIST = 16
D_FF = 4096
FF_CHUNK = 1024
EPS = 1e-6
PAST_LEN = 16384
PROMPT_STEPS = 128
GELU_C = math.sqrt(2.0 / math.pi)

VMEM_LIMIT_BYTES = 62 * 1024 * 1024


def _rms_norm(x, g):
    ms = jnp.mean(x * x, axis=-1, keepdims=True)
    return x * lax.rsqrt(ms + EPS) * g


def _full_spec(shape):
    return pl.BlockSpec(shape, lambda *_: (0,) * len(shape))


def _s5_prep_kernel(lam_re_ref, lam_im_ref, log_dt_ref, b_re_ref, b_im_ref,
                    c_re_ref, c_im_ref, wglu_ref, poolw_ref,
                    a_re_ref, a_im_ref, bm_ref, cm_ref, glu_ref, poolbd_ref):
    lam_re = lam_re_ref[...]
    lam_im = lam_im_ref[...]
    eye = (lax.broadcasted_iota(jnp.int32, (S5_GROUPS, S5_GROUPS), 0)
           == lax.broadcasted_iota(jnp.int32, (S5_GROUPS, S5_GROUPS), 1))
    log_dt = jnp.sum(jnp.where(eye, log_dt_ref[...], 0.0), axis=1, keepdims=True)
    dt = jnp.exp(log_dt)
    mag = jnp.exp(lam_re * dt)
    ang = lam_im * dt
    a_re = mag * jnp.cos(ang)
    a_im = mag * jnp.sin(ang)
    lanes = lambda m: jnp.concatenate([m[g:g + 1, :] for g in range(S5_GROUPS)], axis=1)
    a_re_ref[...] = lanes(a_re)
    a_im_ref[...] = lanes(a_im)
    n_re = a_re - 1.0
    n_im = a_im
    den = lam_re * lam_re + lam_im * lam_im
    k_re = (n_re * lam_re + n_im * lam_im) / den
    k_im = (n_im * lam_re - n_re * lam_im) / den
    per_h = lambda m: jnp.broadcast_to(m[:, None, :], (S5_GROUPS, S5_H, S5_STATE)).reshape(
        S5_GROUPS * S5_H, S5_STATE)
    k_re = per_h(k_re)
    k_im = per_h(k_im)
    b_re = b_re_ref[...]
    b_im = b_im_ref[...]
    bb_re = k_re * b_re - k_im * b_im
    bb_im = k_re * b_im + k_im * b_re
    c_re = c_re_ref[...]
    c_im_neg = -c_im_ref[...]

    rows = lax.broadcasted_iota(jnp.int32, (SLAB_U, SLAB_S), 0) // S5_H
    cols = lax.broadcasted_iota(jnp.int32, (SLAB_U, SLAB_S), 1) // S5_STATE
    diag = rows == cols

    def block_diag(m, j):
        sl = m[j * SLAB_U:(j + 1) * SLAB_U, :]
        tiled = jnp.concatenate([sl] * SLAB_GROUPS, axis=1)
        return jnp.where(diag, tiled, 0.0)

    for j in range(N_SLAB):
        bm_ref[j, :, :SLAB_S] = block_diag(bb_re, j).astype(BF16)
        bm_ref[j, :, SLAB_S:] = block_diag(bb_im, j).astype(BF16)
        cm_ref[j, :SLAB_S, :] = block_diag(c_re, j).T.astype(BF16)
        cm_ref[j, SLAB_S:, :] = block_diag(c_im_neg, j).T.astype(BF16)

    half = W_A // 2
    g_rows = lax.broadcasted_iota(jnp.int32, (half, half), 0) // S5_H
    g_cols = lax.broadcasted_iota(jnp.int32, (half, half), 1) // S5_H
    for t in range(2):
        blk = wglu_ref[t * half:(t + 1) * half, :]
        tiled = jnp.concatenate([blk] * (half // S5_H), axis=1)
        glu_ref[t] = jnp.where(g_rows == g_cols, tiled, 0.0).astype(BF16)

    zeros = jnp.zeros((POOL_CH, POOL_CH), BF16)
    for t in range(2):
        poolbd_ref[t, :POOL_CH, :POOL_CH] = poolw_ref[2 * t].astype(BF16)
        poolbd_ref[t, :POOL_CH, POOL_CH:] = zeros
        poolbd_ref[t, POOL_CH:, :POOL_CH] = zeros
        poolbd_ref[t, POOL_CH:, POOL_CH:] = poolw_ref[2 * t + 1].astype(BF16)


def _s5_prep(lam_re, lam_im, log_dt, b_re, b_im, c_re, c_im, w_glu, pool_w):
    gh_p = lambda b: jnp.transpose(b, (0, 2, 1)).reshape(S5_GROUPS * S5_H, S5_STATE)
    ins = (lam_re, lam_im, log_dt.reshape(1, S5_GROUPS), gh_p(b_re), gh_p(b_im),
           c_re.reshape(S5_GROUPS * S5_H, S5_STATE),
           c_im.reshape(S5_GROUPS * S5_H, S5_STATE),
           w_glu.reshape(S5_GROUPS * S5_H, S5_H), pool_w)
    out_shape = (
        jax.ShapeDtypeStruct((1, N_STATE), F32),
        jax.ShapeDtypeStruct((1, N_STATE), F32),
        jax.ShapeDtypeStruct((N_SLAB, SLAB_U, 2 * SLAB_S), BF16),
        jax.ShapeDtypeStruct((N_SLAB, 2 * SLAB_S, SLAB_U), BF16),
        jax.ShapeDtypeStruct((2, W_A // 2, W_A // 2), BF16),
        jax.ShapeDtypeStruct((2, 2 * POOL_CH, 2 * POOL_CH), BF16),
    )
    return pl.pallas_call(
        _s5_prep_kernel,
        out_shape=out_shape,
        name="s5_prep",
    )(*ins)


def _mixer_math(x, i, g_pre_ref, g_post_ref, win_ref, a_re_ref, a_im_ref, bm_ref,
                cm_ref, d_ref, glu_ref, poolw_ref, pscale_ref, wout_ref,
                slab_ref, hsbf_ref, hr_s, hi_s, hist_s, *, batch, steps, start_pos):
    rows = batch * steps
    xn = _rms_norm(x, g_pre_ref[...]).astype(BF16)
    w_in = win_ref[...].reshape(D_MODEL, D_MODEL)
    ua = jnp.dot(xn, w_in[:, :W_A], preferred_element_type=F32)
    ua_bf = ua.astype(BF16)

    pair = max(1, 16 // batch)

    def project_in(j):
        slab_ref[j % 2] = jnp.dot(ua_bf[:, j * SLAB_U:(j + 1) * SLAB_U], bm_ref[j],
                                  preferred_element_type=F32)

    def scan(j):
        st = slice(j * SLAB_S, (j + 1) * SLAB_S)
        sb = slab_ref.at[j % 2]
        hb = hsbf_ref.at[j % 2]
        ar = jnp.broadcast_to(a_re_ref[:, st], (batch, SLAB_S))
        ai = jnp.broadcast_to(a_im_ref[:, st], (batch, SLAB_S))
        hr = hr_s[:, st]
        hi = hi_s[:, st]
        for t0 in range(0, steps, pair):
            res, ims = [], []
            for t in range(t0, min(t0 + pair, steps)):
                rt = slice(t * batch, (t + 1) * batch)
                nr = ar * hr - ai * hi + sb[rt, :SLAB_S]
                ni = ar * hi + ai * hr + sb[rt, SLAB_S:]
                res.append(nr)
                ims.append(ni)
                hr, hi = nr, ni
            rg = slice(t0 * batch, (t0 + len(res)) * batch)
            hb[rg, :SLAB_S] = jnp.concatenate(res, axis=0).astype(BF16)
            hb[rg, SLAB_S:] = jnp.concatenate(ims, axis=0).astype(BF16)
        hr_s[:, st] = hr
        hi_s[:, st] = hi

    def project_out(j):
        return jnp.dot(hsbf_ref[j % 2], cm_ref[j],
                       preferred_element_type=F32)

    ys = []
    project_in(0)
    scan(0)
    ub = jnp.dot(xn, w_in[:, W_A:], preferred_element_type=F32)
    for j in range(1, N_SLAB):
        project_in(j)
        ys.append(project_out(j - 1))
        scan(j)
    ys.append(project_out(N_SLAB - 1))

    y = jnp.concatenate(ys, axis=1) + d_ref[...] * ua
    y = y * (0.5 * (1.0 + jnp.tanh(GELU_C * (y + 0.044715 * (y * y * y)))))
    y_bf = y.astype(BF16)
    half = W_A // 2
    gate = jnp.concatenate(
        [jnp.dot(y_bf[:, :half], glu_ref[0], preferred_element_type=F32),
         jnp.dot(y_bf[:, half:], glu_ref[1], preferred_element_type=F32)], axis=1)
    ya = y * (1.0 / (1.0 + jnp.exp(-gate)))

    hist = hist_s[...].reshape(POOL_HIST * batch, W_B)
    ext = jnp.concatenate([hist, ub], axis=0)
    n_ext = POOL_HIST * batch + rows
    hist_s[...] = ext[n_ext - POOL_HIST * batch:, :].reshape(POOL_HIST, batch, W_B)
    t_loc = lax.broadcasted_iota(jnp.int32, (rows, 1), 0) // batch
    pos1 = t_loc + (start_pos + 1) + i * steps
    pooled = []
    for gi, w in enumerate(POOL_WINDOWS):
        s = ext[:, gi * POOL_CH:(gi + 1) * POOL_CH]
        span = 1
        while span < w:
            n = s.shape[0]
            s = s[span * batch:, :] + s[:n - span * batch, :]
            span *= 2
        win = s[s.shape[0] - rows:, :]
        count = jnp.minimum(pos1, w).astype(F32)
        pooled.append(win / count - ub[:, gi * POOL_CH:(gi + 1) * POOL_CH])
    pooled = jnp.concatenate(pooled, axis=1).astype(BF16)
    halfb = W_B // 2
    yb = jnp.concatenate(
        [jnp.dot(pooled[:, :halfb], poolw_ref[0], preferred_element_type=F32),
         jnp.dot(pooled[:, halfb:], poolw_ref[1], preferred_element_type=F32)], axis=1)
    yb = yb * pscale_ref[...]

    ycat = jnp.concatenate([ya, yb], axis=1).astype(BF16)
    mix = jnp.dot(ycat, wout_ref[...].reshape(D_MODEL, D_MODEL),
                  preferred_element_type=F32)
    return x + _rms_norm(mix, g_post_ref[...])


def _mlp_math(h, g_pre_ref, g_post_ref, wup_ref, wdown_ref):
    hn = _rms_norm(h, g_pre_ref[...]).astype(BF16)
    ffs = []
    for j in range(D_FF // FF_CHUNK):
        sl = slice(j * FF_CHUNK, (j + 1) * FF_CHUNK)
        up = jnp.dot(hn, wup_ref[:, sl], preferred_element_type=F32)
        up = jnp.maximum(up, 0.0)
        ffs.append((up * up).astype(BF16))
    acc = jnp.dot(jnp.concatenate(ffs, axis=1), wdown_ref[...], preferred_element_type=F32)
    return h + _rms_norm(acc, g_post_ref[...])


def _mixer_scratch(batch, steps):
    return [
        pltpu.VMEM((2, batch * steps, 2 * SLAB_S), F32),
        pltpu.VMEM((2, batch * steps, 2 * SLAB_S), BF16),
        pltpu.VMEM((batch, N_STATE), F32),
        pltpu.VMEM((batch, N_STATE), F32),
        pltpu.VMEM((POOL_HIST, batch, W_B), F32),
    ]


_VMEM_WHOLE = pl.BlockSpec(memory_space=pltpu.VMEM)
_HBM = pl.BlockSpec(memory_space=pl.ANY)
_COMPILER_PARAMS = pltpu.CompilerParams(dimension_semantics=("arbitrary",),
                                        vmem_limit_bytes=VMEM_LIMIT_BYTES)


def _cast_weight(w_hbm, dst_ref, stage_ref, sem, row_chunk):
    n = w_hbm.shape[0] // row_chunk

    def copy(c):
        return pltpu.make_async_copy(w_hbm.at[pl.ds(c * row_chunk, row_chunk)],
                                     stage_ref.at[c % 2], sem.at[c % 2])

    copy(0).start()
    for c in range(n):
        if c + 1 < n:
            copy(c + 1).start()
        copy(c).wait()
        dst_ref[pl.ds(c * row_chunk, row_chunk)] = stage_ref[c % 2].astype(BF16)


N_SEQ_PARAM = 10


def _mixer_kernel(x_hbm, xs_hbm, h0r_ref, h0i_ref, pool0_hbm, win_hbm, wout_hbm,
                  *refs, batch, steps, batch_s, start_pos_s):
    g_pre, g_post, a_re, a_im, bm, cm, d, glu, poolbd, pscale = refs[:N_SEQ_PARAM]
    (h1_hbm, hr_out, hi_out, pool_out, h1s_out, hrs_out, his_out,
     pools_hbm) = refs[N_SEQ_PARAM:N_SEQ_PARAM + 8]
    (slab, hsbf, hr_s, hi_s, hist_s, hrs_s, his_s, hists_s, xs, win_bf, wout_bf, xbuf, hbuf,
     in_sem, out_sem, s_sem, w_sem) = refs[N_SEQ_PARAM + 8:]
    params = (g_pre, g_post, win_bf, a_re, a_im, bm, cm, d, glu, poolbd, pscale, wout_bf)
    i = pl.program_id(0)
    n_chunks = pl.num_programs(0)
    slot = lax.rem(i, 2)

    def x_copy(b, chunk, sl):
        return pltpu.make_async_copy(
            x_hbm.at[b, pl.ds(chunk * steps, steps), :], xbuf.at[sl, :, b, :],
            in_sem.at[sl])

    def h_copy(chunk, sl):
        return pltpu.make_async_copy(
            hbuf.at[sl], h1_hbm.at[pl.ds(chunk * steps, steps)], out_sem.at[sl])

    xs_copy = pltpu.make_async_copy(xs_hbm.at[:, 0, :], xs, s_sem.at[0])
    pool_in = pltpu.make_async_copy(pool0_hbm, hists_s.at[pl.ds(1, POOL_HIST - 1)],
                                    s_sem.at[1])
    pool_o = pltpu.make_async_copy(hists_s.at[pl.ds(1, POOL_HIST - 1)], pools_hbm,
                                   s_sem.at[1])

    @pl.when(i == 0)
    def _():
        for b in range(batch):
            x_copy(b, 0, 0).start()
        xs_copy.start()
        pool_in.start()
        hr_s[...] = jnp.zeros_like(hr_s)
        hi_s[...] = jnp.zeros_like(hi_s)
        hist_s[...] = jnp.zeros_like(hist_s)
        _cast_weight(win_hbm, win_bf, hbuf, w_sem, steps)
        _cast_weight(wout_hbm, wout_bf, hbuf, w_sem, steps)

    @pl.when(i + 1 < n_chunks)
    def _():
        for b in range(batch):
            x_copy(b, i + 1, 1 - slot).start()

    @pl.when(i >= 2)
    def _():
        h_copy(i - 2, slot).wait()

    for b in range(batch):
        x_copy(b, i, slot).wait()
    x = xbuf[slot].reshape(steps * batch, D_MODEL)
    h1 = _mixer_math(x, i, *params, slab, hsbf, hr_s, hi_s, hist_s,
                     batch=batch, steps=steps, start_pos=0)

    hbuf[slot] = h1.reshape(steps, batch, D_MODEL)
    h_copy(i, slot).start()

    @pl.when(i == n_chunks - 1)
    def _():
        for g in range(S5_GROUPS):
            hr_out[:, g, :] = hr_s[:, g * S5_STATE:(g + 1) * S5_STATE]
            hi_out[:, g, :] = hi_s[:, g * S5_STATE:(g + 1) * S5_STATE]
        pool_out[...] = hist_s[pl.ds(1, POOL_HIST - 1)]

        xs_copy.wait()
        pool_in.wait()
        hists_s[0] = jnp.zeros((batch_s, W_B), F32)
        hrs_s[...] = h0r_ref[...].T
        his_s[...] = h0i_ref[...].T
        h1s_out[...] = _mixer_math(
            xs[...], 0, *params, slab.at[:, pl.ds(0, batch_s), :],
            hsbf.at[:, pl.ds(0, batch_s), :], hrs_s, his_s, hists_s,
            batch=batch_s, steps=1, start_pos=start_pos_s)
        pool_o.start()
        hrs_out[...] = hrs_s[...].T
        his_out[...] = his_s[...].T
        @pl.when(i >= 1)
        def _():
            h_copy(i - 1, 1 - slot).wait()

        h_copy(i, slot).wait()
        pool_o.wait()


def _mixer(x, xs, h0r_t, h0i_t, pool0_t, w_in, w_out, seq_params, *, steps, start_pos_s):
    batch, seq, _ = x.shape
    batch_s = xs.shape[0]
    assert len(seq_params) == N_SEQ_PARAM and seq % steps == 0 and batch_s <= steps * batch
    f32 = lambda *shape: jax.ShapeDtypeStruct(shape, F32)
    w3 = lambda w: w.reshape(D_MODEL // batch, batch, D_MODEL)
    return pl.pallas_call(
        functools.partial(_mixer_kernel, batch=batch, steps=steps, batch_s=batch_s,
                          start_pos_s=start_pos_s),
        out_shape=(f32(seq, batch, D_MODEL), f32(batch, S5_GROUPS, S5_STATE),
                   f32(batch, S5_GROUPS, S5_STATE), f32(POOL_HIST - 1, batch, W_B),
                   f32(batch_s, D_MODEL), f32(N_STATE, batch_s), f32(N_STATE, batch_s),
                   f32(*pool0_t.shape)),
        grid=(seq // steps,),
        in_specs=[_HBM, _HBM, _VMEM_WHOLE, _VMEM_WHOLE, _HBM, _HBM, _HBM]
        + [_VMEM_WHOLE] * N_SEQ_PARAM,
        out_specs=(_HBM, _full_spec((batch, S5_GROUPS, S5_STATE)),
                   _full_spec((batch, S5_GROUPS, S5_STATE)),
                   _full_spec((POOL_HIST - 1, batch, W_B)),
                   _full_spec((batch_s, D_MODEL)), _full_spec((N_STATE, batch_s)),
                   _full_spec((N_STATE, batch_s)), _HBM),
        scratch_shapes=_mixer_scratch(batch, steps) + [
            pltpu.VMEM((batch_s, N_STATE), F32),
            pltpu.VMEM((batch_s, N_STATE), F32),
            pltpu.VMEM((POOL_HIST, batch_s, W_B), F32),
            pltpu.VMEM((batch_s, D_MODEL), F32),
            pltpu.VMEM((D_MODEL // batch, batch, D_MODEL), BF16),
            pltpu.VMEM((D_MODEL // batch, batch, D_MODEL), BF16),
            pltpu.VMEM((2, steps, batch, D_MODEL), F32),
            pltpu.VMEM((2, steps, batch, D_MODEL), F32),
            pltpu.SemaphoreType.DMA((2,)),
            pltpu.SemaphoreType.DMA((2,)),
            pltpu.SemaphoreType.DMA((2,)),
            pltpu.SemaphoreType.DMA((2,)),
        ],
        compiler_params=_COMPILER_PARAMS,
        name="mixer",
    )(x, xs, h0r_t, h0i_t, pool0_t, w3(w_in), w3(w_out), *seq_params)


MLP_STAGE_ROWS = 128


def _mlp_kernel(h_ref, hs_ref, g_pre_ref, g_post_ref, wup_hbm, wdown_hbm,
                y_hbm, ys_hbm, wup_bf, wdown_bf, stage_up, stage_down, ybuf, ys,
                w_sem, o_sem, y_sem, *, batch, steps):
    i = pl.program_id(0)
    n_chunks = pl.num_programs(0)
    slot = lax.rem(i, 2)

    def y_copy(b, chunk, sl):
        return pltpu.make_async_copy(
            ybuf.at[sl, :, b, :], y_hbm.at[b, pl.ds(chunk * steps, steps), :],
            o_sem.at[sl])

    @pl.when(i == 0)
    def _():
        _cast_weight(wup_hbm, wup_bf, stage_up, w_sem, MLP_STAGE_ROWS)
        _cast_weight(wdown_hbm, wdown_bf, stage_down, w_sem, 4 * MLP_STAGE_ROWS)

    @pl.when(i >= 2)
    def _():
        for b in range(batch):
            y_copy(b, i - 2, slot).wait()

    y = _mlp_math(h_ref[...], g_pre_ref, g_post_ref, wup_bf, wdown_bf)
    ybuf[slot] = y.reshape(steps, batch, D_MODEL)
    for b in range(batch):
        y_copy(b, i, slot).start()

    @pl.when(i == n_chunks - 1)
    def _():
        ys[...] = _mlp_math(hs_ref[...], g_pre_ref, g_post_ref, wup_bf, wdown_bf)
        y_out = pltpu.make_async_copy(ys, ys_hbm.at[:, 0, :], y_sem.at[0])
        y_out.start()

        @pl.when(i >= 1)
        def _():
            for b in range(batch):
                y_copy(b, i - 1, 1 - slot).wait()

        for b in range(batch):
            y_copy(b, i, slot).wait()
        y_out.wait()


def _mlp(h, hs, g_pre, g_post, wup, wdown, *, batch, steps):
    block_rows = steps * batch
    n_rows = h.shape[0]
    batch_s = hs.shape[0]
    assert n_rows % block_rows == 0
    seq = n_rows // batch
    return pl.pallas_call(
        functools.partial(_mlp_kernel, batch=batch, steps=steps),
        out_shape=(jax.ShapeDtypeStruct((batch, seq, D_MODEL), F32),
                   jax.ShapeDtypeStruct((batch_s, 1, D_MODEL), F32)),
        grid=(n_rows // block_rows,),
        in_specs=[
            pl.BlockSpec((block_rows, D_MODEL), lambda i: (i, 0)),
            _VMEM_WHOLE, _VMEM_WHOLE, _VMEM_WHOLE, _HBM, _HBM,
        ],
        out_specs=(_HBM, _HBM),
        scratch_shapes=[
            pltpu.VMEM((D_MODEL, D_FF), BF16),
            pltpu.VMEM((D_FF, D_MODEL), BF16),
            pltpu.VMEM((2, MLP_STAGE_ROWS, D_FF), F32),
            pltpu.VMEM((2, 4 * MLP_STAGE_ROWS, D_MODEL), F32),
            pltpu.VMEM((2, steps, batch, D_MODEL), F32),
            pltpu.VMEM((batch_s, D_MODEL), F32),
            pltpu.SemaphoreType.DMA((2,)),
            pltpu.SemaphoreType.DMA((2,)),
            pltpu.SemaphoreType.DMA((1,)),
        ],
        compiler_params=_COMPILER_PARAMS,
        name="mlp",
    )(h, hs, g_pre, g_post, wup, wdown)


def kernel(x_prompt, x_sample, state_s5_re, state_s5_im, state_pool, norm_mix_pre, norm_mix_post, norm_mlp_pre, norm_mlp_post, w_in, s5_lambda_re, s5_lambda_im, s5_log_dt, s5_b_re, s5_b_im, s5_c_re, s5_c_im, s5_d, s5_w_glu, pool_w, pool_scale, w_out, w_mlp_up, w_mlp_down):
    bp, seq, _ = x_prompt.shape
    bs = x_sample.shape[0]

    a_re, a_im, bm, cm, glu, poolbd = _s5_prep(
        s5_lambda_re, s5_lambda_im, s5_log_dt, s5_b_re, s5_b_im, s5_c_re, s5_c_im,
        s5_w_glu, pool_w)
    row = lambda v: v.reshape(1, -1)
    seq_params = [row(norm_mix_pre), row(norm_mix_post), a_re, a_im, bm, cm, row(s5_d),
                  glu, poolbd, row(pool_scale)]

    st_in = lambda a: jnp.transpose(a, (1, 2, 0)).reshape(N_STATE, bs)
    st_out = lambda a: jnp.transpose(a.reshape(S5_GROUPS, S5_STATE, bs), (2, 0, 1))
    tbc = lambda a: jnp.transpose(a, (1, 0, 2))

    h1p, hpr, hpi, pool_p, h1s, hsr, hsi, pool_s = _mixer(
        x_prompt, x_sample, st_in(state_s5_re), st_in(state_s5_im), tbc(state_pool),
        w_in, w_out, seq_params, steps=PROMPT_STEPS, start_pos_s=PAST_LEN)
    yp, y_sample = _mlp(h1p.reshape(seq * bp, D_MODEL), h1s, row(norm_mlp_pre),
                        row(norm_mlp_post), w_mlp_up, w_mlp_down, batch=bp,
                        steps=PROMPT_STEPS)

    return (yp, y_sample, hpr, hpi, tbc(pool_p),
            st_out(hsr), st_out(hsi), tbc(pool_s))
```

```python
import functools
import math

import jax
import jax.numpy as jnp
from jax import lax
from jax.experimental import pallas as pl
from jax.experimental.pallas import tpu as pltpu

F32 = jnp.float32
BF16 = jnp.bfloat16

D_MODEL = 1024
W_A = 512
W_B = 512
S5_H = 16
S5_GROUPS = 32
S5_STATE = 64
N_STATE = S5_GROUPS * S5_STATE
N_SLAB = 4
SLAB_GROUPS = S5_GROUPS // N_SLAB
SLAB_U = SLAB_GROUPS * S5_H
SLAB_S = SLAB_GROUPS * S5_STATE
POOL_WINDOWS = (2, 4, 8, 16)
POOL_CH = 128
POOL_HIST = 16
D_FF = 4096
FF_CHUNK = 1024
EPS = 1e-6
PAST_LEN = 16384
PROMPT_STEPS = 128
GELU_C = math.sqrt(2.0 / math.pi)

VMEM_LIMIT_BYTES = 62 * 1024 * 1024


def _rms_norm(x, g):
    ms = jnp.mean(x * x, axis=-1, keepdims=True)
    return x * lax.rsqrt(ms + EPS) * g


def _full_spec(shape):
    return pl.BlockSpec(shape, lambda *_: (0,) * len(shape))


def _s5_prep_kernel(lam_re_ref, lam_im_ref, log_dt_ref, b_re_ref, b_im_ref,
                    c_re_ref, c_im_ref, wglu_ref, poolw_ref,
                    a_re_ref, a_im_ref, bm_ref, cm_ref, glu_ref, poolbd_ref):
    lam_re = lam_re_ref[...]
    lam_im = lam_im_ref[...]
    eye = (lax.broadcasted_iota(jnp.int32, (S5_GROUPS, S5_GROUPS), 0)
           == lax.broadcasted_iota(jnp.int32, (S5_GROUPS, S5_GROUPS), 1))
    log_dt = jnp.sum(jnp.where(eye, log_dt_ref[...], 0.0), axis=1, keepdims=True)
    dt = jnp.exp(log_dt)
    mag = jnp.exp(lam_re * dt)
    ang = lam_im * dt
    a_re = mag * jnp.cos(ang)
    a_im = mag * jnp.sin(ang)
    lanes = lambda m: jnp.concatenate([m[g:g + 1, :] for g in range(S5_GROUPS)], axis=1)
    a_re_ref[...] = lanes(a_re)
    a_im_ref[...] = lanes(a_im)
    n_re = a_re - 1.0
    n_im = a_im
    den = lam_re * lam_re + lam_im * lam_im
    k_re = (n_re * lam_re + n_im * lam_im) / den
    k_im = (n_im * lam_re - n_re * lam_im) / den
    per_h = lambda m: jnp.broadcast_to(m[:, None, :], (S5_GROUPS, S5_H, S5_STATE)).reshape(
        S5_GROUPS * S5_H, S5_STATE)
    k_re = per_h(k_re)
    k_im = per_h(k_im)
    b_re = b_re_ref[...]
    b_im = b_im_ref[...]
    bb_re = k_re * b_re - k_im * b_im
    bb_im = k_re * b_im + k_im * b_re
    c_re = c_re_ref[...]
    c_im_neg = -c_im_ref[...]

    rows = lax.broadcasted_iota(jnp.int32, (SLAB_U, SLAB_S), 0) // S5_H
    cols = lax.broadcasted_iota(jnp.int32, (SLAB_U, SLAB_S), 1) // S5_STATE
    diag = rows == cols

    def block_diag(m, j):
        sl = m[j * SLAB_U:(j + 1) * SLAB_U, :]
        tiled = jnp.concatenate([sl] * SLAB_GROUPS, axis=1)
        return jnp.where(diag, tiled, 0.0)

    for j in range(N_SLAB):
        bm_ref[j, :, :SLAB_S] = block_diag(bb_re, j).astype(BF16)
        bm_ref[j, :, SLAB_S:] = block_diag(bb_im, j).astype(BF16)
        cm_ref[j, :SLAB_S, :] = block_diag(c_re, j).T.astype(BF16)
        cm_ref[j, SLAB_S:, :] = block_diag(c_im_neg, j).T.astype(BF16)

    half = W_A // 2
    g_rows = lax.broadcasted_iota(jnp.int32, (half, half), 0) // S5_H
    g_cols = lax.broadcasted_iota(jnp.int32, (half, half), 1) // S5_H
    for t in range(2):
        blk = wglu_ref[t * half:(t + 1) * half, :]
        tiled = jnp.concatenate([blk] * (half // S5_H), axis=1)
        glu_ref[t] = jnp.where(g_rows == g_cols, tiled, 0.0).astype(BF16)

    zeros = jnp.zeros((POOL_CH, POOL_CH), BF16)
    for t in range(2):
        poolbd_ref[t, :POOL_CH, :POOL_CH] = poolw_ref[2 * t].astype(BF16)
        poolbd_ref[t, :POOL_CH, POOL_CH:] = zeros
        poolbd_ref[t, POOL_CH:, :POOL_CH] = zeros
        poolbd_ref[t, POOL_CH:, POOL_CH:] = poolw_ref[2 * t + 1].astype(BF16)


def _s5_prep(lam_re, lam_im, log_dt, b_re, b_im, c_re, c_im, w_glu, pool_w):
    gh_p = lambda b: jnp.transpose(b, (0, 2, 1)).reshape(S5_GROUPS * S5_H, S5_STATE)
    ins = (lam_re, lam_im, log_dt.reshape(1, S5_GROUPS), gh_p(b_re), gh_p(b_im),
           c_re.reshape(S5_GROUPS * S5_H, S5_STATE),
           c_im.reshape(S5_GROUPS * S5_H, S5_STATE),
           w_glu.reshape(S5_GROUPS * S5_H, S5_H), pool_w)
    out_shape = (
        jax.ShapeDtypeStruct((1, N_STATE), F32),
        jax.ShapeDtypeStruct((1, N_STATE), F32),
        jax.ShapeDtypeStruct((N_SLAB, SLAB_U, 2 * SLAB_S), BF16),
        jax.ShapeDtypeStruct((N_SLAB, 2 * SLAB_S, SLAB_U), BF16),
        jax.ShapeDtypeStruct((2, W_A // 2, W_A // 2), BF16),
        jax.ShapeDtypeStruct((2, 2 * POOL_CH, 2 * POOL_CH), BF16),
    )
    return pl.pallas_call(
        _s5_prep_kernel,
        out_shape=out_shape,
        name="s5_prep",
    )(*ins)


def _mixer_math(x, i, g_pre_ref, g_post_ref, win_ref, a_re_ref, a_im_ref, bm_ref,
                cm_ref, d_ref, glu_ref, poolw_ref, pscale_ref, wout_ref,
                slab_ref, hsbf_ref, hr_s, hi_s, hist_s, *, batch, steps, start_pos):
    rows = batch * steps
    xn = _rms_norm(x, g_pre_ref[...]).astype(BF16)
    w_in = win_ref[...].reshape(D_MODEL, D_MODEL)
    ua = jnp.dot(xn, w_in[:, :W_A], preferred_element_type=F32)
    ua_bf = ua.astype(BF16)

    pair = max(1, 16 // batch)

    def project_in(j):
        slab_ref[j % 2] = jnp.dot(ua_bf[:, j * SLAB_U:(j + 1) * SLAB_U], bm_ref[j],
                                  preferred_element_type=F32)

    def scan(j):
        st = slice(j * SLAB_S, (j + 1) * SLAB_S)
        sb = slab_ref.at[j % 2]
        hb = hsbf_ref.at[j % 2]
        ar = jnp.broadcast_to(a_re_ref[:, st], (batch, SLAB_S))
        ai = jnp.broadcast_to(a_im_ref[:, st], (batch, SLAB_S))
        hr = hr_s[:, st]
        hi = hi_s[:, st]
        for t0 in range(0, steps, pair):
            res, ims = [], []
            for t in range(t0, min(t0 + pair, steps)):
                rt = slice(t * batch, (t + 1) * batch)
                nr = ar * hr - ai * hi + sb[rt, :SLAB_S]
                ni = ar * hi + ai * hr + sb[rt, SLAB_S:]
                res.append(nr)
                ims.append(ni)
                hr, hi = nr, ni
            rg = slice(t0 * batch, (t0 + len(res)) * batch)
            hb[rg, :SLAB_S] = jnp.concatenate(res, axis=0).astype(BF16)
            hb[rg, SLAB_S:] = jnp.concatenate(ims, axis=0).astype(BF16)
        hr_s[:, st] = hr
        hi_s[:, st] = hi

    def project_out(j):
        return jnp.dot(hsbf_ref[j % 2], cm_ref[j],
                       preferred_element_type=F32)

    ys = []
    project_in(0)
    scan(0)
    ub = jnp.dot(xn, w_in[:, W_A:], preferred_element_type=F32)
    for j in range(1, N_SLAB):
        project_in(j)
        ys.append(project_out(j - 1))
        scan(j)
    ys.append(project_out(N_SLAB - 1))

    y = jnp.concatenate(ys, axis=1) + d_ref[...] * ua
    y = y * (0.5 * (1.0 + jnp.tanh(GELU_C * (y + 0.044715 * (y * y * y)))))
    y_bf = y.astype(BF16)
    half = W_A // 2
    gate = jnp.concatenate(
        [jnp.dot(y_bf[:, :half], glu_ref[0], preferred_element_type=F32),
         jnp.dot(y_bf[:, half:], glu_ref[1], preferred_element_type=F32)], axis=1)
    ya = y * (1.0 / (1.0 + jnp.exp(-gate)))

    hist = hist_s[...].reshape(POOL_HIST * batch, W_B)
    ext = jnp.concatenate([hist, ub], axis=0)
    n_ext = POOL_HIST * batch + rows
    hist_s[...] = ext[n_ext - POOL_HIST * batch:, :].reshape(POOL_HIST, batch, W_B)
    t_loc = lax.broadcasted_iota(jnp.int32, (rows, 1), 0) // batch
    pos1 = t_loc + (start_pos + 1) + i * steps
    pooled = []
    for gi, w in enumerate(POOL_WINDOWS):
        s = ext[:, gi * POOL_CH:(gi + 1) * POOL_CH]
        span = 1
        while span < w:
            n = s.shape[0]
            s = s[span * batch:, :] + s[:n - span * batch, :]
            span *= 2
        win = s[s.shape[0] - rows:, :]
        count = jnp.minimum(pos1, w).astype(F32)
        pooled.append(win / count - ub[:, gi * POOL_CH:(gi + 1) * POOL_CH])
    pooled = jnp.concatenate(pooled, axis=1).astype(BF16)
    halfb = W_B // 2
    yb = jnp.concatenate(
        [jnp.dot(pooled[:, :halfb], poolw_ref[0], preferred_element_type=F32),
         jnp.dot(pooled[:, halfb:], poolw_ref[1], preferred_element_type=F32)], axis=1)
    yb = yb * pscale_ref[...]

    ycat = jnp.concatenate([ya, yb], axis=1).astype(BF16)
    mix = jnp.dot(ycat, wout_ref[...].reshape(D_MODEL, D_MODEL),
                  preferred_element_type=F32)
    return x + _rms_norm(mix, g_post_ref[...])


def _mlp_math(h, g_pre_ref, g_post_ref, wup_ref, wdown_ref):
    hn = _rms_norm(h, g_pre_ref[...]).astype(BF16)
    ffs = []
    for j in range(D_FF // FF_CHUNK):
        sl = slice(j * FF_CHUNK, (j + 1) * FF_CHUNK)
        up = jnp.dot(hn, wup_ref[:, sl], preferred_element_type=F32)
        up = jnp.maximum(up, 0.0)
        ffs.append((up * up).astype(BF16))
    acc = jnp.dot(jnp.concatenate(ffs, axis=1), wdown_ref[...], preferred_element_type=F32)
    return h + _rms_norm(acc, g_post_ref[...])


def _mixer_scratch(batch, steps):
    return [
        pltpu.VMEM((2, batch * steps, 2 * SLAB_S), F32),
        pltpu.VMEM((2, batch * steps, 2 * SLAB_S), BF16),
        pltpu.VMEM((batch, N_STATE), F32),
        pltpu.VMEM((batch, N_STATE), F32),
        pltpu.VMEM((POOL_HIST, batch, W_B), F32),
    ]


_VMEM_WHOLE = pl.BlockSpec(memory_space=pltpu.VMEM)
_HBM = pl.BlockSpec(memory_space=pl.ANY)
_COMPILER_PARAMS = pltpu.CompilerParams(dimension_semantics=("arbitrary",),
                                        vmem_limit_bytes=VMEM_LIMIT_BYTES)


def _cast_weights(sem, jobs):
    def copy(job, c):
        w_hbm, _, stage_ref, first_slot, first_sem, row_chunk = job
        return pltpu.make_async_copy(w_hbm.at[pl.ds(c * row_chunk, row_chunk)],
                                     stage_ref.at[first_slot + c % 2],
                                     sem.at[first_sem + c % 2])

    counts = [job[0].shape[0] // job[5] for job in jobs]
    for job in jobs:
        copy(job, 0).start()
    for c in range(max(counts)):
        for job, n in zip(jobs, counts):
            if c >= n:
                continue
            _, dst_ref, stage_ref, first_slot, _, row_chunk = job
            if c + 1 < n:
                copy(job, c + 1).start()
            copy(job, c).wait()
            dst_ref[pl.ds(c * row_chunk, row_chunk)] = (
                stage_ref[first_slot + c % 2].astype(BF16))


N_SEQ_PARAM = 10


def _mixer_kernel(x_hbm, xs_hbm, h0r_ref, h0i_ref, pool0_hbm, win_hbm, wout_hbm,
                  *refs, batch, steps, batch_s, start_pos_s):
    g_pre, g_post, a_re, a_im, bm, cm, d, glu, poolbd, pscale = refs[:N_SEQ_PARAM]
    (h1_hbm, hr_out, hi_out, pool_out, h1s_out, hrs_out, his_out,
     pools_hbm) = refs[N_SEQ_PARAM:N_SEQ_PARAM + 8]
    (slab, hsbf, hr_s, hi_s, hist_s, hrs_s, his_s, hists_s, xs, win_bf, wout_bf, xbuf, hbuf,
     in_sem, out_sem, s_sem, w_sem) = refs[N_SEQ_PARAM + 8:]
    params = (g_pre, g_post, win_bf, a_re, a_im, bm, cm, d, glu, poolbd, pscale, wout_bf)
    i = pl.program_id(0)
    n_chunks = pl.num_programs(0)
    slot = lax.rem(i, 2)

    def x_copy(b, chunk, sl):
        return pltpu.make_async_copy(
            x_hbm.at[b, pl.ds(chunk * steps, steps), :], xbuf.at[sl, :, b, :],
            in_sem.at[sl])

    def h_copy(chunk, sl):
        return pltpu.make_async_copy(
            hbuf.at[sl], h1_hbm.at[pl.ds(chunk * steps, steps)], out_sem.at[sl])

    xs_copy = pltpu.make_async_copy(xs_hbm.at[:, 0, :], xs, s_sem.at[0])
    pool_in = pltpu.make_async_copy(pool0_hbm, hists_s.at[pl.ds(1, POOL_HIST - 1)],
                                    s_sem.at[1])
    pool_o = pltpu.make_async_copy(hists_s.at[pl.ds(1, POOL_HIST - 1)], pools_hbm,
                                   s_sem.at[1])

    @pl.when(i == 0)
    def _():
        for b in range(batch):
            x_copy(b, 0, 0).start()
        xs_copy.start()
        pool_in.start()
        hr_s[...] = jnp.zeros_like(hr_s)
        hi_s[...] = jnp.zeros_like(hi_s)
        hist_s[...] = jnp.zeros_like(hist_s)
        _cast_weights(w_sem, [(win_hbm, win_bf, hbuf, 0, 0, steps),
                              (wout_hbm, wout_bf, hbuf, 1, 1, steps)])

    @pl.when(i + 1 < n_chunks)
    def _():
        for b in range(batch):
            x_copy(b, i + 1, 1 - slot).start()

    @pl.when(i >= 2)
    def _():
        h_copy(i - 2, slot).wait()

    for b in range(batch):
        x_copy(b, i, slot).wait()
    x = xbuf[slot].reshape(steps * batch, D_MODEL)
    h1 = _mixer_math(x, i, *params, slab, hsbf, hr_s, hi_s, hist_s,
                     batch=batch, steps=steps, start_pos=0)

    hbuf[slot] = h1.reshape(steps, batch, D_MODEL)
    h_copy(i, slot).start()

    @pl.when(i == n_chunks - 1)
    def _():
        for g in range(S5_GROUPS):
            hr_out[:, g, :] = hr_s[:, g * S5_STATE:(g + 1) * S5_STATE]
            hi_out[:, g, :] = hi_s[:, g * S5_STATE:(g + 1) * S5_STATE]
        pool_out[...] = hist_s[pl.ds(1, POOL_HIST - 1)]

        xs_copy.wait()
        pool_in.wait()
        hists_s[0] = jnp.zeros((batch_s, W_B), F32)
        hrs_s[...] = h0r_ref[...].T
        his_s[...] = h0i_ref[...].T
        h1s_out[...] = _mixer_math(
            xs[...], 0, *params, slab.at[:, pl.ds(0, batch_s), :],
            hsbf.at[:, pl.ds(0, batch_s), :], hrs_s, his_s, hists_s,
            batch=batch_s, steps=1, start_pos=start_pos_s)
        pool_o.start()
        hrs_out[...] = hrs_s[...].T
        his_out[...] = his_s[...].T
        @pl.when(i >= 1)
        def _():
            h_copy(i - 1, 1 - slot).wait()

        h_copy(i, slot).wait()
        pool_o.wait()


def _mixer(x, xs, h0r_t, h0i_t, pool0_t, w_in, w_out, seq_params, *, steps, start_pos_s):
    batch, seq, _ = x.shape
    batch_s = xs.shape[0]
    assert len(seq_params) == N_SEQ_PARAM and seq % steps == 0 and batch_s <= steps * batch
    f32 = lambda *shape: jax.ShapeDtypeStruct(shape, F32)
    w3 = lambda w: w.reshape(D_MODEL // batch, batch, D_MODEL)
    return pl.pallas_call(
        functools.partial(_mixer_kernel, batch=batch, steps=steps, batch_s=batch_s,
                          start_pos_s=start_pos_s),
        out_shape=(f32(seq, batch, D_MODEL), f32(batch, S5_GROUPS, S5_STATE),
                   f32(batch, S5_GROUPS, S5_STATE), f32(POOL_HIST - 1, batch, W_B),
                   f32(batch_s, D_MODEL), f32(N_STATE, batch_s), f32(N_STATE, batch_s),
                   f32(*pool0_t.shape)),
        grid=(seq // steps,),
        in_specs=[_HBM, _HBM, _VMEM_WHOLE, _VMEM_WHOLE, _HBM, _HBM, _HBM]
        + [_VMEM_WHOLE] * N_SEQ_PARAM,
        out_specs=(_HBM, _full_spec((batch, S5_GROUPS, S5_STATE)),
                   _full_spec((batch, S5_GROUPS, S5_STATE)),
                   _full_spec((POOL_HIST - 1, batch, W_B)),
                   _full_spec((batch_s, D_MODEL)), _full_spec((N_STATE, batch_s)),
                   _full_spec((N_STATE, batch_s)), _HBM),
        scratch_shapes=_mixer_scratch(batch, steps) + [
            pltpu.VMEM((batch_s, N_STATE), F32),
            pltpu.VMEM((batch_s, N_STATE), F32),
            pltpu.VMEM((POOL_HIST, batch_s, W_B), F32),
            pltpu.VMEM((batch_s, D_MODEL), F32),
            pltpu.VMEM((D_MODEL // batch, batch, D_MODEL), BF16),
            pltpu.VMEM((D_MODEL // batch, batch, D_MODEL), BF16),
            pltpu.VMEM((2, steps, batch, D_MODEL), F32),
            pltpu.VMEM((2, steps, batch, D_MODEL), F32),
            pltpu.SemaphoreType.DMA((2,)),
            pltpu.SemaphoreType.DMA((2,)),
            pltpu.SemaphoreType.DMA((2,)),
            pltpu.SemaphoreType.DMA((2,)),
        ],
        compiler_params=_COMPILER_PARAMS,
        name="mixer",
    )(x, xs, h0r_t, h0i_t, pool0_t, w3(w_in), w3(w_out), *seq_params)


MLP_STAGE_ROWS = 128


def _mlp_kernel(h_ref, hs_ref, g_pre_ref, g_post_ref, wup_hbm, wdown_hbm,
                y_hbm, ys_hbm, wup_bf, wdown_bf, stage_up, stage_down, ybuf, ys,
                w_sem, o_sem, y_sem, *, batch, steps):
    i = pl.program_id(0)
    n_chunks = pl.num_programs(0)
    slot = lax.rem(i, 2)

    def y_copy(b, chunk, sl):
        return pltpu.make_async_copy(
            ybuf.at[sl, :, b, :], y_hbm.at[b, pl.ds(chunk * steps, steps), :],
            o_sem.at[sl])

    @pl.when(i == 0)
    def _():
        _cast_weights(w_sem, [(wup_hbm, wup_bf, stage_up, 0, 0, MLP_STAGE_ROWS),
                              (wdown_hbm, wdown_bf, stage_down, 0, 2, 4 * MLP_STAGE_ROWS)])

    @pl.when(i >= 2)
    def _():
        for b in range(batch):
            y_copy(b, i - 2, slot).wait()

    y = _mlp_math(h_ref[...], g_pre_ref, g_post_ref, wup_bf, wdown_bf)
    ybuf[slot] = y.reshape(steps, batch, D_MODEL)
    for b in range(batch):
        y_copy(b, i, slot).start()

    @pl.when(i == n_chunks - 1)
    def _():
        ys[...] = _mlp_math(hs_ref[...], g_pre_ref, g_post_ref, wup_bf, wdown_bf)
        y_out = pltpu.make_async_copy(ys, ys_hbm.at[:, 0, :], y_sem.at[0])
        y_out.start()

        @pl.when(i >= 1)
        def _():
            for b in range(batch):
                y_copy(b, i - 1, 1 - slot).wait()

        for b in range(batch):
            y_copy(b, i, slot).wait()
        y_out.wait()


def _mlp(h, hs, g_pre, g_post, wup, wdown, *, batch, steps):
    block_rows = steps * batch
    n_rows = h.shape[0]
    batch_s = hs.shape[0]
    assert n_rows % block_rows == 0
    seq = n_rows // batch
    return pl.pallas_call(
        functools.partial(_mlp_kernel, batch=batch, steps=steps),
        out_shape=(jax.ShapeDtypeStruct((batch, seq, D_MODEL), F32),
                   jax.ShapeDtypeStruct((batch_s, 1, D_MODEL), F32)),
        grid=(n_rows // block_rows,),
        in_specs=[
            pl.BlockSpec((block_rows, D_MODEL), lambda i: (i, 0)),
            _VMEM_WHOLE, _VMEM_WHOLE, _VMEM_WHOLE, _HBM, _HBM,
        ],
        out_specs=(_HBM, _HBM),
        scratch_shapes=[
            pltpu.VMEM((D_MODEL, D_FF), BF16),
            pltpu.VMEM((D_FF, D_MODEL), BF16),
            pltpu.VMEM((2, MLP_STAGE_ROWS, D_FF), F32),
            pltpu.VMEM((2, 4 * MLP_STAGE_ROWS, D_MODEL), F32),
            pltpu.VMEM((2, steps, batch, D_MODEL), F32),
            pltpu.VMEM((batch_s, D_MODEL), F32),
            pltpu.SemaphoreType.DMA((4,)),
            pltpu.SemaphoreType.DMA((2,)),
            pltpu.SemaphoreType.DMA((1,)),
        ],
        compiler_params=_COMPILER_PARAMS,
        name="mlp",
    )(h, hs, g_pre, g_post, wup, wdown)


def kernel(x_prompt, x_sample, state_s5_re, state_s5_im, state_pool, norm_mix_pre, norm_mix_post, norm_mlp_pre, norm_mlp_post, w_in, s5_lambda_re, s5_lambda_im, s5_log_dt, s5_b_re, s5_b_im, s5_c_re, s5_c_im, s5_d, s5_w_glu, pool_w, pool_scale, w_out, w_mlp_up, w_mlp_down):
    bp, seq, _ = x_prompt.shape
    bs = x_sample.shape[0]

    a_re, a_im, bm, cm, glu, poolbd = _s5_prep(
        s5_lambda_re, s5_lambda_im, s5_log_dt, s5_b_re, s5_b_im, s5_c_re, s5_c_im,
        s5_w_glu, pool_w)
    row = lambda v: v.reshape(1, -1)
    seq_params = [row(norm_mix_pre), row(norm_mix_post), a_re, a_im, bm, cm, row(s5_d),
                  glu, poolbd, row(pool_scale)]

    st_in = lambda a: jnp.transpose(a, (1, 2, 0)).reshape(N_STATE, bs)
    st_out = lambda a: jnp.transpose(a.reshape(S5_GROUPS, S5_STATE, bs), (2, 0, 1))
    tbc = lambda a: jnp.transpose(a, (1, 0, 2))

    h1p, hpr, hpi, pool_p, h1s, hsr, hsi, pool_s = _mixer(
        x_prompt, x_sample, st_in(state_s5_re), st_in(state_s5_im), tbc(state_pool),
        w_in, w_out, seq_params, steps=PROMPT_STEPS, start_pos_s=PAST_LEN)
    yp, y_sample = _mlp(h1p.reshape(seq * bp, D_MODEL), h1s, row(norm_mlp_pre),
                        row(norm_mlp_post), w_mlp_up, w_mlp_down, batch=bp,
                        steps=PROMPT_STEPS)

    return (yp, y_sample, hpr, hpi, tbc(pool_p),
            st_out(hsr), st_out(hsi), tbc(pool_s))
```

```python
import functools
import math

import jax
import jax.numpy as jnp
from jax import lax
from jax.experimental import pallas as pl
from jax.experimental.pallas import tpu as pltpu

F32 = jnp.float32
BF16 = jnp.bfloat16

D_MODEL = 1024
W_A = 512
W_B = 512
S5_H = 16
S5_GROUPS = 32
S5_STATE = 64
N_STATE = S5_GROUPS * S5_STATE
N_SLAB = 4
SLAB_GROUPS = S5_GROUPS // N_SLAB
SLAB_U = SLAB_GROUPS * S5_H
SLAB_S = SLAB_GROUPS * S5_STATE
POOL_WINDOWS = (2, 4, 8, 16)
POOL_CH = 128
POOL_HIST = 16
D_FF = 4096
FF_CHUNK = 1024
EPS = 1e-6
PAST_LEN = 16384
PROMPT_STEPS = 128
GELU_C = math.sqrt(2.0 / math.pi)

VMEM_LIMIT_BYTES = 62 * 1024 * 1024


def _rms_norm(x, g):
    ms = jnp.mean(x * x, axis=-1, keepdims=True)
    return x * lax.rsqrt(ms + EPS) * g


def _full_spec(shape):
    return pl.BlockSpec(shape, lambda *_: (0,) * len(shape))


def _s5_prep_kernel(lam_re_ref, lam_im_ref, log_dt_ref, b_re_ref, b_im_ref,
                    c_re_ref, c_im_ref, wglu_ref, poolw_ref,
                    a_re_ref, a_im_ref, bm_ref, cm_ref, glu_ref, poolbd_ref):
    lam_re = lam_re_ref[...]
    lam_im = lam_im_ref[...]
    eye = (lax.broadcasted_iota(jnp.int32, (S5_GROUPS, S5_GROUPS), 0)
           == lax.broadcasted_iota(jnp.int32, (S5_GROUPS, S5_GROUPS), 1))
    log_dt = jnp.sum(jnp.where(eye, log_dt_ref[...], 0.0), axis=1, keepdims=True)
    dt = jnp.exp(log_dt)
    mag = jnp.exp(lam_re * dt)
    ang = lam_im * dt
    a_re = mag * jnp.cos(ang)
    a_im = mag * jnp.sin(ang)
    lanes = lambda m: jnp.concatenate([m[g:g + 1, :] for g in range(S5_GROUPS)], axis=1)
    a_re_ref[...] = lanes(a_re)
    a_im_ref[...] = lanes(a_im)
    n_re = a_re - 1.0
    n_im = a_im
    den = lam_re * lam_re + lam_im * lam_im
    k_re = (n_re * lam_re + n_im * lam_im) / den
    k_im = (n_im * lam_re - n_re * lam_im) / den
    per_h = lambda m: jnp.broadcast_to(m[:, None, :], (S5_GROUPS, S5_H, S5_STATE)).reshape(
        S5_GROUPS * S5_H, S5_STATE)
    k_re = per_h(k_re)
    k_im = per_h(k_im)
    b_re = b_re_ref[...]
    b_im = b_im_ref[...]
    bb_re = k_re * b_re - k_im * b_im
    bb_im = k_re * b_im + k_im * b_re
    c_re = c_re_ref[...]
    c_im_neg = -c_im_ref[...]

    rows = lax.broadcasted_iota(jnp.int32, (SLAB_U, SLAB_S), 0) // S5_H
    cols = lax.broadcasted_iota(jnp.int32, (SLAB_U, SLAB_S), 1) // S5_STATE
    diag = rows == cols

    def block_diag(m, j):
        sl = m[j * SLAB_U:(j + 1) * SLAB_U, :]
        tiled = jnp.concatenate([sl] * SLAB_GROUPS, axis=1)
        return jnp.where(diag, tiled, 0.0)

    for j in range(N_SLAB):
        bm_ref[j, :, :SLAB_S] = block_diag(bb_re, j).astype(BF16)
        bm_ref[j, :, SLAB_S:] = block_diag(bb_im, j).astype(BF16)
        cm_ref[j, :SLAB_S, :] = block_diag(c_re, j).T.astype(BF16)
        cm_ref[j, SLAB_S:, :] = block_diag(c_im_neg, j).T.astype(BF16)

    half = W_A // 2
    g_rows = lax.broadcasted_iota(jnp.int32, (half, half), 0) // S5_H
    g_cols = lax.broadcasted_iota(jnp.int32, (half, half), 1) // S5_H
    for t in range(2):
        blk = wglu_ref[t * half:(t + 1) * half, :]
        tiled = jnp.concatenate([blk] * (half // S5_H), axis=1)
        glu_ref[t] = jnp.where(g_rows == g_cols, tiled, 0.0).astype(BF16)

    zeros = jnp.zeros((POOL_CH, POOL_CH), BF16)
    for t in range(2):
        poolbd_ref[t, :POOL_CH, :POOL_CH] = poolw_ref[2 * t].astype(BF16)
        poolbd_ref[t, :POOL_CH, POOL_CH:] = zeros
        poolbd_ref[t, POOL_CH:, :POOL_CH] = zeros
        poolbd_ref[t, POOL_CH:, POOL_CH:] = poolw_ref[2 * t + 1].astype(BF16)


def _s5_prep(lam_re, lam_im, log_dt, b_re, b_im, c_re, c_im, w_glu, pool_w):
    gh_p = lambda b: jnp.transpose(b, (0, 2, 1)).reshape(S5_GROUPS * S5_H, S5_STATE)
    ins = (lam_re, lam_im, log_dt.reshape(1, S5_GROUPS), gh_p(b_re), gh_p(b_im),
           c_re.reshape(S5_GROUPS * S5_H, S5_STATE),
           c_im.reshape(S5_GROUPS * S5_H, S5_STATE),
           w_glu.reshape(S5_GROUPS * S5_H, S5_H), pool_w)
    out_shape = (
        jax.ShapeDtypeStruct((1, N_STATE), F32),
        jax.ShapeDtypeStruct((1, N_STATE), F32),
        jax.ShapeDtypeStruct((N_SLAB, SLAB_U, 2 * SLAB_S), BF16),
        jax.ShapeDtypeStruct((N_SLAB, 2 * SLAB_S, SLAB_U), BF16),
        jax.ShapeDtypeStruct((2, W_A // 2, W_A // 2), BF16),
        jax.ShapeDtypeStruct((2, 2 * POOL_CH, 2 * POOL_CH), BF16),
    )
    return pl.pallas_call(
        _s5_prep_kernel,
        out_shape=out_shape,
        name="s5_prep",
    )(*ins)


def _mixer_math(x, i, g_pre_ref, g_post_ref, win_ref, a_re_ref, a_im_ref, bm_ref,
                cm_ref, d_ref, glu_ref, poolw_ref, pscale_ref, wout_ref,
                slab_ref, hsbf_ref, hr_s, hi_s, hist_s, *, batch, steps, start_pos):
    rows = batch * steps
    xn = _rms_norm(x, g_pre_ref[...]).astype(BF16)
    w_in = win_ref[...].reshape(D_MODEL, D_MODEL)
    ua = jnp.dot(xn, w_in[:, :W_A], preferred_element_type=F32)
    ua_bf = ua.astype(BF16)

    pair = max(1, 16 // batch)

    def project_in(j):
        slab_ref[j % 2] = jnp.dot(ua_bf[:, j * SLAB_U:(j + 1) * SLAB_U], bm_ref[j],
                                  preferred_element_type=F32)

    def scan(j):
        st = slice(j * SLAB_S, (j + 1) * SLAB_S)
        sb = slab_ref.at[j % 2]
        hb = hsbf_ref.at[j % 2]
        ar = jnp.broadcast_to(a_re_ref[:, st], (batch, SLAB_S))
        ai = jnp.broadcast_to(a_im_ref[:, st], (batch, SLAB_S))
        hr = hr_s[:, st]
        hi = hi_s[:, st]
        for t0 in range(0, steps, pair):
            res, ims = [], []
            for t in range(t0, min(t0 + pair, steps)):
                rt = slice(t * batch, (t + 1) * batch)
                nr = ar * hr - ai * hi + sb[rt, :SLAB_S]
                ni = ar * hi + ai * hr + sb[rt, SLAB_S:]
                res.append(nr)
                ims.append(ni)
                hr, hi = nr, ni
            rg = slice(t0 * batch, (t0 + len(res)) * batch)
            hb[rg, :SLAB_S] = jnp.concatenate(res, axis=0).astype(BF16)
            hb[rg, SLAB_S:] = jnp.concatenate(ims, axis=0).astype(BF16)
        hr_s[:, st] = hr
        hi_s[:, st] = hi

    def project_out(j):
        return jnp.dot(hsbf_ref[j % 2], cm_ref[j],
                       preferred_element_type=F32)

    ys = []
    project_in(0)
    scan(0)
    ub = jnp.dot(xn, w_in[:, W_A:], preferred_element_type=F32)
    for j in range(1, N_SLAB):
        project_in(j)
        ys.append(project_out(j - 1))
        scan(j)
    ys.append(project_out(N_SLAB - 1))

    y = jnp.concatenate(ys, axis=1) + d_ref[...] * ua
    y = y * (0.5 * (1.0 + jnp.tanh(GELU_C * (y + 0.044715 * (y * y * y)))))
    y_bf = y.astype(BF16)
    half = W_A // 2
    gate = jnp.concatenate(
        [jnp.dot(y_bf[:, :half], glu_ref[0], preferred_element_type=F32),
         jnp.dot(y_bf[:, half:], glu_ref[1], preferred_element_type=F32)], axis=1)
    ya = y * (1.0 / (1.0 + jnp.exp(-gate)))

    hist = hist_s[...].reshape(POOL_HIST * batch, W_B)
    ext = jnp.concatenate([hist, ub], axis=0)
    n_ext = POOL_HIST * batch + rows
    hist_s[...] = ext[n_ext - POOL_HIST * batch:, :].reshape(POOL_HIST, batch, W_B)
    t_loc = lax.broadcasted_iota(jnp.int32, (rows, 1), 0) // batch
    pos1 = t_loc + (start_pos + 1) + i * steps
    pooled = []
    for gi, w in enumerate(POOL_WINDOWS):
        s = ext[:, gi * POOL_CH:(gi + 1) * POOL_CH]
        span = 1
        while span < w:
            n = s.shape[0]
            s = s[span * batch:, :] + s[:n - span * batch, :]
            span *= 2
        win = s[s.shape[0] - rows:, :]
        count = jnp.minimum(pos1, w).astype(F32)
        pooled.append(win / count - ub[:, gi * POOL_CH:(gi + 1) * POOL_CH])
    pooled = jnp.concatenate(pooled, axis=1).astype(BF16)
    halfb = W_B // 2
    yb = jnp.concatenate(
        [jnp.dot(pooled[:, :halfb], poolw_ref[0], preferred_element_type=F32),
         jnp.dot(pooled[:, halfb:], poolw_ref[1], preferred_element_type=F32)], axis=1)
    yb = yb * pscale_ref[...]

    ycat = jnp.concatenate([ya, yb], axis=1).astype(BF16)
    mix = jnp.dot(ycat, wout_ref[...].reshape(D_MODEL, D_MODEL),
                  preferred_element_type=F32)
    return x + _rms_norm(mix, g_post_ref[...])


def _mlp_math(h, g_pre_ref, g_post_ref, wup_ref, wdown_ref):
    hn = _rms_norm(h, g_pre_ref[...]).astype(BF16)
    ffs = []
    for j in range(D_FF // FF_CHUNK):
        sl = slice(j * FF_CHUNK, (j + 1) * FF_CHUNK)
        up = jnp.dot(hn, wup_ref[:, sl], preferred_element_type=F32)
        up = jnp.maximum(up, 0.0)
        ffs.append((up * up).astype(BF16))
    acc = jnp.dot(jnp.concatenate(ffs, axis=1), wdown_ref[...], preferred_element_type=F32)
    return h + _rms_norm(acc, g_post_ref[...])


def _mixer_scratch(batch, steps):
    return [
        pltpu.VMEM((2, batch * steps, 2 * SLAB_S), F32),
        pltpu.VMEM((2, batch * steps, 2 * SLAB_S), BF16),
        pltpu.VMEM((batch, N_STATE), F32),
        pltpu.VMEM((batch, N_STATE), F32),
        pltpu.VMEM((POOL_HIST, batch, W_B), F32),
    ]


_VMEM_WHOLE = pl.BlockSpec(memory_space=pltpu.VMEM)
_HBM = pl.BlockSpec(memory_space=pl.ANY)
_COMPILER_PARAMS = pltpu.CompilerParams(dimension_semantics=("arbitrary",),
                                        vmem_limit_bytes=VMEM_LIMIT_BYTES)


def _cast_weights(sem, n_slots, jobs):
    ahead = max(n_slots - 1, 1)

    def copy(job, c):
        w_hbm, _, stage_ref, first_slot, first_sem, row_chunk = job
        return pltpu.make_async_copy(w_hbm.at[pl.ds(c * row_chunk, row_chunk)],
                                     stage_ref.at[first_slot + c % n_slots],
                                     sem.at[first_sem + c % n_slots])

    counts = [job[0].shape[0] // job[5] for job in jobs]
    for job, n in zip(jobs, counts):
        for c in range(min(ahead, n)):
            copy(job, c).start()
    for c in range(max(counts)):
        for job, n in zip(jobs, counts):
            if c >= n:
                continue
            _, dst_ref, stage_ref, first_slot, _, row_chunk = job
            if c + ahead < n:
                copy(job, c + ahead).start()
            copy(job, c).wait()
            dst_ref[pl.ds(c * row_chunk, row_chunk)] = (
                stage_ref[first_slot + c % n_slots].astype(BF16))


N_SEQ_PARAM = 10


def _mixer_kernel(x_hbm, xs_hbm, h0r_ref, h0i_ref, pool0_hbm, win_hbm, wout_hbm,
                  *refs, batch, steps, batch_s, start_pos_s):
    g_pre, g_post, a_re, a_im, bm, cm, d, glu, poolbd, pscale = refs[:N_SEQ_PARAM]
    (h1_hbm, hr_out, hi_out, pool_out, h1s_out, hrs_out, his_out,
     pools_hbm) = refs[N_SEQ_PARAM:N_SEQ_PARAM + 8]
    (slab, hsbf, hr_s, hi_s, hist_s, hrs_s, his_s, hists_s, xs, win_bf, wout_bf, xbuf, hbuf,
     in_sem, out_sem, s_sem, w_sem) = refs[N_SEQ_PARAM + 8:]
    params = (g_pre, g_post, win_bf, a_re, a_im, bm, cm, d, glu, poolbd, pscale, wout_bf)
    i = pl.program_id(0)
    n_chunks = pl.num_programs(0)
    slot = lax.rem(i, 2)

    def x_copy(b, chunk, sl):
        return pltpu.make_async_copy(
            x_hbm.at[b, pl.ds(chunk * steps, steps), :], xbuf.at[sl, :, b, :],
            in_sem.at[sl])

    def h_copy(chunk, sl):
        return pltpu.make_async_copy(
            hbuf.at[sl], h1_hbm.at[pl.ds(chunk * steps, steps)], out_sem.at[sl])

    xs_copy = pltpu.make_async_copy(xs_hbm.at[:, 0, :], xs, s_sem.at[0])
    pool_in = pltpu.make_async_copy(pool0_hbm, hists_s.at[pl.ds(1, POOL_HIST - 1)],
                                    s_sem.at[1])
    pool_o = pltpu.make_async_copy(hists_s.at[pl.ds(1, POOL_HIST - 1)], pools_hbm,
                                   s_sem.at[1])

    @pl.when(i == 0)
    def _():
        for b in range(batch):
            x_copy(b, 0, 0).start()
        xs_copy.start()
        pool_in.start()
        hr_s[...] = jnp.zeros_like(hr_s)
        hi_s[...] = jnp.zeros_like(hi_s)
        hist_s[...] = jnp.zeros_like(hist_s)
        _cast_weights(w_sem, 1, [(win_hbm, win_bf, hbuf, 0, 0, steps),
                                 (wout_hbm, wout_bf, hbuf, 1, 1, steps)])

    @pl.when(i + 1 < n_chunks)
    def _():
        for b in range(batch):
            x_copy(b, i + 1, 1 - slot).start()

    @pl.when(i >= 2)
    def _():
        h_copy(i - 2, slot).wait()

    for b in range(batch):
        x_copy(b, i, slot).wait()
    x = xbuf[slot].reshape(steps * batch, D_MODEL)
    h1 = _mixer_math(x, i, *params, slab, hsbf, hr_s, hi_s, hist_s,
                     batch=batch, steps=steps, start_pos=0)

    hbuf[slot] = h1.reshape(steps, batch, D_MODEL)
    h_copy(i, slot).start()

    @pl.when(i == n_chunks - 1)
    def _():
        for g in range(S5_GROUPS):
            hr_out[:, g, :] = hr_s[:, g * S5_STATE:(g + 1) * S5_STATE]
            hi_out[:, g, :] = hi_s[:, g * S5_STATE:(g + 1) * S5_STATE]
        pool_out[...] = hist_s[pl.ds(1, POOL_HIST - 1)]

        xs_copy.wait()
        pool_in.wait()
        hists_s[0] = jnp.zeros((batch_s, W_B), F32)
        hrs_s[...] = h0r_ref[...].T
        his_s[...] = h0i_ref[...].T
        h1s_out[...] = _mixer_math(
            xs[...], 0, *params, slab.at[:, pl.ds(0, batch_s), :],
            hsbf.at[:, pl.ds(0, batch_s), :], hrs_s, his_s, hists_s,
            batch=batch_s, steps=1, start_pos=start_pos_s)
        pool_o.start()
        hrs_out[...] = hrs_s[...].T
        his_out[...] = his_s[...].T
        @pl.when(i >= 1)
        def _():
            h_copy(i - 1, 1 - slot).wait()

        h_copy(i, slot).wait()
        pool_o.wait()


def _mixer(x, xs, h0r_t, h0i_t, pool0_t, w_in, w_out, seq_params, *, steps, start_pos_s):
    batch, seq, _ = x.shape
    batch_s = xs.shape[0]
    assert len(seq_params) == N_SEQ_PARAM and seq % steps == 0 and batch_s <= steps * batch
    f32 = lambda *shape: jax.ShapeDtypeStruct(shape, F32)
    w3 = lambda w: w.reshape(D_MODEL // batch, batch, D_MODEL)
    return pl.pallas_call(
        functools.partial(_mixer_kernel, batch=batch, steps=steps, batch_s=batch_s,
                          start_pos_s=start_pos_s),
        out_shape=(f32(seq, batch, D_MODEL), f32(batch, S5_GROUPS, S5_STATE),
                   f32(batch, S5_GROUPS, S5_STATE), f32(POOL_HIST - 1, batch, W_B),
                   f32(batch_s, D_MODEL), f32(N_STATE, batch_s), f32(N_STATE, batch_s),
                   f32(*pool0_t.shape)),
        grid=(seq // steps,),
        in_specs=[_HBM, _HBM, _VMEM_WHOLE, _VMEM_WHOLE, _HBM, _HBM, _HBM]
        + [_VMEM_WHOLE] * N_SEQ_PARAM,
        out_specs=(_HBM, _full_spec((batch, S5_GROUPS, S5_STATE)),
                   _full_spec((batch, S5_GROUPS, S5_STATE)),
                   _full_spec((POOL_HIST - 1, batch, W_B)),
                   _full_spec((batch_s, D_MODEL)), _full_spec((N_STATE, batch_s)),
                   _full_spec((N_STATE, batch_s)), _HBM),
        scratch_shapes=_mixer_scratch(batch, steps) + [
            pltpu.VMEM((batch_s, N_STATE), F32),
            pltpu.VMEM((batch_s, N_STATE), F32),
            pltpu.VMEM((POOL_HIST, batch_s, W_B), F32),
            pltpu.VMEM((batch_s, D_MODEL), F32),
            pltpu.VMEM((D_MODEL // batch, batch, D_MODEL), BF16),
            pltpu.VMEM((D_MODEL // batch, batch, D_MODEL), BF16),
            pltpu.VMEM((2, steps, batch, D_MODEL), F32),
            pltpu.VMEM((2, steps, batch, D_MODEL), F32),
            pltpu.SemaphoreType.DMA((2,)),
            pltpu.SemaphoreType.DMA((2,)),
            pltpu.SemaphoreType.DMA((2,)),
            pltpu.SemaphoreType.DMA((2,)),
        ],
        compiler_params=_COMPILER_PARAMS,
        name="mixer",
    )(x, xs, h0r_t, h0i_t, pool0_t, w3(w_in), w3(w_out), *seq_params)


MLP_STAGE_ROWS = 64
MLP_STAGE_SLOTS = 4


def _mlp_kernel(h_ref, hs_ref, g_pre_ref, g_post_ref, wup_hbm, wdown_hbm,
                y_hbm, ys_hbm, wup_bf, wdown_bf, stage_up, stage_down, ybuf, ys,
                w_sem, o_sem, y_sem, *, batch, steps):
    i = pl.program_id(0)
    n_chunks = pl.num_programs(0)
    slot = lax.rem(i, 2)

    def y_copy(b, chunk, sl):
        return pltpu.make_async_copy(
            ybuf.at[sl, :, b, :], y_hbm.at[b, pl.ds(chunk * steps, steps), :],
            o_sem.at[sl])

    @pl.when(i == 0)
    def _():
        _cast_weights(w_sem, MLP_STAGE_SLOTS,
                      [(wup_hbm, wup_bf, stage_up, 0, 0, MLP_STAGE_ROWS),
                       (wdown_hbm, wdown_bf, stage_down, 0, MLP_STAGE_SLOTS,
                        4 * MLP_STAGE_ROWS)])

    @pl.when(i >= 2)
    def _():
        for b in range(batch):
            y_copy(b, i - 2, slot).wait()

    y = _mlp_math(h_ref[...], g_pre_ref, g_post_ref, wup_bf, wdown_bf)
    ybuf[slot] = y.reshape(steps, batch, D_MODEL)
    for b in range(batch):
        y_copy(b, i, slot).start()

    @pl.when(i == n_chunks - 1)
    def _():
        ys[...] = _mlp_math(hs_ref[...], g_pre_ref, g_post_ref, wup_bf, wdown_bf)
        y_out = pltpu.make_async_copy(ys, ys_hbm.at[:, 0, :], y_sem.at[0])
        y_out.start()

        @pl.when(i >= 1)
        def _():
            for b in range(batch):
                y_copy(b, i - 1, 1 - slot).wait()

        for b in range(batch):
            y_copy(b, i, slot).wait()
        y_out.wait()


def _mlp(h, hs, g_pre, g_post, wup, wdown, *, batch, steps):
    block_rows = steps * batch
    n_rows = h.shape[0]
    batch_s = hs.shape[0]
    assert n_rows % block_rows == 0
    seq = n_rows // batch
    return pl.pallas_call(
        functools.partial(_mlp_kernel, batch=batch, steps=steps),
        out_shape=(jax.ShapeDtypeStruct((batch, seq, D_MODEL), F32),
                   jax.ShapeDtypeStruct((batch_s, 1, D_MODEL), F32)),
        grid=(n_rows // block_rows,),
        in_specs=[
            pl.BlockSpec((block_rows, D_MODEL), lambda i: (i, 0)),
            _VMEM_WHOLE, _VMEM_WHOLE, _VMEM_WHOLE, _HBM, _HBM,
        ],
        out_specs=(_HBM, _HBM),
        scratch_shapes=[
            pltpu.VMEM((D_MODEL, D_FF), BF16),
            pltpu.VMEM((D_FF, D_MODEL), BF16),
            pltpu.VMEM((MLP_STAGE_SLOTS, MLP_STAGE_ROWS, D_FF), F32),
            pltpu.VMEM((MLP_STAGE_SLOTS, 4 * MLP_STAGE_ROWS, D_MODEL), F32),
            pltpu.VMEM((2, steps, batch, D_MODEL), F32),
            pltpu.VMEM((batch_s, D_MODEL), F32),
            pltpu.SemaphoreType.DMA((2 * MLP_STAGE_SLOTS,)),
            pltpu.SemaphoreType.DMA((2,)),
            pltpu.SemaphoreType.DMA((1,)),
        ],
        compiler_params=_COMPILER_PARAMS,
        name="mlp",
    )(h, hs, g_pre, g_post, wup, wdown)


def kernel(x_prompt, x_sample, state_s5_re, state_s5_im, state_pool, norm_mix_pre, norm_mix_post, norm_mlp_pre, norm_mlp_post, w_in, s5_lambda_re, s5_lambda_im, s5_log_dt, s5_b_re, s5_b_im, s5_c_re, s5_c_im, s5_d, s5_w_glu, pool_w, pool_scale, w_out, w_mlp_up, w_mlp_down):
    bp, seq, _ = x_prompt.shape
    bs = x_sample.shape[0]

    a_re, a_im, bm, cm, glu, poolbd = _s5_prep(
        s5_lambda_re, s5_lambda_im, s5_log_dt, s5_b_re, s5_b_im, s5_c_re, s5_c_im,
        s5_w_glu, pool_w)
    row = lambda v: v.reshape(1, -1)
    seq_params = [row(norm_mix_pre), row(norm_mix_post), a_re, a_im, bm, cm, row(s5_d),
                  glu, poolbd, row(pool_scale)]

    st_in = lambda a: jnp.transpose(a, (1, 2, 0)).reshape(N_STATE, bs)
    st_out = lambda a: jnp.transpose(a.reshape(S5_GROUPS, S5_STATE, bs), (2, 0, 1))
    tbc = lambda a: jnp.transpose(a, (1, 0, 2))

    h1p, hpr, hpi, pool_p, h1s, hsr, hsi, pool_s = _mixer(
        x_prompt, x_sample, st_in(state_s5_re), st_in(state_s5_im), tbc(state_pool),
        w_in, w_out, seq_params, steps=PROMPT_STEPS, start_pos_s=PAST_LEN)
    yp, y_sample = _mlp(h1p.reshape(seq * bp, D_MODEL), h1s, row(norm_mlp_pre),
                        row(norm_mlp_post), w_mlp_up, w_mlp_down, batch=bp,
                        steps=PROMPT_STEPS)

    return (yp, y_sample, hpr, hpi, tbc(pool_p),
            st_out(hsr), st_out(hsi), tbc(pool_s))
```

```python
import functools
import math

import jax
import jax.numpy as jnp
from jax import lax
from jax.experimental import pallas as pl
from jax.experimental.pallas import tpu as pltpu

F32 = jnp.float32
BF16 = jnp.bfloat16

D_MODEL = 1024
W_A = 512
W_B = 512
S5_H = 16
S5_GROUPS = 32
S5_STATE = 64
N_STATE = S5_GROUPS * S5_STATE
N_SLAB = 4
SLAB_GROUPS = S5_GROUPS // N_SLAB
SLAB_U = SLAB_GROUPS * S5_H
SLAB_S = SLAB_GROUPS * S5_STATE
POOL_WINDOWS = (2, 4, 8, 16)
POOL_CH = 128
POOL_HIST = 16
D_FF = 4096
FF_CHUNK = 1024
EPS = 1e-6
PAST_LEN = 16384
PROMPT_STEPS = 128
GELU_C = math.sqrt(2.0 / math.pi)

VMEM_LIMIT_BYTES = 63 * 1024 * 1024


def _rms_norm(x, g):
    ms = jnp.mean(x * x, axis=-1, keepdims=True)
    return x * lax.rsqrt(ms + EPS) * g


def _full_spec(shape):
    return pl.BlockSpec(shape, lambda *_: (0,) * len(shape))


def _s5_prep_kernel(lam_re_ref, lam_im_ref, log_dt_ref, b_re_ref, b_im_ref,
                    c_re_ref, c_im_ref, wglu_ref, poolw_ref,
                    a_re_ref, a_im_ref, bm_ref, cm_ref, glu_ref, poolbd_ref):
    lam_re = lam_re_ref[...]
    lam_im = lam_im_ref[...]
    eye = (lax.broadcasted_iota(jnp.int32, (S5_GROUPS, S5_GROUPS), 0)
           == lax.broadcasted_iota(jnp.int32, (S5_GROUPS, S5_GROUPS), 1))
    log_dt = jnp.sum(jnp.where(eye, log_dt_ref[...], 0.0), axis=1, keepdims=True)
    dt = jnp.exp(log_dt)
    mag = jnp.exp(lam_re * dt)
    ang = lam_im * dt
    a_re = mag * jnp.cos(ang)
    a_im = mag * jnp.sin(ang)
    lanes = lambda m: jnp.concatenate([m[g:g + 1, :] for g in range(S5_GROUPS)], axis=1)
    a_re_ref[...] = lanes(a_re)
    a_im_ref[...] = lanes(a_im)
    n_re = a_re - 1.0
    n_im = a_im
    den = lam_re * lam_re + lam_im * lam_im
    k_re = (n_re * lam_re + n_im * lam_im) / den
    k_im = (n_im * lam_re - n_re * lam_im) / den
    per_h = lambda m: jnp.broadcast_to(m[:, None, :], (S5_GROUPS, S5_H, S5_STATE)).reshape(
        S5_GROUPS * S5_H, S5_STATE)
    k_re = per_h(k_re)
    k_im = per_h(k_im)
    b_re = b_re_ref[...]
    b_im = b_im_ref[...]
    bb_re = k_re * b_re - k_im * b_im
    bb_im = k_re * b_im + k_im * b_re
    c_re = c_re_ref[...]
    c_im_neg = -c_im_ref[...]

    rows = lax.broadcasted_iota(jnp.int32, (SLAB_U, SLAB_S), 0) // S5_H
    cols = lax.broadcasted_iota(jnp.int32, (SLAB_U, SLAB_S), 1) // S5_STATE
    diag = rows == cols

    def block_diag(m, j):
        sl = m[j * SLAB_U:(j + 1) * SLAB_U, :]
        tiled = jnp.concatenate([sl] * SLAB_GROUPS, axis=1)
        return jnp.where(diag, tiled, 0.0)

    for j in range(N_SLAB):
        bm_ref[j, :, :SLAB_S] = block_diag(bb_re, j).astype(BF16)
        bm_ref[j, :, SLAB_S:] = block_diag(bb_im, j).astype(BF16)
        cm_ref[j, :SLAB_S, :] = block_diag(c_re, j).T.astype(BF16)
        cm_ref[j, SLAB_S:, :] = block_diag(c_im_neg, j).T.astype(BF16)

    half = W_A // 2
    g_rows = lax.broadcasted_iota(jnp.int32, (half, half), 0) // S5_H
    g_cols = lax.broadcasted_iota(jnp.int32, (half, half), 1) // S5_H
    for t in range(2):
        blk = wglu_ref[t * half:(t + 1) * half, :]
        tiled = jnp.concatenate([blk] * (half // S5_H), axis=1)
        glu_ref[t] = jnp.where(g_rows == g_cols, tiled, 0.0).astype(BF16)

    zeros = jnp.zeros((POOL_CH, POOL_CH), BF16)
    for t in range(2):
        poolbd_ref[t, :POOL_CH, :POOL_CH] = poolw_ref[2 * t].astype(BF16)
        poolbd_ref[t, :POOL_CH, POOL_CH:] = zeros
        poolbd_ref[t, POOL_CH:, :POOL_CH] = zeros
        poolbd_ref[t, POOL_CH:, POOL_CH:] = poolw_ref[2 * t + 1].astype(BF16)


def _s5_prep_inputs(lam_re, lam_im, log_dt, b_re, b_im, c_re, c_im, w_glu, pool_w):
    gh_p = lambda b: jnp.transpose(b, (0, 2, 1)).reshape(S5_GROUPS * S5_H, S5_STATE)
    return (lam_re, lam_im, log_dt.reshape(1, S5_GROUPS), gh_p(b_re), gh_p(b_im),
            c_re.reshape(S5_GROUPS * S5_H, S5_STATE),
            c_im.reshape(S5_GROUPS * S5_H, S5_STATE),
            w_glu.reshape(S5_GROUPS * S5_H, S5_H), pool_w)


def _s5_prep_scratch():
    return [
        pltpu.VMEM((1, N_STATE), F32),
        pltpu.VMEM((1, N_STATE), F32),
        pltpu.VMEM((N_SLAB, SLAB_U, 2 * SLAB_S), BF16),
        pltpu.VMEM((N_SLAB, 2 * SLAB_S, SLAB_U), BF16),
        pltpu.VMEM((2, W_A // 2, W_A // 2), BF16),
        pltpu.VMEM((2, 2 * POOL_CH, 2 * POOL_CH), BF16),
    ]


def _mixer_math(x, i, g_pre_ref, g_post_ref, win_ref, a_re_ref, a_im_ref, bm_ref,
                cm_ref, d_ref, glu_ref, poolw_ref, pscale_ref, wout_ref,
                slab_ref, hsbf_ref, hr_s, hi_s, hist_s, *, batch, steps, start_pos):
    rows = batch * steps
    xn = _rms_norm(x, g_pre_ref[...]).astype(BF16)
    w_in = win_ref[...].reshape(D_MODEL, D_MODEL)
    ua = jnp.dot(xn, w_in[:, :W_A], preferred_element_type=F32)
    ua_bf = ua.astype(BF16)

    pair = max(1, 16 // batch)

    def project_in(j):
        slab_ref[j % 2] = jnp.dot(ua_bf[:, j * SLAB_U:(j + 1) * SLAB_U], bm_ref[j],
                                  preferred_element_type=F32)

    def scan(j):
        st = slice(j * SLAB_S, (j + 1) * SLAB_S)
        sb = slab_ref.at[j % 2]
        hb = hsbf_ref.at[j % 2]
        ar = jnp.broadcast_to(a_re_ref[:, st], (batch, SLAB_S))
        ai = jnp.broadcast_to(a_im_ref[:, st], (batch, SLAB_S))
        hr = hr_s[:, st]
        hi = hi_s[:, st]
        for t0 in range(0, steps, pair):
            res, ims = [], []
            for t in range(t0, min(t0 + pair, steps)):
                rt = slice(t * batch, (t + 1) * batch)
                nr = ar * hr - ai * hi + sb[rt, :SLAB_S]
                ni = ar * hi + ai * hr + sb[rt, SLAB_S:]
                res.append(nr)
                ims.append(ni)
                hr, hi = nr, ni
            rg = slice(t0 * batch, (t0 + len(res)) * batch)
            hb[rg, :SLAB_S] = jnp.concatenate(res, axis=0).astype(BF16)
            hb[rg, SLAB_S:] = jnp.concatenate(ims, axis=0).astype(BF16)
        hr_s[:, st] = hr
        hi_s[:, st] = hi

    def project_out(j):
        return jnp.dot(hsbf_ref[j % 2], cm_ref[j],
                       preferred_element_type=F32)

    ys = []
    project_in(0)
    scan(0)
    ub = jnp.dot(xn, w_in[:, W_A:], preferred_element_type=F32)
    for j in range(1, N_SLAB):
        project_in(j)
        ys.append(project_out(j - 1))
        scan(j)
    ys.append(project_out(N_SLAB - 1))

    y = jnp.concatenate(ys, axis=1) + d_ref[...] * ua
    y = y * (0.5 * (1.0 + jnp.tanh(GELU_C * (y + 0.044715 * (y * y * y)))))
    y_bf = y.astype(BF16)
    half = W_A // 2
    gate = jnp.concatenate(
        [jnp.dot(y_bf[:, :half], glu_ref[0], preferred_element_type=F32),
         jnp.dot(y_bf[:, half:], glu_ref[1], preferred_element_type=F32)], axis=1)
    ya = y * (1.0 / (1.0 + jnp.exp(-gate)))

    hist = hist_s[...].reshape(POOL_HIST * batch, W_B)
    ext = jnp.concatenate([hist, ub], axis=0)
    n_ext = POOL_HIST * batch + rows
    hist_s[...] = ext[n_ext - POOL_HIST * batch:, :].reshape(POOL_HIST, batch, W_B)
    t_loc = lax.broadcasted_iota(jnp.int32, (rows, 1), 0) // batch
    pos1 = t_loc + (start_pos + 1) + i * steps
    pooled = []
    for gi, w in enumerate(POOL_WINDOWS):
        s = ext[:, gi * POOL_CH:(gi + 1) * POOL_CH]
        span = 1
        while span < w:
            n = s.shape[0]
            s = s[span * batch:, :] + s[:n - span * batch, :]
            span *= 2
        win = s[s.shape[0] - rows:, :]
        count = jnp.minimum(pos1, w).astype(F32)
        pooled.append(win / count - ub[:, gi * POOL_CH:(gi + 1) * POOL_CH])
    pooled = jnp.concatenate(pooled, axis=1).astype(BF16)
    halfb = W_B // 2
    yb = jnp.concatenate(
        [jnp.dot(pooled[:, :halfb], poolw_ref[0], preferred_element_type=F32),
         jnp.dot(pooled[:, halfb:], poolw_ref[1], preferred_element_type=F32)], axis=1)
    yb = yb * pscale_ref[...]

    ycat = jnp.concatenate([ya, yb], axis=1).astype(BF16)
    mix = jnp.dot(ycat, wout_ref[...].reshape(D_MODEL, D_MODEL),
                  preferred_element_type=F32)
    return x + _rms_norm(mix, g_post_ref[...])


def _mlp_math(h, g_pre_ref, g_post_ref, wup_ref, wdown_ref):
    hn = _rms_norm(h, g_pre_ref[...]).astype(BF16)
    ffs = []
    for j in range(D_FF // FF_CHUNK):
        sl = slice(j * FF_CHUNK, (j + 1) * FF_CHUNK)
        up = jnp.dot(hn, wup_ref[:, sl], preferred_element_type=F32)
        up = jnp.maximum(up, 0.0)
        ffs.append((up * up).astype(BF16))
    acc = jnp.dot(jnp.concatenate(ffs, axis=1), wdown_ref[...], preferred_element_type=F32)
    return h + _rms_norm(acc, g_post_ref[...])


def _mixer_scratch(batch, steps):
    return [
        pltpu.VMEM((2, batch * steps, 2 * SLAB_S), F32),
        pltpu.VMEM((2, batch * steps, 2 * SLAB_S), BF16),
        pltpu.VMEM((batch, N_STATE), F32),
        pltpu.VMEM((batch, N_STATE), F32),
        pltpu.VMEM((POOL_HIST, batch, W_B), F32),
    ]


_VMEM_WHOLE = pl.BlockSpec(memory_space=pltpu.VMEM)
_HBM = pl.BlockSpec(memory_space=pl.ANY)
_COMPILER_PARAMS = pltpu.CompilerParams(dimension_semantics=("arbitrary",),
                                        vmem_limit_bytes=VMEM_LIMIT_BYTES)


def _cast_weights(sem, n_slots, jobs):
    ahead = max(n_slots - 1, 1)

    def copy(job, c):
        w_hbm, _, stage_ref, first_slot, first_sem, row_chunk = job
        return pltpu.make_async_copy(w_hbm.at[pl.ds(c * row_chunk, row_chunk)],
                                     stage_ref.at[first_slot + c % n_slots],
                                     sem.at[first_sem + c % n_slots])

    counts = [job[0].shape[0] // job[5] for job in jobs]
    for job, n in zip(jobs, counts):
        for c in range(min(ahead, n)):
            copy(job, c).start()
    for c in range(max(counts)):
        for job, n in zip(jobs, counts):
            if c >= n:
                continue
            _, dst_ref, stage_ref, first_slot, _, row_chunk = job
            if c + ahead < n:
                copy(job, c + ahead).start()
            copy(job, c).wait()
            dst_ref[pl.ds(c * row_chunk, row_chunk)] = (
                stage_ref[first_slot + c % n_slots].astype(BF16))


N_PREP_IN = 9
N_SEQ_PARAM = 4


def _mixer_kernel(x_hbm, xs_hbm, h0r_ref, h0i_ref, pool0_hbm, win_hbm, wout_hbm,
                  *refs, batch, steps, batch_s, start_pos_s):
    prep_in = refs[:N_PREP_IN]
    n_in = N_PREP_IN + N_SEQ_PARAM
    g_pre, g_post, d, pscale = refs[N_PREP_IN:n_in]
    (h1_hbm, hr_out, hi_out, pool_out, h1s_out, hrs_out, his_out,
     pools_hbm) = refs[n_in:n_in + 8]
    (slab, hsbf, hr_s, hi_s, hist_s, hrs_s, his_s, hists_s, xs, win_bf, wout_bf, xbuf, hbuf,
     a_re, a_im, bm, cm, glu, poolbd, in_sem, out_sem, s_sem, w_sem) = refs[n_in + 8:]
    params = (g_pre, g_post, win_bf, a_re, a_im, bm, cm, d, glu, poolbd, pscale, wout_bf)
    i = pl.program_id(0)
    n_chunks = pl.num_programs(0)
    slot = lax.rem(i, 2)

    def x_copy(b, chunk, sl):
        return pltpu.make_async_copy(
            x_hbm.at[b, pl.ds(chunk * steps, steps), :], xbuf.at[sl, :, b, :],
            in_sem.at[sl])

    def h_copy(chunk, sl):
        return pltpu.make_async_copy(
            hbuf.at[sl], h1_hbm.at[pl.ds(chunk * steps, steps)], out_sem.at[sl])

    xs_copy = pltpu.make_async_copy(xs_hbm.at[:, 0, :], xs, s_sem.at[0])
    pool_in = pltpu.make_async_copy(pool0_hbm, hists_s.at[pl.ds(1, POOL_HIST - 1)],
                                    s_sem.at[1])
    pool_o = pltpu.make_async_copy(hists_s.at[pl.ds(1, POOL_HIST - 1)], pools_hbm,
                                   s_sem.at[1])

    @pl.when(i == 0)
    def _():
        for b in range(batch):
            x_copy(b, 0, 0).start()
        xs_copy.start()
        pool_in.start()
        hr_s[...] = jnp.zeros_like(hr_s)
        hi_s[...] = jnp.zeros_like(hi_s)
        hist_s[...] = jnp.zeros_like(hist_s)
        _s5_prep_kernel(*prep_in, a_re, a_im, bm, cm, glu, poolbd)
        _cast_weights(w_sem, 1, [(win_hbm, win_bf, hbuf, 0, 0, steps),
                                 (wout_hbm, wout_bf, hbuf, 1, 1, steps)])

    @pl.when(i + 1 < n_chunks)
    def _():
        for b in range(batch):
            x_copy(b, i + 1, 1 - slot).start()

    @pl.when(i >= 2)
    def _():
        h_copy(i - 2, slot).wait()

    for b in range(batch):
        x_copy(b, i, slot).wait()
    x = xbuf[slot].reshape(steps * batch, D_MODEL)
    h1 = _mixer_math(x, i, *params, slab, hsbf, hr_s, hi_s, hist_s,
                     batch=batch, steps=steps, start_pos=0)

    hbuf[slot] = h1.reshape(steps, batch, D_MODEL)
    h_copy(i, slot).start()

    @pl.when(i == n_chunks - 1)
    def _():
        for g in range(S5_GROUPS):
            hr_out[:, g, :] = hr_s[:, g * S5_STATE:(g + 1) * S5_STATE]
            hi_out[:, g, :] = hi_s[:, g * S5_STATE:(g + 1) * S5_STATE]
        pool_out[...] = hist_s[pl.ds(1, POOL_HIST - 1)]

        xs_copy.wait()
        pool_in.wait()
        hists_s[0] = jnp.zeros((batch_s, W_B), F32)
        hrs_s[...] = h0r_ref[...].T
        his_s[...] = h0i_ref[...].T
        h1s_out[...] = _mixer_math(
            xs[...], 0, *params, slab.at[:, pl.ds(0, batch_s), :],
            hsbf.at[:, pl.ds(0, batch_s), :], hrs_s, his_s, hists_s,
            batch=batch_s, steps=1, start_pos=start_pos_s)
        pool_o.start()
        hrs_out[...] = hrs_s[...].T
        his_out[...] = his_s[...].T
        @pl.when(i >= 1)
        def _():
            h_copy(i - 1, 1 - slot).wait()

        h_copy(i, slot).wait()
        pool_o.wait()


def _mixer(x, xs, h0r_t, h0i_t, pool0_t, w_in, w_out, prep_in, seq_params, *, steps,
           start_pos_s):
    batch, seq, _ = x.shape
    batch_s = xs.shape[0]
    assert len(prep_in) == N_PREP_IN and len(seq_params) == N_SEQ_PARAM
    assert seq % steps == 0 and batch_s <= steps * batch
    f32 = lambda *shape: jax.ShapeDtypeStruct(shape, F32)
    w3 = lambda w: w.reshape(D_MODEL // batch, batch, D_MODEL)
    return pl.pallas_call(
        functools.partial(_mixer_kernel, batch=batch, steps=steps, batch_s=batch_s,
                          start_pos_s=start_pos_s),
        out_shape=(f32(seq, batch, D_MODEL), f32(batch, S5_GROUPS, S5_STATE),
                   f32(batch, S5_GROUPS, S5_STATE), f32(POOL_HIST - 1, batch, W_B),
                   f32(batch_s, D_MODEL), f32(N_STATE, batch_s), f32(N_STATE, batch_s),
                   f32(*pool0_t.shape)),
        grid=(seq // steps,),
        in_specs=[_HBM, _HBM, _VMEM_WHOLE, _VMEM_WHOLE, _HBM, _HBM, _HBM]
        + [_VMEM_WHOLE] * (N_PREP_IN + N_SEQ_PARAM),
        out_specs=(_HBM, _full_spec((batch, S5_GROUPS, S5_STATE)),
                   _full_spec((batch, S5_GROUPS, S5_STATE)),
                   _full_spec((POOL_HIST - 1, batch, W_B)),
                   _full_spec((batch_s, D_MODEL)), _full_spec((N_STATE, batch_s)),
                   _full_spec((N_STATE, batch_s)), _HBM),
        scratch_shapes=_mixer_scratch(batch, steps) + [
            pltpu.VMEM((batch_s, N_STATE), F32),
            pltpu.VMEM((batch_s, N_STATE), F32),
            pltpu.VMEM((POOL_HIST, batch_s, W_B), F32),
            pltpu.VMEM((batch_s, D_MODEL), F32),
            pltpu.VMEM((D_MODEL // batch, batch, D_MODEL), BF16),
            pltpu.VMEM((D_MODEL // batch, batch, D_MODEL), BF16),
            pltpu.VMEM((2, steps, batch, D_MODEL), F32),
            pltpu.VMEM((2, steps, batch, D_MODEL), F32),
        ] + _s5_prep_scratch() + [
            pltpu.SemaphoreType.DMA((2,)),
            pltpu.SemaphoreType.DMA((2,)),
            pltpu.SemaphoreType.DMA((2,)),
            pltpu.SemaphoreType.DMA((2,)),
        ],
        compiler_params=_COMPILER_PARAMS,
        name="mixer",
    )(x, xs, h0r_t, h0i_t, pool0_t, w3(w_in), w3(w_out), *prep_in, *seq_params)


MLP_STAGE_ROWS = 64
MLP_STAGE_SLOTS = 4


def _mlp_kernel(h_ref, hs_ref, g_pre_ref, g_post_ref, wup_hbm, wdown_hbm,
                y_hbm, ys_hbm, wup_bf, wdown_bf, stage_up, stage_down, ybuf, ys,
                w_sem, o_sem, y_sem, *, batch, steps):
    i = pl.program_id(0)
    n_chunks = pl.num_programs(0)
    slot = lax.rem(i, 2)

    def y_copy(b, chunk, sl):
        return pltpu.make_async_copy(
            ybuf.at[sl, :, b, :], y_hbm.at[b, pl.ds(chunk * steps, steps), :],
            o_sem.at[sl])

    @pl.when(i == 0)
    def _():
        _cast_weights(w_sem, MLP_STAGE_SLOTS,
                      [(wup_hbm, wup_bf, stage_up, 0, 0, MLP_STAGE_ROWS),
                       (wdown_hbm, wdown_bf, stage_down, 0, MLP_STAGE_SLOTS,
                        4 * MLP_STAGE_ROWS)])

    @pl.when(i >= 2)
    def _():
        for b in range(batch):
            y_copy(b, i - 2, slot).wait()

    y = _mlp_math(h_ref[...], g_pre_ref, g_post_ref, wup_bf, wdown_bf)
    ybuf[slot] = y.reshape(steps, batch, D_MODEL)
    for b in range(batch):
        y_copy(b, i, slot).start()

    @pl.when(i == n_chunks - 1)
    def _():
        ys[...] = _mlp_math(hs_ref[...], g_pre_ref, g_post_ref, wup_bf, wdown_bf)
        y_out = pltpu.make_async_copy(ys, ys_hbm.at[:, 0, :], y_sem.at[0])
        y_out.start()

        @pl.when(i >= 1)
        def _():
            for b in range(batch):
                y_copy(b, i - 1, 1 - slot).wait()

        for b in range(batch):
            y_copy(b, i, slot).wait()
        y_out.wait()


def _mlp(h, hs, g_pre, g_post, wup, wdown, *, batch, steps):
    block_rows = steps * batch
    n_rows = h.shape[0]
    batch_s = hs.shape[0]
    assert n_rows % block_rows == 0
    seq = n_rows // batch
    return pl.pallas_call(
        functools.partial(_mlp_kernel, batch=batch, steps=steps),
        out_shape=(jax.ShapeDtypeStruct((batch, seq, D_MODEL), F32),
                   jax.ShapeDtypeStruct((batch_s, 1, D_MODEL), F32)),
        grid=(n_rows // block_rows,),
        in_specs=[
            pl.BlockSpec((block_rows, D_MODEL), lambda i: (i, 0)),
            _VMEM_WHOLE, _VMEM_WHOLE, _VMEM_WHOLE, _HBM, _HBM,
        ],
        out_specs=(_HBM, _HBM),
        scratch_shapes=[
            pltpu.VMEM((D_MODEL, D_FF), BF16),
            pltpu.VMEM((D_FF, D_MODEL), BF16),
            pltpu.VMEM((MLP_STAGE_SLOTS, MLP_STAGE_ROWS, D_FF), F32),
            pltpu.VMEM((MLP_STAGE_SLOTS, 4 * MLP_STAGE_ROWS, D_MODEL), F32),
            pltpu.VMEM((2, steps, batch, D_MODEL), F32),
            pltpu.VMEM((batch_s, D_MODEL), F32),
            pltpu.SemaphoreType.DMA((2 * MLP_STAGE_SLOTS,)),
            pltpu.SemaphoreType.DMA((2,)),
            pltpu.SemaphoreType.DMA((1,)),
        ],
        compiler_params=_COMPILER_PARAMS,
        name="mlp",
    )(h, hs, g_pre, g_post, wup, wdown)


def kernel(x_prompt, x_sample, state_s5_re, state_s5_im, state_pool, norm_mix_pre, norm_mix_post, norm_mlp_pre, norm_mlp_post, w_in, s5_lambda_re, s5_lambda_im, s5_log_dt, s5_b_re, s5_b_im, s5_c_re, s5_c_im, s5_d, s5_w_glu, pool_w, pool_scale, w_out, w_mlp_up, w_mlp_down):
    bp, seq, _ = x_prompt.shape
    bs = x_sample.shape[0]

    prep_in = _s5_prep_inputs(
        s5_lambda_re, s5_lambda_im, s5_log_dt, s5_b_re, s5_b_im, s5_c_re, s5_c_im,
        s5_w_glu, pool_w)
    row = lambda v: v.reshape(1, -1)
    seq_params = [row(norm_mix_pre), row(norm_mix_post), row(s5_d), row(pool_scale)]

    st_in = lambda a: jnp.transpose(a, (1, 2, 0)).reshape(N_STATE, bs)
    st_out = lambda a: jnp.transpose(a.reshape(S5_GROUPS, S5_STATE, bs), (2, 0, 1))
    tbc = lambda a: jnp.transpose(a, (1, 0, 2))

    h1p, hpr, hpi, pool_p, h1s, hsr, hsi, pool_s = _mixer(
        x_prompt, x_sample, st_in(state_s5_re), st_in(state_s5_im), tbc(state_pool),
        w_in, w_out, prep_in, seq_params, steps=PROMPT_STEPS, start_pos_s=PAST_LEN)
    yp, y_sample = _mlp(h1p.reshape(seq * bp, D_MODEL), h1s, row(norm_mlp_pre),
                        row(norm_mlp_post), w_mlp_up, w_mlp_down, batch=bp,
                        steps=PROMPT_STEPS)

    return (yp, y_sample, hpr, hpi, tbc(pool_p),
            st_out(hsr), st_out(hsi), tbc(pool_s))
```

```python
import functools
import math

import jax
import jax.numpy as jnp
from jax import lax
from jax.experimental import pallas as pl
from jax.experimental.pallas import tpu as pltpu

F32 = jnp.float32
BF16 = jnp.bfloat16

D_MODEL = 1024
W_A = 512
W_B = 512
S5_H = 16
S5_GROUPS = 32
S5_STATE = 64
N_STATE = S5_GROUPS * S5_STATE
N_SLAB = 4
SLAB_GROUPS = S5_GROUPS // N_SLAB
SLAB_U = SLAB_GROUPS * S5_H
SLAB_S = SLAB_GROUPS * S5_STATE
POOL_WINDOWS = (2, 4, 8, 16)
POOL_CH = 128
POOL_HIST = 16
D_FF = 4096
FF_CHUNK = 1024
EPS = 1e-6
PAST_LEN = 16384
PROMPT_STEPS = 128
GELU_C = math.sqrt(2.0 / math.pi)

VMEM_LIMIT_BYTES = 63 * 1024 * 1024


def _rms_norm(x, g):
    ms = jnp.mean(x * x, axis=-1, keepdims=True)
    return x * lax.rsqrt(ms + EPS) * g


def _full_spec(shape):
    return pl.BlockSpec(shape, lambda *_: (0,) * len(shape))


def _s5_prep_kernel(lam_re_ref, lam_im_ref, log_dt_ref, b_re_ref, b_im_ref,
                    c_re_ref, c_im_ref, wglu_ref, poolw_ref,
                    a_re_ref, a_im_ref, bm_ref, cm_ref, glu_ref, poolbd_ref):
    lam_re = lam_re_ref[...]
    lam_im = lam_im_ref[...]
    eye = (lax.broadcasted_iota(jnp.int32, (S5_GROUPS, S5_GROUPS), 0)
           == lax.broadcasted_iota(jnp.int32, (S5_GROUPS, S5_GROUPS), 1))
    log_dt = jnp.sum(jnp.where(eye, log_dt_ref[...], 0.0), axis=1, keepdims=True)
    dt = jnp.exp(log_dt)
    mag = jnp.exp(lam_re * dt)
    ang = lam_im * dt
    a_re = mag * jnp.cos(ang)
    a_im = mag * jnp.sin(ang)
    lanes = lambda m: jnp.concatenate([m[g:g + 1, :] for g in range(S5_GROUPS)], axis=1)
    a_re_ref[...] = lanes(a_re)
    a_im_ref[...] = lanes(a_im)
    n_re = a_re - 1.0
    n_im = a_im
    den = lam_re * lam_re + lam_im * lam_im
    k_re = (n_re * lam_re + n_im * lam_im) / den
    k_im = (n_im * lam_re - n_re * lam_im) / den
    per_h = lambda m: jnp.broadcast_to(m[:, None, :], (S5_GROUPS, S5_H, S5_STATE)).reshape(
        S5_GROUPS * S5_H, S5_STATE)
    k_re = per_h(k_re)
    k_im = per_h(k_im)
    b_re = b_re_ref[...]
    b_im = b_im_ref[...]
    bb_re = k_re * b_re - k_im * b_im
    bb_im = k_re * b_im + k_im * b_re
    c_re = c_re_ref[...]
    c_im_neg = -c_im_ref[...]

    rows = lax.broadcasted_iota(jnp.int32, (SLAB_U, SLAB_S), 0) // S5_H
    cols = lax.broadcasted_iota(jnp.int32, (SLAB_U, SLAB_S), 1) // S5_STATE
    diag = rows == cols

    def block_diag(m, j):
        sl = m[j * SLAB_U:(j + 1) * SLAB_U, :]
        tiled = jnp.concatenate([sl] * SLAB_GROUPS, axis=1)
        return jnp.where(diag, tiled, 0.0)

    for j in range(N_SLAB):
        bm_ref[j, :, :SLAB_S] = block_diag(bb_re, j).astype(BF16)
        bm_ref[j, :, SLAB_S:] = block_diag(bb_im, j).astype(BF16)
        cm_ref[j, :SLAB_S, :] = block_diag(c_re, j).T.astype(BF16)
        cm_ref[j, SLAB_S:, :] = block_diag(c_im_neg, j).T.astype(BF16)

    half = W_A // 2
    n_g = half // S5_H
    w_hk_g = wglu_ref[...].reshape(S5_H * S5_H, S5_GROUPS)
    w_hk_g = jnp.concatenate(
        [w_hk_g, jnp.zeros((S5_H * S5_H, 128 - S5_GROUPS), F32)], axis=1)
    w_g_hk = w_hk_g.T
    col_g = lax.broadcasted_iota(jnp.int32, (n_g, half), 1) // S5_H
    row_g = lax.broadcasted_iota(jnp.int32, (n_g, half), 0)
    sel_r = lax.broadcasted_iota(jnp.int32, (half, n_g), 0)
    sel_g = lax.broadcasted_iota(jnp.int32, (half, n_g), 1)
    for t in range(2):
        w_t = w_g_hk[t * n_g:(t + 1) * n_g, :]
        acc = jnp.zeros((half, half), F32)
        for h in range(S5_H):
            blk = w_t[:, h * S5_H:(h + 1) * S5_H]
            tiled = jnp.concatenate([blk] * n_g, axis=1)
            w_h = jnp.where(row_g == col_g, tiled, 0.0).astype(BF16)
            place = (sel_r == sel_g * S5_H + h).astype(BF16)
            acc = acc + jnp.dot(place, w_h, preferred_element_type=F32)
        glu_ref[t] = acc.astype(BF16)

    zeros = jnp.zeros((POOL_CH, POOL_CH), BF16)
    for t in range(2):
        poolbd_ref[t, :POOL_CH, :POOL_CH] = poolw_ref[2 * t].astype(BF16)
        poolbd_ref[t, :POOL_CH, POOL_CH:] = zeros
        poolbd_ref[t, POOL_CH:, :POOL_CH] = zeros
        poolbd_ref[t, POOL_CH:, POOL_CH:] = poolw_ref[2 * t + 1].astype(BF16)


def _s5_prep_inputs(lam_re, lam_im, log_dt, b_re, b_im, c_re, c_im, w_glu, pool_w):
    gh_p = lambda b: jnp.transpose(b, (0, 2, 1)).reshape(S5_GROUPS * S5_H, S5_STATE)
    return (lam_re, lam_im, log_dt.reshape(1, S5_GROUPS), gh_p(b_re), gh_p(b_im),
            c_re.reshape(S5_GROUPS * S5_H, S5_STATE),
            c_im.reshape(S5_GROUPS * S5_H, S5_STATE),
            jnp.transpose(w_glu, (1, 2, 0)), pool_w)


def _s5_prep_scratch():
    return [
        pltpu.VMEM((1, N_STATE), F32),
        pltpu.VMEM((1, N_STATE), F32),
        pltpu.VMEM((N_SLAB, SLAB_U, 2 * SLAB_S), BF16),
        pltpu.VMEM((N_SLAB, 2 * SLAB_S, SLAB_U), BF16),
        pltpu.VMEM((2, W_A // 2, W_A // 2), BF16),
        pltpu.VMEM((2, 2 * POOL_CH, 2 * POOL_CH), BF16),
    ]


def _mixer_math(x, i, g_pre_ref, g_post_ref, win_ref, a_re_ref, a_im_ref, bm_ref,
                cm_ref, d_ref, glu_ref, poolw_ref, pscale_ref, wout_ref,
                slab_ref, hsbf_ref, hr_s, hi_s, hist_s, *, batch, steps, start_pos):
    rows = batch * steps
    xn = _rms_norm(x, g_pre_ref[...]).astype(BF16)
    w_in = win_ref[...].reshape(D_MODEL, D_MODEL)
    ua = jnp.dot(xn, w_in[:, :W_A], preferred_element_type=F32)
    ua_bf = ua.astype(BF16)

    pair = max(1, 16 // batch)

    def project_in(j):
        slab_ref[j % 2] = jnp.dot(ua_bf[:, j * SLAB_U:(j + 1) * SLAB_U], bm_ref[j],
                                  preferred_element_type=F32)

    def scan(j):
        st = slice(j * SLAB_S, (j + 1) * SLAB_S)
        sb = slab_ref.at[j % 2]
        hb = hsbf_ref.at[j % 2]
        ar = jnp.broadcast_to(a_re_ref[:, st], (batch, SLAB_S))
        ai = jnp.broadcast_to(a_im_ref[:, st], (batch, SLAB_S))
        hr = hr_s[:, st]
        hi = hi_s[:, st]
        for t0 in range(0, steps, pair):
            res, ims = [], []
            for t in range(t0, min(t0 + pair, steps)):
                rt = slice(t * batch, (t + 1) * batch)
                nr = ar * hr - ai * hi + sb[rt, :SLAB_S]
                ni = ar * hi + ai * hr + sb[rt, SLAB_S:]
                res.append(nr)
                ims.append(ni)
                hr, hi = nr, ni
            rg = slice(t0 * batch, (t0 + len(res)) * batch)
            hb[rg, :SLAB_S] = jnp.concatenate(res, axis=0).astype(BF16)
            hb[rg, SLAB_S:] = jnp.concatenate(ims, axis=0).astype(BF16)
        hr_s[:, st] = hr
        hi_s[:, st] = hi

    def project_out(j):
        return jnp.dot(hsbf_ref[j % 2], cm_ref[j],
                       preferred_element_type=F32)

    ys = []
    project_in(0)
    scan(0)
    ub = jnp.dot(xn, w_in[:, W_A:], preferred_element_type=F32)
    for j in range(1, N_SLAB):
        project_in(j)
        ys.append(project_out(j - 1))
        scan(j)
    ys.append(project_out(N_SLAB - 1))

    y = jnp.concatenate(ys, axis=1) + d_ref[...] * ua
    y = y * (0.5 * (1.0 + jnp.tanh(GELU_C * (y + 0.044715 * (y * y * y)))))
    y_bf = y.astype(BF16)
    half = W_A // 2
    gate = jnp.concatenate(
        [jnp.dot(y_bf[:, :half], glu_ref[0], preferred_element_type=F32),
         jnp.dot(y_bf[:, half:], glu_ref[1], preferred_element_type=F32)], axis=1)
    ya = y * (1.0 / (1.0 + jnp.exp(-gate)))

    hist = hist_s[...].reshape(POOL_HIST * batch, W_B)
    ext = jnp.concatenate([hist, ub], axis=0)
    n_ext = POOL_HIST * batch + rows
    hist_s[...] = ext[n_ext - POOL_HIST * batch:, :].reshape(POOL_HIST, batch, W_B)
    t_loc = lax.broadcasted_iota(jnp.int32, (rows, 1), 0) // batch
    pos1 = t_loc + (start_pos + 1) + i * steps
    pooled = []
    for gi, w in enumerate(POOL_WINDOWS):
        s = ext[:, gi * POOL_CH:(gi + 1) * POOL_CH]
        span = 1
        while span < w:
            n = s.shape[0]
            s = s[span * batch:, :] + s[:n - span * batch, :]
            span *= 2
        win = s[s.shape[0] - rows:, :]
        count = jnp.minimum(pos1, w).astype(F32)
        pooled.append(win / count - ub[:, gi * POOL_CH:(gi + 1) * POOL_CH])
    pooled = jnp.concatenate(pooled, axis=1).astype(BF16)
    halfb = W_B // 2
    yb = jnp.concatenate(
        [jnp.dot(pooled[:, :halfb], poolw_ref[0], preferred_element_type=F32),
         jnp.dot(pooled[:, halfb:], poolw_ref[1], preferred_element_type=F32)], axis=1)
    yb = yb * pscale_ref[...]

    ycat = jnp.concatenate([ya, yb], axis=1).astype(BF16)
    mix = jnp.dot(ycat, wout_ref[...].reshape(D_MODEL, D_MODEL),
                  preferred_element_type=F32)
    return x + _rms_norm(mix, g_post_ref[...])


def _mlp_math(h, g_pre_ref, g_post_ref, wup_ref, wdown_ref):
    hn = _rms_norm(h, g_pre_ref[...]).astype(BF16)
    ffs = []
    for j in range(D_FF // FF_CHUNK):
        sl = slice(j * FF_CHUNK, (j + 1) * FF_CHUNK)
        up = jnp.dot(hn, wup_ref[:, sl], preferred_element_type=F32)
        up = jnp.maximum(up, 0.0)
        ffs.append((up * up).astype(BF16))
    acc = jnp.dot(jnp.concatenate(ffs, axis=1), wdown_ref[...], preferred_element_type=F32)
    return h + _rms_norm(acc, g_post_ref[...])


def _mixer_scratch(batch, steps):
    return [
        pltpu.VMEM((2, batch * steps, 2 * SLAB_S), F32),
        pltpu.VMEM((2, batch * steps, 2 * SLAB_S), BF16),
        pltpu.VMEM((batch, N_STATE), F32),
        pltpu.VMEM((batch, N_STATE), F32),
        pltpu.VMEM((POOL_HIST, batch, W_B), F32),
    ]


_VMEM_WHOLE = pl.BlockSpec(memory_space=pltpu.VMEM)
_HBM = pl.BlockSpec(memory_space=pl.ANY)
_COMPILER_PARAMS = pltpu.CompilerParams(dimension_semantics=("arbitrary",),
                                        vmem_limit_bytes=VMEM_LIMIT_BYTES)


def _cast_weights(sem, n_slots, jobs):
    ahead = max(n_slots - 1, 1)

    def copy(job, c):
        w_hbm, _, stage_ref, first_slot, first_sem, row_chunk = job
        return pltpu.make_async_copy(w_hbm.at[pl.ds(c * row_chunk, row_chunk)],
                                     stage_ref.at[first_slot + c % n_slots],
                                     sem.at[first_sem + c % n_slots])

    counts = [job[0].shape[0] // job[5] for job in jobs]
    for job, n in zip(jobs, counts):
        for c in range(min(ahead, n)):
            copy(job, c).start()
    for c in range(max(counts)):
        for job, n in zip(jobs, counts):
            if c >= n:
                continue
            _, dst_ref, stage_ref, first_slot, _, row_chunk = job
            if c + ahead < n:
                copy(job, c + ahead).start()
            copy(job, c).wait()
            dst_ref[pl.ds(c * row_chunk, row_chunk)] = (
                stage_ref[first_slot + c % n_slots].astype(BF16))


N_PREP_IN = 9
N_SEQ_PARAM = 4


def _mixer_kernel(x_hbm, xs_hbm, h0r_ref, h0i_ref, pool0_hbm, win_hbm, wout_hbm,
                  *refs, batch, steps, batch_s, start_pos_s):
    prep_in = refs[:N_PREP_IN]
    n_in = N_PREP_IN + N_SEQ_PARAM
    g_pre, g_post, d, pscale = refs[N_PREP_IN:n_in]
    (h1_hbm, hr_out, hi_out, pool_out, h1s_out, hrs_out, his_out,
     pools_hbm) = refs[n_in:n_in + 8]
    (slab, hsbf, hr_s, hi_s, hist_s, hrs_s, his_s, hists_s, xs, win_bf, wout_bf, xbuf, hbuf,
     a_re, a_im, bm, cm, glu, poolbd, in_sem, out_sem, s_sem, w_sem) = refs[n_in + 8:]
    params = (g_pre, g_post, win_bf, a_re, a_im, bm, cm, d, glu, poolbd, pscale, wout_bf)
    i = pl.program_id(0)
    n_chunks = pl.num_programs(0)
    slot = lax.rem(i, 2)

    def x_copy(b, chunk, sl):
        return pltpu.make_async_copy(
            x_hbm.at[b, pl.ds(chunk * steps, steps), :], xbuf.at[sl, :, b, :],
            in_sem.at[sl])

    def h_copy(chunk, sl):
        return pltpu.make_async_copy(
            hbuf.at[sl], h1_hbm.at[pl.ds(chunk * steps, steps)], out_sem.at[sl])

    xs_copy = pltpu.make_async_copy(xs_hbm.at[:, 0, :], xs, s_sem.at[0])
    pool_in = pltpu.make_async_copy(pool0_hbm, hists_s.at[pl.ds(1, POOL_HIST - 1)],
                                    s_sem.at[1])
    pool_o = pltpu.make_async_copy(hists_s.at[pl.ds(1, POOL_HIST - 1)], pools_hbm,
                                   s_sem.at[1])

    @pl.when(i == 0)
    def _():
        for b in range(batch):
            x_copy(b, 0, 0).start()
        xs_copy.start()
        pool_in.start()
        hr_s[...] = jnp.zeros_like(hr_s)
        hi_s[...] = jnp.zeros_like(hi_s)
        hist_s[...] = jnp.zeros_like(hist_s)
        _s5_prep_kernel(*prep_in, a_re, a_im, bm, cm, glu, poolbd)
        _cast_weights(w_sem, 1, [(win_hbm, win_bf, hbuf, 0, 0, steps),
                                 (wout_hbm, wout_bf, hbuf, 1, 1, steps)])

    @pl.when(i + 1 < n_chunks)
    def _():
        for b in range(batch):
            x_copy(b, i + 1, 1 - slot).start()

    @pl.when(i >= 2)
    def _():
        h_copy(i - 2, slot).wait()

    for b in range(batch):
        x_copy(b, i, slot).wait()
    x = xbuf[slot].reshape(steps * batch, D_MODEL)
    h1 = _mixer_math(x, i, *params, slab, hsbf, hr_s, hi_s, hist_s,
                     batch=batch, steps=steps, start_pos=0)

    hbuf[slot] = h1.reshape(steps, batch, D_MODEL)
    h_copy(i, slot).start()

    @pl.when(i == n_chunks - 1)
    def _():
        for g in range(S5_GROUPS):
            hr_out[:, g, :] = hr_s[:, g * S5_STATE:(g + 1) * S5_STATE]
            hi_out[:, g, :] = hi_s[:, g * S5_STATE:(g + 1) * S5_STATE]
        pool_out[...] = hist_s[pl.ds(1, POOL_HIST - 1)]

        xs_copy.wait()
        pool_in.wait()
        hists_s[0] = jnp.zeros((batch_s, W_B), F32)
        hrs_s[...] = h0r_ref[...].T
        his_s[...] = h0i_ref[...].T
        h1s_out[...] = _mixer_math(
            xs[...], 0, *params, slab.at[:, pl.ds(0, batch_s), :],
            hsbf.at[:, pl.ds(0, batch_s), :], hrs_s, his_s, hists_s,
            batch=batch_s, steps=1, start_pos=start_pos_s)
        pool_o.start()
        hrs_out[...] = hrs_s[...].T
        his_out[...] = his_s[...].T
        @pl.when(i >= 1)
        def _():
            h_copy(i - 1, 1 - slot).wait()

        h_copy(i, slot).wait()
        pool_o.wait()


def _mixer(x, xs, h0r_t, h0i_t, pool0_t, w_in, w_out, prep_in, seq_params, *, steps,
           start_pos_s):
    batch, seq, _ = x.shape
    batch_s = xs.shape[0]
    assert len(prep_in) == N_PREP_IN and len(seq_params) == N_SEQ_PARAM
    assert seq % steps == 0 and batch_s <= steps * batch
    f32 = lambda *shape: jax.ShapeDtypeStruct(shape, F32)
    w3 = lambda w: w.reshape(D_MODEL // batch, batch, D_MODEL)
    return pl.pallas_call(
        functools.partial(_mixer_kernel, batch=batch, steps=steps, batch_s=batch_s,
                          start_pos_s=start_pos_s),
        out_shape=(f32(seq, batch, D_MODEL), f32(batch, S5_GROUPS, S5_STATE),
                   f32(batch, S5_GROUPS, S5_STATE), f32(POOL_HIST - 1, batch, W_B),
                   f32(batch_s, D_MODEL), f32(N_STATE, batch_s), f32(N_STATE, batch_s),
                   f32(*pool0_t.shape)),
        grid=(seq // steps,),
        in_specs=[_HBM, _HBM, _VMEM_WHOLE, _VMEM_WHOLE, _HBM, _HBM, _HBM]
        + [_VMEM_WHOLE] * (N_PREP_IN + N_SEQ_PARAM),
        out_specs=(_HBM, _full_spec((batch, S5_GROUPS, S5_STATE)),
                   _full_spec((batch, S5_GROUPS, S5_STATE)),
                   _full_spec((POOL_HIST - 1, batch, W_B)),
                   _full_spec((batch_s, D_MODEL)), _full_spec((N_STATE, batch_s)),
                   _full_spec((N_STATE, batch_s)), _HBM),
        scratch_shapes=_mixer_scratch(batch, steps) + [
            pltpu.VMEM((batch_s, N_STATE), F32),
            pltpu.VMEM((batch_s, N_STATE), F32),
            pltpu.VMEM((POOL_HIST, batch_s, W_B), F32),
            pltpu.VMEM((batch_s, D_MODEL), F32),
            pltpu.VMEM((D_MODEL // batch, batch, D_MODEL), BF16),
            pltpu.VMEM((D_MODEL // batch, batch, D_MODEL), BF16),
            pltpu.VMEM((2, steps, batch, D_MODEL), F32),
            pltpu.VMEM((2, steps, batch, D_MODEL), F32),
        ] + _s5_prep_scratch() + [
            pltpu.SemaphoreType.DMA((2,)),
            pltpu.SemaphoreType.DMA((2,)),
            pltpu.SemaphoreType.DMA((2,)),
            pltpu.SemaphoreType.DMA((2,)),
        ],
        compiler_params=_COMPILER_PARAMS,
        name="mixer",
    )(x, xs, h0r_t, h0i_t, pool0_t, w3(w_in), w3(w_out), *prep_in, *seq_params)


MLP_STAGE_ROWS = 64
MLP_STAGE_SLOTS = 4


def _mlp_kernel(h_ref, hs_ref, g_pre_ref, g_post_ref, wup_hbm, wdown_hbm,
                y_hbm, ys_hbm, wup_bf, wdown_bf, stage_up, stage_down, ybuf, ys,
                w_sem, o_sem, y_sem, *, batch, steps):
    i = pl.program_id(0)
    n_chunks = pl.num_programs(0)
    slot = lax.rem(i, 2)

    def y_copy(b, chunk, sl):
        return pltpu.make_async_copy(
            ybuf.at[sl, :, b, :], y_hbm.at[b, pl.ds(chunk * steps, steps), :],
            o_sem.at[sl])

    @pl.when(i == 0)
    def _():
        _cast_weights(w_sem, MLP_STAGE_SLOTS,
                      [(wup_hbm, wup_bf, stage_up, 0, 0, MLP_STAGE_ROWS),
                       (wdown_hbm, wdown_bf, stage_down, 0, MLP_STAGE_SLOTS,
                        4 * MLP_STAGE_ROWS)])

    @pl.when(i >= 2)
    def _():
        for b in range(batch):
            y_copy(b, i - 2, slot).wait()

    y = _mlp_math(h_ref[...], g_pre_ref, g_post_ref, wup_bf, wdown_bf)
    ybuf[slot] = y.reshape(steps, batch, D_MODEL)
    for b in range(batch):
        y_copy(b, i, slot).start()

    @pl.when(i == n_chunks - 1)
    def _():
        ys[...] = _mlp_math(hs_ref[...], g_pre_ref, g_post_ref, wup_bf, wdown_bf)
        y_out = pltpu.make_async_copy(ys, ys_hbm.at[:, 0, :], y_sem.at[0])
        y_out.start()

        @pl.when(i >= 1)
        def _():
            for b in range(batch):
                y_copy(b, i - 1, 1 - slot).wait()

        for b in range(batch):
            y_copy(b, i, slot).wait()
        y_out.wait()


def _mlp(h, hs, g_pre, g_post, wup, wdown, *, batch, steps):
    block_rows = steps * batch
    n_rows = h.shape[0]
    batch_s = hs.shape[0]
    assert n_rows % block_rows == 0
    seq = n_rows // batch
    return pl.pallas_call(
        functools.partial(_mlp_kernel, batch=batch, steps=steps),
        out_shape=(jax.ShapeDtypeStruct((batch, seq, D_MODEL), F32),
                   jax.ShapeDtypeStruct((batch_s, 1, D_MODEL), F32)),
        grid=(n_rows // block_rows,),
        in_specs=[
            pl.BlockSpec((block_rows, D_MODEL), lambda i: (i, 0)),
            _VMEM_WHOLE, _VMEM_WHOLE, _VMEM_WHOLE, _HBM, _HBM,
        ],
        out_specs=(_HBM, _HBM),
        scratch_shapes=[
            pltpu.VMEM((D_MODEL, D_FF), BF16),
            pltpu.VMEM((D_FF, D_MODEL), BF16),
            pltpu.VMEM((MLP_STAGE_SLOTS, MLP_STAGE_ROWS, D_FF), F32),
            pltpu.VMEM((MLP_STAGE_SLOTS, 4 * MLP_STAGE_ROWS, D_MODEL), F32),
            pltpu.VMEM((2, steps, batch, D_MODEL), F32),
            pltpu.VMEM((batch_s, D_MODEL), F32),
            pltpu.SemaphoreType.DMA((2 * MLP_STAGE_SLOTS,)),
            pltpu.SemaphoreType.DMA((2,)),
            pltpu.SemaphoreType.DMA((1,)),
        ],
        compiler_params=_COMPILER_PARAMS,
        name="mlp",
    )(h, hs, g_pre, g_post, wup, wdown)


def kernel(x_prompt, x_sample, state_s5_re, state_s5_im, state_pool, norm_mix_pre, norm_mix_post, norm_mlp_pre, norm_mlp_post, w_in, s5_lambda_re, s5_lambda_im, s5_log_dt, s5_b_re, s5_b_im, s5_c_re, s5_c_im, s5_d, s5_w_glu, pool_w, pool_scale, w_out, w_mlp_up, w_mlp_down):
    bp, seq, _ = x_prompt.shape
    bs = x_sample.shape[0]

    prep_in = _s5_prep_inputs(
        s5_lambda_re, s5_lambda_im, s5_log_dt, s5_b_re, s5_b_im, s5_c_re, s5_c_im,
        s5_w_glu, pool_w)
    row = lambda v: v.reshape(1, -1)
    seq_params = [row(norm_mix_pre), row(norm_mix_post), row(s5_d), row(pool_scale)]

    st_in = lambda a: jnp.transpose(a, (1, 2, 0)).reshape(N_STATE, bs)
    st_out = lambda a: jnp.transpose(a.reshape(S5_GROUPS, S5_STATE, bs), (2, 0, 1))
    tbc = lambda a: jnp.transpose(a, (1, 0, 2))

    h1p, hpr, hpi, pool_p, h1s, hsr, hsi, pool_s = _mixer(
        x_prompt, x_sample, st_in(state_s5_re), st_in(state_s5_im), tbc(state_pool),
        w_in, w_out, prep_in, seq_params, steps=PROMPT_STEPS, start_pos_s=PAST_LEN)
    yp, y_sample = _mlp(h1p.reshape(seq * bp, D_MODEL), h1s, row(norm_mlp_pre),
                        row(norm_mlp_post), w_mlp_up, w_mlp_down, batch=bp,
                        steps=PROMPT_STEPS)

    return (yp, y_sample, hpr, hpi, tbc(pool_p),
            st_out(hsr), st_out(hsi), tbc(pool_s))
```

```python
import functools
import math

import jax
import jax.numpy as jnp
from jax import lax
from jax.experimental import pallas as pl
from jax.experimental.pallas import tpu as pltpu

F32 = jnp.float32
BF16 = jnp.bfloat16

D_MODEL = 1024
W_A = 512
W_B = 512
S5_H = 16
S5_GROUPS = 32
S5_STATE = 64
N_STATE = S5_GROUPS * S5_STATE
N_SLAB = 4
SLAB_GROUPS = S5_GROUPS // N_SLAB
SLAB_U = SLAB_GROUPS * S5_H
SLAB_S = SLAB_GROUPS * S5_STATE
POOL_WINDOWS = (2, 4, 8, 16)
POOL_CH = 128
POOL_HIST = 16
D_FF = 4096
FF_CHUNK = 1024
EPS = 1e-6
PAST_LEN = 16384
PROMPT_STEPS = 128
GELU_C = math.sqrt(2.0 / math.pi)

VMEM_LIMIT_BYTES = 63 * 1024 * 1024


def _rms_norm(x, g):
    ms = jnp.mean(x * x, axis=-1, keepdims=True)
    return x * lax.rsqrt(ms + EPS) * g


def _full_spec(shape):
    return pl.BlockSpec(shape, lambda *_: (0,) * len(shape))


def _s5_prep_kernel(lam_re_ref, lam_im_ref, log_dt_ref, b_re_ref, b_im_ref,
                    c_re_ref, c_im_ref, wglu_ref, poolw_ref,
                    a_re_ref, a_im_ref, bm_ref, cm_ref, glu_ref, poolbd_ref):
    lam_re = lam_re_ref[...]
    lam_im = lam_im_ref[...]
    eye = (lax.broadcasted_iota(jnp.int32, (S5_GROUPS, S5_GROUPS), 0)
           == lax.broadcasted_iota(jnp.int32, (S5_GROUPS, S5_GROUPS), 1))
    log_dt = jnp.sum(jnp.where(eye, log_dt_ref[...], 0.0), axis=1, keepdims=True)
    dt = jnp.exp(log_dt)
    mag = jnp.exp(lam_re * dt)
    ang = lam_im * dt
    a_re = mag * jnp.cos(ang)
    a_im = mag * jnp.sin(ang)
    lanes = lambda m: jnp.concatenate([m[g:g + 1, :] for g in range(S5_GROUPS)], axis=1)
    a_re_ref[...] = lanes(a_re)
    a_im_ref[...] = lanes(a_im)
    n_re = a_re - 1.0
    n_im = a_im
    den = lam_re * lam_re + lam_im * lam_im
    k_re = (n_re * lam_re + n_im * lam_im) / den
    k_im = (n_im * lam_re - n_re * lam_im) / den
    per_h = lambda m: jnp.broadcast_to(m[:, None, :], (S5_GROUPS, S5_H, S5_STATE)).reshape(
        S5_GROUPS * S5_H, S5_STATE)
    k_re = per_h(k_re)
    k_im = per_h(k_im)
    b_re = b_re_ref[...]
    b_im = b_im_ref[...]
    bb_re = k_re * b_re - k_im * b_im
    bb_im = k_re * b_im + k_im * b_re
    c_re = c_re_ref[...]
    c_im_neg = -c_im_ref[...]

    rows = lax.broadcasted_iota(jnp.int32, (SLAB_U, SLAB_S), 0) // S5_H
    cols = lax.broadcasted_iota(jnp.int32, (SLAB_U, SLAB_S), 1) // S5_STATE
    diag = rows == cols

    def block_diag(m, j):
        sl = m[j * SLAB_U:(j + 1) * SLAB_U, :]
        tiled = jnp.concatenate([sl] * SLAB_GROUPS, axis=1)
        return jnp.where(diag, tiled, 0.0)

    for j in range(N_SLAB):
        bm_ref[j, :, :SLAB_S] = block_diag(bb_re, j).astype(BF16)
        bm_ref[j, :, SLAB_S:] = block_diag(bb_im, j).astype(BF16)
        cm_ref[j, :SLAB_S, :] = block_diag(c_re, j).T.astype(BF16)
        cm_ref[j, SLAB_S:, :] = block_diag(c_im_neg, j).T.astype(BF16)

    half = W_A // 2
    n_g = half // S5_H
    w_hk_g = wglu_ref[...].reshape(S5_H * S5_H, S5_GROUPS)
    w_hk_g = jnp.concatenate(
        [w_hk_g, jnp.zeros((S5_H * S5_H, 128 - S5_GROUPS), F32)], axis=1)
    w_g_hk = w_hk_g.T
    col_g = lax.broadcasted_iota(jnp.int32, (n_g, half), 1) // S5_H
    row_g = lax.broadcasted_iota(jnp.int32, (n_g, half), 0)
    sel_r = lax.broadcasted_iota(jnp.int32, (half, n_g), 0)
    sel_g = lax.broadcasted_iota(jnp.int32, (half, n_g), 1)
    for t in range(2):
        w_t = w_g_hk[t * n_g:(t + 1) * n_g, :]
        acc = jnp.zeros((half, half), F32)
        for h in range(S5_H):
            blk = w_t[:, h * S5_H:(h + 1) * S5_H]
            tiled = jnp.concatenate([blk] * n_g, axis=1)
            w_h = jnp.where(row_g == col_g, tiled, 0.0).astype(BF16)
            place = (sel_r == sel_g * S5_H + h).astype(BF16)
            acc = acc + jnp.dot(place, w_h, preferred_element_type=F32)
        glu_ref[t] = acc.astype(BF16)

    zeros = jnp.zeros((POOL_CH, POOL_CH), BF16)
    for t in range(2):
        poolbd_ref[t, :POOL_CH, :POOL_CH] = poolw_ref[2 * t].astype(BF16)
        poolbd_ref[t, :POOL_CH, POOL_CH:] = zeros
        poolbd_ref[t, POOL_CH:, :POOL_CH] = zeros
        poolbd_ref[t, POOL_CH:, POOL_CH:] = poolw_ref[2 * t + 1].astype(BF16)


def _s5_prep_inputs(lam_re, lam_im, log_dt, b_re, b_im, c_re, c_im, w_glu, pool_w):
    gh_p = lambda b: jnp.transpose(b, (0, 2, 1)).reshape(S5_GROUPS * S5_H, S5_STATE)
    return (lam_re, lam_im, log_dt.reshape(1, S5_GROUPS), gh_p(b_re), gh_p(b_im),
            c_re.reshape(S5_GROUPS * S5_H, S5_STATE),
            c_im.reshape(S5_GROUPS * S5_H, S5_STATE),
            jnp.transpose(w_glu, (1, 2, 0)), pool_w)


def _s5_prep_scratch():
    return [
        pltpu.VMEM((1, N_STATE), F32),
        pltpu.VMEM((1, N_STATE), F32),
        pltpu.VMEM((N_SLAB, SLAB_U, 2 * SLAB_S), BF16),
        pltpu.VMEM((N_SLAB, 2 * SLAB_S, SLAB_U), BF16),
        pltpu.VMEM((2, W_A // 2, W_A // 2), BF16),
        pltpu.VMEM((2, 2 * POOL_CH, 2 * POOL_CH), BF16),
    ]


def _mixer_math(x, i, g_pre_ref, g_post_ref, win_ref, a_re_ref, a_im_ref, bm_ref,
                cm_ref, d_ref, glu_ref, poolw_ref, pscale_ref, wout_ref,
                slab_ref, hsbf_ref, hr_s, hi_s, hist_s, *, batch, steps, start_pos):
    rows = batch * steps
    xn = _rms_norm(x, g_pre_ref[...]).astype(BF16)
    w_in = win_ref[...].reshape(D_MODEL, D_MODEL)
    ua = jnp.dot(xn, w_in[:, :W_A], preferred_element_type=F32)
    ua_bf = ua.astype(BF16)

    pair = max(1, 16 // batch)

    def project_in(j):
        slab_ref[j % 2] = jnp.dot(ua_bf[:, j * SLAB_U:(j + 1) * SLAB_U], bm_ref[j],
                                  preferred_element_type=F32)

    def scan(j):
        st = slice(j * SLAB_S, (j + 1) * SLAB_S)
        sb = slab_ref.at[j % 2]
        hb = hsbf_ref.at[j % 2]
        ar = jnp.broadcast_to(a_re_ref[:, st], (batch, SLAB_S))
        ai = jnp.broadcast_to(a_im_ref[:, st], (batch, SLAB_S))
        hr = hr_s[:, st]
        hi = hi_s[:, st]
        for t0 in range(0, steps, pair):
            res, ims = [], []
            for t in range(t0, min(t0 + pair, steps)):
                rt = slice(t * batch, (t + 1) * batch)
                nr = ar * hr - ai * hi + sb[rt, :SLAB_S]
                ni = ar * hi + ai * hr + sb[rt, SLAB_S:]
                res.append(nr)
                ims.append(ni)
                hr, hi = nr, ni
            rg = slice(t0 * batch, (t0 + len(res)) * batch)
            hb[rg, :SLAB_S] = jnp.concatenate(res, axis=0).astype(BF16)
            hb[rg, SLAB_S:] = jnp.concatenate(ims, axis=0).astype(BF16)
        hr_s[:, st] = hr
        hi_s[:, st] = hi

    def project_out(j):
        return jnp.dot(hsbf_ref[j % 2], cm_ref[j],
                       preferred_element_type=F32)

    ys = []
    project_in(0)
    scan(0)
    ub = jnp.dot(xn, w_in[:, W_A:], preferred_element_type=F32)
    for j in range(1, N_SLAB):
        project_in(j)
        ys.append(project_out(j - 1))
        scan(j)
    ys.append(project_out(N_SLAB - 1))

    y = jnp.concatenate(ys, axis=1) + d_ref[...] * ua
    y = y * (0.5 * (1.0 + jnp.tanh(GELU_C * (y + 0.044715 * (y * y * y)))))
    y_bf = y.astype(BF16)
    half = W_A // 2
    gate = jnp.concatenate(
        [jnp.dot(y_bf[:, :half], glu_ref[0], preferred_element_type=F32),
         jnp.dot(y_bf[:, half:], glu_ref[1], preferred_element_type=F32)], axis=1)
    ya = y * (1.0 / (1.0 + jnp.exp(-gate)))

    hist = hist_s[...].reshape(POOL_HIST * batch, W_B)
    ext = jnp.concatenate([hist, ub], axis=0)
    n_ext = POOL_HIST * batch + rows
    hist_s[...] = ext[n_ext - POOL_HIST * batch:, :].reshape(POOL_HIST, batch, W_B)
    t_loc = lax.broadcasted_iota(jnp.int32, (rows, 1), 0) // batch
    pos1 = t_loc + (start_pos + 1) + i * steps
    pooled = []
    for gi, w in enumerate(POOL_WINDOWS):
        s = ext[:, gi * POOL_CH:(gi + 1) * POOL_CH]
        span = 1
        while span < w:
            n = s.shape[0]
            s = s[span * batch:, :] + s[:n - span * batch, :]
            span *= 2
        win = s[s.shape[0] - rows:, :]
        count = jnp.minimum(pos1, w).astype(F32)
        pooled.append(win / count - ub[:, gi * POOL_CH:(gi + 1) * POOL_CH])
    pooled = jnp.concatenate(pooled, axis=1).astype(BF16)
    halfb = W_B // 2
    yb = jnp.concatenate(
        [jnp.dot(pooled[:, :halfb], poolw_ref[0], preferred_element_type=F32),
         jnp.dot(pooled[:, halfb:], poolw_ref[1], preferred_element_type=F32)], axis=1)
    yb = yb * pscale_ref[...]

    ycat = jnp.concatenate([ya, yb], axis=1).astype(BF16)
    mix = jnp.dot(ycat, wout_ref[...].reshape(D_MODEL, D_MODEL),
                  preferred_element_type=F32)
    return x + _rms_norm(mix, g_post_ref[...])


def _mlp_math(h, g_pre_ref, g_post_ref, wup_ref, wdown_ref):
    hn = _rms_norm(h, g_pre_ref[...]).astype(BF16)
    ffs = []
    for j in range(D_FF // FF_CHUNK):
        sl = slice(j * FF_CHUNK, (j + 1) * FF_CHUNK)
        up = jnp.dot(hn, wup_ref[:, sl], preferred_element_type=F32)
        up = jnp.maximum(up, 0.0)
        ffs.append((up * up).astype(BF16))
    acc = jnp.dot(jnp.concatenate(ffs, axis=1), wdown_ref[...], preferred_element_type=F32)
    return h + _rms_norm(acc, g_post_ref[...])


def _mixer_scratch(batch, steps):
    return [
        pltpu.VMEM((2, batch * steps, 2 * SLAB_S), F32),
        pltpu.VMEM((2, batch * steps, 2 * SLAB_S), BF16),
        pltpu.VMEM((batch, N_STATE), F32),
        pltpu.VMEM((batch, N_STATE), F32),
        pltpu.VMEM((POOL_HIST, batch, W_B), F32),
    ]


_VMEM_WHOLE = pl.BlockSpec(memory_space=pltpu.VMEM)
_HBM = pl.BlockSpec(memory_space=pl.ANY)
_COMPILER_PARAMS = pltpu.CompilerParams(dimension_semantics=("arbitrary",),
                                        vmem_limit_bytes=VMEM_LIMIT_BYTES)


def _cast_weights(sem, n_slots, jobs):
    ahead = max(n_slots - 1, 1)

    def copy(job, c):
        w_hbm, _, stage_ref, first_slot, first_sem, row_chunk = job
        return pltpu.make_async_copy(w_hbm.at[pl.ds(c * row_chunk, row_chunk)],
                                     stage_ref.at[first_slot + c % n_slots],
                                     sem.at[first_sem + c % n_slots])

    counts = [job[0].shape[0] // job[5] for job in jobs]
    for job, n in zip(jobs, counts):
        for c in range(min(ahead, n)):
            copy(job, c).start()
    for c in range(max(counts)):
        for job, n in zip(jobs, counts):
            if c >= n:
                continue
            _, dst_ref, stage_ref, first_slot, _, row_chunk = job
            if c + ahead < n:
                copy(job, c + ahead).start()
            copy(job, c).wait()
            dst_ref[pl.ds(c * row_chunk, row_chunk)] = (
                stage_ref[first_slot + c % n_slots].astype(BF16))


N_PREP_IN = 9
N_SEQ_PARAM = 4


def _mixer_kernel(x_hbm, xs_hbm, h0r_ref, h0i_ref, pool0_hbm, win_hbm, wout_hbm,
                  *refs, batch, steps, batch_s, start_pos_s):
    prep_in = refs[:N_PREP_IN]
    n_in = N_PREP_IN + N_SEQ_PARAM
    g_pre, g_post, d, pscale = refs[N_PREP_IN:n_in]
    (h1_hbm, hr_out, hi_out, pool_out, h1s_out, hrs_out, his_out,
     pools_hbm) = refs[n_in:n_in + 8]
    (slab, hsbf, hr_s, hi_s, hist_s, hrs_s, his_s, hists_s, xs, win_bf, wout_bf, xbuf, hbuf,
     a_re, a_im, bm, cm, glu, poolbd, in_sem, out_sem, s_sem, w_sem) = refs[n_in + 8:]
    params = (g_pre, g_post, win_bf, a_re, a_im, bm, cm, d, glu, poolbd, pscale, wout_bf)
    i = pl.program_id(0)
    n_chunks = pl.num_programs(0)
    slot = lax.rem(i, 2)

    def x_copy(b, chunk, sl):
        return pltpu.make_async_copy(
            x_hbm.at[b, pl.ds(chunk * steps, steps), :], xbuf.at[sl, :, b, :],
            in_sem.at[sl])

    def h_copy(chunk, sl):
        return pltpu.make_async_copy(
            hbuf.at[sl], h1_hbm.at[pl.ds(chunk * steps, steps)], out_sem.at[sl])

    xs_copy = pltpu.make_async_copy(xs_hbm.at[:, 0, :], xs, s_sem.at[0])
    pool_in = pltpu.make_async_copy(pool0_hbm, hists_s.at[pl.ds(1, POOL_HIST - 1)],
                                    s_sem.at[1])
    pool_o = pltpu.make_async_copy(hists_s.at[pl.ds(1, POOL_HIST - 1)], pools_hbm,
                                   s_sem.at[1])

    @pl.when(i == 0)
    def _():
        for b in range(batch):
            x_copy(b, 0, 0).start()
        xs_copy.start()
        pool_in.start()
        hr_s[...] = jnp.zeros_like(hr_s)
        hi_s[...] = jnp.zeros_like(hi_s)
        hist_s[...] = jnp.zeros_like(hist_s)
        _s5_prep_kernel(*prep_in, a_re, a_im, bm, cm, glu, poolbd)
        _cast_weights(w_sem, 1, [(win_hbm, win_bf, hbuf, 0, 0, steps),
                                 (wout_hbm, wout_bf, hbuf, 1, 1, steps)])

        xs_copy.wait()
        pool_in.wait()
        hists_s[0] = jnp.zeros((batch_s, W_B), F32)
        hrs_s[...] = h0r_ref[...].T
        his_s[...] = h0i_ref[...].T
        h1s_out[...] = _mixer_math(
            xs[...], 0, *params, slab.at[:, pl.ds(0, batch_s), :],
            hsbf.at[:, pl.ds(0, batch_s), :], hrs_s, his_s, hists_s,
            batch=batch_s, steps=1, start_pos=start_pos_s)
        pool_o.start()
        hrs_out[...] = hrs_s[...].T
        his_out[...] = his_s[...].T

    @pl.when(i + 1 < n_chunks)
    def _():
        for b in range(batch):
            x_copy(b, i + 1, 1 - slot).start()

    @pl.when(i >= 2)
    def _():
        h_copy(i - 2, slot).wait()

    for b in range(batch):
        x_copy(b, i, slot).wait()
    x = xbuf[slot].reshape(steps * batch, D_MODEL)
    h1 = _mixer_math(x, i, *params, slab, hsbf, hr_s, hi_s, hist_s,
                     batch=batch, steps=steps, start_pos=0)

    hbuf[slot] = h1.reshape(steps, batch, D_MODEL)
    h_copy(i, slot).start()

    @pl.when(i == n_chunks - 1)
    def _():
        for g in range(S5_GROUPS):
            hr_out[:, g, :] = hr_s[:, g * S5_STATE:(g + 1) * S5_STATE]
            hi_out[:, g, :] = hi_s[:, g * S5_STATE:(g + 1) * S5_STATE]
        pool_out[...] = hist_s[pl.ds(1, POOL_HIST - 1)]

        @pl.when(i >= 1)
        def _():
            h_copy(i - 1, 1 - slot).wait()

        h_copy(i, slot).wait()
        pool_o.wait()


def _mixer(x, xs, h0r_t, h0i_t, pool0_t, w_in, w_out, prep_in, seq_params, *, steps,
           start_pos_s):
    batch, seq, _ = x.shape
    batch_s = xs.shape[0]
    assert len(prep_in) == N_PREP_IN and len(seq_params) == N_SEQ_PARAM
    assert seq % steps == 0 and batch_s <= steps * batch
    f32 = lambda *shape: jax.ShapeDtypeStruct(shape, F32)
    w3 = lambda w: w.reshape(D_MODEL // batch, batch, D_MODEL)
    return pl.pallas_call(
        functools.partial(_mixer_kernel, batch=batch, steps=steps, batch_s=batch_s,
                          start_pos_s=start_pos_s),
        out_shape=(f32(seq, batch, D_MODEL), f32(batch, S5_GROUPS, S5_STATE),
                   f32(batch, S5_GROUPS, S5_STATE), f32(POOL_HIST - 1, batch, W_B),
                   f32(batch_s, D_MODEL), f32(N_STATE, batch_s), f32(N_STATE, batch_s),
                   f32(*pool0_t.shape)),
        grid=(seq // steps,),
        in_specs=[_HBM, _HBM, _VMEM_WHOLE, _VMEM_WHOLE, _HBM, _HBM, _HBM]
        + [_VMEM_WHOLE] * (N_PREP_IN + N_SEQ_PARAM),
        out_specs=(_HBM, _full_spec((batch, S5_GROUPS, S5_STATE)),
                   _full_spec((batch, S5_GROUPS, S5_STATE)),
                   _full_spec((POOL_HIST - 1, batch, W_B)),
                   _full_spec((batch_s, D_MODEL)), _full_spec((N_STATE, batch_s)),
                   _full_spec((N_STATE, batch_s)), _HBM),
        scratch_shapes=_mixer_scratch(batch, steps) + [
            pltpu.VMEM((batch_s, N_STATE), F32),
            pltpu.VMEM((batch_s, N_STATE), F32),
            pltpu.VMEM((POOL_HIST, batch_s, W_B), F32),
            pltpu.VMEM((batch_s, D_MODEL), F32),
            pltpu.VMEM((D_MODEL // batch, batch, D_MODEL), BF16),
            pltpu.VMEM((D_MODEL // batch, batch, D_MODEL), BF16),
            pltpu.VMEM((2, steps, batch, D_MODEL), F32),
            pltpu.VMEM((2, steps, batch, D_MODEL), F32),
        ] + _s5_prep_scratch() + [
            pltpu.SemaphoreType.DMA((2,)),
            pltpu.SemaphoreType.DMA((2,)),
            pltpu.SemaphoreType.DMA((2,)),
            pltpu.SemaphoreType.DMA((2,)),
        ],
        compiler_params=_COMPILER_PARAMS,
        name="mixer",
    )(x, xs, h0r_t, h0i_t, pool0_t, w3(w_in), w3(w_out), *prep_in, *seq_params)


MLP_STAGE_ROWS = 64
MLP_STAGE_SLOTS = 4


def _mlp_kernel(h_ref, hs_ref, g_pre_ref, g_post_ref, wup_hbm, wdown_hbm,
                y_hbm, ys_hbm, wup_bf, wdown_bf, stage_up, stage_down, ybuf, ys,
                w_sem, o_sem, y_sem, *, batch, steps):
    i = pl.program_id(0)
    n_chunks = pl.num_programs(0)
    slot = lax.rem(i, 2)

    def y_copy(b, chunk, sl):
        return pltpu.make_async_copy(
            ybuf.at[sl, :, b, :], y_hbm.at[b, pl.ds(chunk * steps, steps), :],
            o_sem.at[sl])

    @pl.when(i == 0)
    def _():
        _cast_weights(w_sem, MLP_STAGE_SLOTS,
                      [(wup_hbm, wup_bf, stage_up, 0, 0, MLP_STAGE_ROWS),
                       (wdown_hbm, wdown_bf, stage_down, 0, MLP_STAGE_SLOTS,
                        4 * MLP_STAGE_ROWS)])

    @pl.when(i >= 2)
    def _():
        for b in range(batch):
            y_copy(b, i - 2, slot).wait()

    y = _mlp_math(h_ref[...], g_pre_ref, g_post_ref, wup_bf, wdown_bf)
    ybuf[slot] = y.reshape(steps, batch, D_MODEL)
    for b in range(batch):
        y_copy(b, i, slot).start()

    @pl.when(i == n_chunks - 1)
    def _():
        ys[...] = _mlp_math(hs_ref[...], g_pre_ref, g_post_ref, wup_bf, wdown_bf)
        y_out = pltpu.make_async_copy(ys, ys_hbm.at[:, 0, :], y_sem.at[0])
        y_out.start()

        @pl.when(i >= 1)
        def _():
            for b in range(batch):
                y_copy(b, i - 1, 1 - slot).wait()

        for b in range(batch):
            y_copy(b, i, slot).wait()
        y_out.wait()


def _mlp(h, hs, g_pre, g_post, wup, wdown, *, batch, steps):
    block_rows = steps * batch
    n_rows = h.shape[0]
    batch_s = hs.shape[0]
    assert n_rows % block_rows == 0
    seq = n_rows // batch
    return pl.pallas_call(
        functools.partial(_mlp_kernel, batch=batch, steps=steps),
        out_shape=(jax.ShapeDtypeStruct((batch, seq, D_MODEL), F32),
                   jax.ShapeDtypeStruct((batch_s, 1, D_MODEL), F32)),
        grid=(n_rows // block_rows,),
        in_specs=[
            pl.BlockSpec((block_rows, D_MODEL), lambda i: (i, 0)),
            _VMEM_WHOLE, _VMEM_WHOLE, _VMEM_WHOLE, _HBM, _HBM,
        ],
        out_specs=(_HBM, _HBM),
        scratch_shapes=[
            pltpu.VMEM((D_MODEL, D_FF), BF16),
            pltpu.VMEM((D_FF, D_MODEL), BF16),
            pltpu.VMEM((MLP_STAGE_SLOTS, MLP_STAGE_ROWS, D_FF), F32),
            pltpu.VMEM((MLP_STAGE_SLOTS, 4 * MLP_STAGE_ROWS, D_MODEL), F32),
            pltpu.VMEM((2, steps, batch, D_MODEL), F32),
            pltpu.VMEM((batch_s, D_MODEL), F32),
            pltpu.SemaphoreType.DMA((2 * MLP_STAGE_SLOTS,)),
            pltpu.SemaphoreType.DMA((2,)),
            pltpu.SemaphoreType.DMA((1,)),
        ],
        compiler_params=_COMPILER_PARAMS,
        name="mlp",
    )(h, hs, g_pre, g_post, wup, wdown)


def kernel(x_prompt, x_sample, state_s5_re, state_s5_im, state_pool, norm_mix_pre, norm_mix_post, norm_mlp_pre, norm_mlp_post, w_in, s5_lambda_re, s5_lambda_im, s5_log_dt, s5_b_re, s5_b_im, s5_c_re, s5_c_im, s5_d, s5_w_glu, pool_w, pool_scale, w_out, w_mlp_up, w_mlp_down):
    bp, seq, _ = x_prompt.shape
    bs = x_sample.shape[0]

    prep_in = _s5_prep_inputs(
        s5_lambda_re, s5_lambda_im, s5_log_dt, s5_b_re, s5_b_im, s5_c_re, s5_c_im,
        s5_w_glu, pool_w)
    row = lambda v: v.reshape(1, -1)
    seq_params = [row(norm_mix_pre), row(norm_mix_post), row(s5_d), row(pool_scale)]

    st_in = lambda a: jnp.transpose(a, (1, 2, 0)).reshape(N_STATE, bs)
    st_out = lambda a: jnp.transpose(a.reshape(S5_GROUPS, S5_STATE, bs), (2, 0, 1))
    tbc = lambda a: jnp.transpose(a, (1, 0, 2))

    h1p, hpr, hpi, pool_p, h1s, hsr, hsi, pool_s = _mixer(
        x_prompt, x_sample, st_in(state_s5_re), st_in(state_s5_im), tbc(state_pool),
        w_in, w_out, prep_in, seq_params, steps=PROMPT_STEPS, start_pos_s=PAST_LEN)
    yp, y_sample = _mlp(h1p.reshape(seq * bp, D_MODEL), h1s, row(norm_mlp_pre),
                        row(norm_mlp_post), w_mlp_up, w_mlp_down, batch=bp,
                        steps=PROMPT_STEPS)

    return (yp, y_sample, hpr, hpi, tbc(pool_p),
            st_out(hsr), st_out(hsi), tbc(pool_s))
```

```python
import functools
import math

import jax
import jax.numpy as jnp
from jax import lax
from jax.experimental import pallas as pl
from jax.experimental.pallas import tpu as pltpu

F32 = jnp.float32
BF16 = jnp.bfloat16

D_MODEL = 1024
W_A = 512
W_B = 512
S5_H = 16
S5_GROUPS = 32
S5_STATE = 64
N_STATE = S5_GROUPS * S5_STATE
N_SLAB = 4
SLAB_GROUPS = S5_GROUPS // N_SLAB
SLAB_U = SLAB_GROUPS * S5_H
SLAB_S = SLAB_GROUPS * S5_STATE
POOL_WINDOWS = (2, 4, 8, 16)
POOL_CH = 128
POOL_HIST = 16
D_FF = 4096
FF_CHUNK = 1024
EPS = 1e-6
PAST_LEN = 16384
PROMPT_STEPS = 128
GELU_C = math.sqrt(2.0 / math.pi)

VMEM_LIMIT_BYTES = 63 * 1024 * 1024


def _rms_norm(x, g):
    ms = jnp.mean(x * x, axis=-1, keepdims=True)
    return x * lax.rsqrt(ms + EPS) * g


def _full_spec(shape):
    return pl.BlockSpec(shape, lambda *_: (0,) * len(shape))


def _s5_prep_kernel(lam_re_ref, lam_im_ref, log_dt_ref, b_re_ref, b_im_ref,
                    c_re_ref, c_im_ref, wglu_ref, poolw_ref,
                    a_re_ref, a_im_ref, bm_ref, cm_ref, glu_ref, poolbd_ref):
    lam_re = lam_re_ref[...]
    lam_im = lam_im_ref[...]
    eye = (lax.broadcasted_iota(jnp.int32, (S5_GROUPS, S5_GROUPS), 0)
           == lax.broadcasted_iota(jnp.int32, (S5_GROUPS, S5_GROUPS), 1))
    log_dt = jnp.sum(jnp.where(eye, log_dt_ref[...], 0.0), axis=1, keepdims=True)
    dt = jnp.exp(log_dt)
    mag = jnp.exp(lam_re * dt)
    ang = lam_im * dt
    a_re = mag * jnp.cos(ang)
    a_im = mag * jnp.sin(ang)
    lanes = lambda m: jnp.concatenate([m[g:g + 1, :] for g in range(S5_GROUPS)], axis=1)
    a_re_ref[...] = lanes(a_re)
    a_im_ref[...] = lanes(a_im)
    n_re = a_re - 1.0
    n_im = a_im
    den = lam_re * lam_re + lam_im * lam_im
    k_re = (n_re * lam_re + n_im * lam_im) / den
    k_im = (n_im * lam_re - n_re * lam_im) / den
    per_h = lambda m: jnp.broadcast_to(m[:, None, :], (S5_GROUPS, S5_H, S5_STATE)).reshape(
        S5_GROUPS * S5_H, S5_STATE)
    k_re = per_h(k_re)
    k_im = per_h(k_im)
    b_re = b_re_ref[...]
    b_im = b_im_ref[...]
    bb_re = k_re * b_re - k_im * b_im
    bb_im = k_re * b_im + k_im * b_re
    c_re = c_re_ref[...]
    c_im_neg = -c_im_ref[...]

    rows = lax.broadcasted_iota(jnp.int32, (SLAB_U, SLAB_S), 0) // S5_H
    cols = lax.broadcasted_iota(jnp.int32, (SLAB_U, SLAB_S), 1) // S5_STATE
    diag = rows == cols

    def block_diag(m, j):
        sl = m[j * SLAB_U:(j + 1) * SLAB_U, :]
        tiled = jnp.concatenate([sl] * SLAB_GROUPS, axis=1)
        return jnp.where(diag, tiled, 0.0)

    for j in range(N_SLAB):
        bm_ref[j, :, :SLAB_S] = block_diag(bb_re, j).astype(BF16)
        bm_ref[j, :, SLAB_S:] = block_diag(bb_im, j).astype(BF16)
        cm_ref[j, :SLAB_S, :] = block_diag(c_re, j).T.astype(BF16)
        cm_ref[j, SLAB_S:, :] = block_diag(c_im_neg, j).T.astype(BF16)

    half = W_A // 2
    n_g = half // S5_H
    w_hk_g = wglu_ref[...].reshape(S5_H * S5_H, S5_GROUPS)
    w_hk_g = jnp.concatenate(
        [w_hk_g, jnp.zeros((S5_H * S5_H, 128 - S5_GROUPS), F32)], axis=1)
    w_g_hk = w_hk_g.T
    col_g = lax.broadcasted_iota(jnp.int32, (n_g, half), 1) // S5_H
    row_g = lax.broadcasted_iota(jnp.int32, (n_g, half), 0)
    sel_r = lax.broadcasted_iota(jnp.int32, (half, n_g), 0)
    sel_g = lax.broadcasted_iota(jnp.int32, (half, n_g), 1)
    for t in range(2):
        w_t = w_g_hk[t * n_g:(t + 1) * n_g, :]
        acc = jnp.zeros((half, half), F32)
        for h in range(S5_H):
            blk = w_t[:, h * S5_H:(h + 1) * S5_H]
            tiled = jnp.concatenate([blk] * n_g, axis=1)
            w_h = jnp.where(row_g == col_g, tiled, 0.0).astype(BF16)
            place = (sel_r == sel_g * S5_H + h).astype(BF16)
            acc = acc + jnp.dot(place, w_h, preferred_element_type=F32)
        glu_ref[t] = acc.astype(BF16)

    zeros = jnp.zeros((POOL_CH, POOL_CH), BF16)
    for t in range(2):
        poolbd_ref[t, :POOL_CH, :POOL_CH] = poolw_ref[2 * t].astype(BF16)
        poolbd_ref[t, :POOL_CH, POOL_CH:] = zeros
        poolbd_ref[t, POOL_CH:, :POOL_CH] = zeros
        poolbd_ref[t, POOL_CH:, POOL_CH:] = poolw_ref[2 * t + 1].astype(BF16)


def _s5_prep_inputs(lam_re, lam_im, log_dt, b_re, b_im, c_re, c_im, w_glu, pool_w):
    gh_p = lambda b: jnp.transpose(b, (0, 2, 1)).reshape(S5_GROUPS * S5_H, S5_STATE)
    return (lam_re, lam_im, log_dt.reshape(1, S5_GROUPS), gh_p(b_re), gh_p(b_im),
            c_re.reshape(S5_GROUPS * S5_H, S5_STATE),
            c_im.reshape(S5_GROUPS * S5_H, S5_STATE),
            jnp.transpose(w_glu, (1, 2, 0)), pool_w)


def _s5_prep_scratch():
    return [
        pltpu.VMEM((1, N_STATE), F32),
        pltpu.VMEM((1, N_STATE), F32),
        pltpu.VMEM((N_SLAB, SLAB_U, 2 * SLAB_S), BF16),
        pltpu.VMEM((N_SLAB, 2 * SLAB_S, SLAB_U), BF16),
        pltpu.VMEM((2, W_A // 2, W_A // 2), BF16),
        pltpu.VMEM((2, 2 * POOL_CH, 2 * POOL_CH), BF16),
    ]


def _mixer_math(x, i, g_pre_ref, g_post_ref, win_ref, a_re_ref, a_im_ref, bm_ref,
                cm_ref, d_ref, glu_ref, poolw_ref, pscale_ref, wout_ref,
                slab_ref, hsbf_ref, hr_s, hi_s, hist_s, *, batch, steps, start_pos):
    rows = batch * steps
    xn = _rms_norm(x, g_pre_ref[...]).astype(BF16)
    w_in = win_ref[...].reshape(D_MODEL, D_MODEL)
    ua = jnp.dot(xn, w_in[:, :W_A], preferred_element_type=F32)
    ua_bf = ua.astype(BF16)

    pair = max(1, 16 // batch)

    def project_in(j):
        slab_ref[j % 2] = jnp.dot(ua_bf[:, j * SLAB_U:(j + 1) * SLAB_U], bm_ref[j],
                                  preferred_element_type=F32)

    def scan(j):
        st = slice(j * SLAB_S, (j + 1) * SLAB_S)
        sb = slab_ref.at[j % 2]
        hb = hsbf_ref.at[j % 2]
        ar = jnp.broadcast_to(a_re_ref[:, st], (batch, SLAB_S))
        ai = jnp.broadcast_to(a_im_ref[:, st], (batch, SLAB_S))
        hr = hr_s[:, st]
        hi = hi_s[:, st]
        for t0 in range(0, steps, pair):
            res, ims = [], []
            for t in range(t0, min(t0 + pair, steps)):
                rt = slice(t * batch, (t + 1) * batch)
                nr = ar * hr - ai * hi + sb[rt, :SLAB_S]
                ni = ar * hi + ai * hr + sb[rt, SLAB_S:]
                res.append(nr)
                ims.append(ni)
                hr, hi = nr, ni
            rg = slice(t0 * batch, (t0 + len(res)) * batch)
            hb[rg, :SLAB_S] = jnp.concatenate(res, axis=0).astype(BF16)
            hb[rg, SLAB_S:] = jnp.concatenate(ims, axis=0).astype(BF16)
        hr_s[:, st] = hr
        hi_s[:, st] = hi

    def project_out(j):
        return jnp.dot(hsbf_ref[j % 2], cm_ref[j],
                       preferred_element_type=F32)

    ys = []
    project_in(0)
    scan(0)
    ub = jnp.dot(xn, w_in[:, W_A:], preferred_element_type=F32)
    for j in range(1, N_SLAB):
        project_in(j)
        ys.append(project_out(j - 1))
        scan(j)
    ys.append(project_out(N_SLAB - 1))

    y = jnp.concatenate(ys, axis=1) + d_ref[...] * ua
    y = y * (0.5 * (1.0 + jnp.tanh(GELU_C * (y + 0.044715 * (y * y * y)))))
    y_bf = y.astype(BF16)
    half = W_A // 2
    gate = jnp.concatenate(
        [jnp.dot(y_bf[:, :half], glu_ref[0], preferred_element_type=F32),
         jnp.dot(y_bf[:, half:], glu_ref[1], preferred_element_type=F32)], axis=1)
    ya = y * (1.0 / (1.0 + jnp.exp(-gate)))

    hist = hist_s[...].reshape(POOL_HIST * batch, W_B)
    ext = jnp.concatenate([hist, ub], axis=0)
    n_ext = POOL_HIST * batch + rows
    hist_s[...] = ext[n_ext - POOL_HIST * batch:, :].reshape(POOL_HIST, batch, W_B)
    t_loc = lax.broadcasted_iota(jnp.int32, (rows, 1), 0) // batch
    pos1 = t_loc + (start_pos + 1) + i * steps
    pooled = []
    for gi, w in enumerate(POOL_WINDOWS):
        s = ext[:, gi * POOL_CH:(gi + 1) * POOL_CH]
        span = 1
        while span < w:
            n = s.shape[0]
            s = s[span * batch:, :] + s[:n - span * batch, :]
            span *= 2
        win = s[s.shape[0] - rows:, :]
        count = jnp.minimum(pos1, w).astype(F32)
        pooled.append(win / count - ub[:, gi * POOL_CH:(gi + 1) * POOL_CH])
    pooled = jnp.concatenate(pooled, axis=1).astype(BF16)
    halfb = W_B // 2
    yb = jnp.concatenate(
        [jnp.dot(pooled[:, :halfb], poolw_ref[0], preferred_element_type=F32),
         jnp.dot(pooled[:, halfb:], poolw_ref[1], preferred_element_type=F32)], axis=1)
    yb = yb * pscale_ref[...]

    ycat = jnp.concatenate([ya, yb], axis=1).astype(BF16)
    mix = jnp.dot(ycat, wout_ref[...].reshape(D_MODEL, D_MODEL),
                  preferred_element_type=F32)
    return x + _rms_norm(mix, g_post_ref[...])


def _mlp_math(h, g_pre_ref, g_post_ref, wup_ref, wdown_ref):
    hn = _rms_norm(h, g_pre_ref[...]).astype(BF16)
    ffs = []
    for j in range(D_FF // FF_CHUNK):
        sl = slice(j * FF_CHUNK, (j + 1) * FF_CHUNK)
        up = jnp.dot(hn, wup_ref[:, sl], preferred_element_type=F32)
        up = jnp.maximum(up, 0.0)
        ffs.append((up * up).astype(BF16))
    acc = jnp.dot(jnp.concatenate(ffs, axis=1), wdown_ref[...], preferred_element_type=F32)
    return h + _rms_norm(acc, g_post_ref[...])


def _mixer_scratch(batch, steps):
    return [
        pltpu.VMEM((2, batch * steps, 2 * SLAB_S), F32),
        pltpu.VMEM((2, batch * steps, 2 * SLAB_S), BF16),
        pltpu.VMEM((batch, N_STATE), F32),
        pltpu.VMEM((batch, N_STATE), F32),
        pltpu.VMEM((POOL_HIST, batch, W_B), F32),
    ]


_VMEM_WHOLE = pl.BlockSpec(memory_space=pltpu.VMEM)
_HBM = pl.BlockSpec(memory_space=pl.ANY)
_COMPILER_PARAMS = pltpu.CompilerParams(dimension_semantics=("arbitrary",),
                                        vmem_limit_bytes=VMEM_LIMIT_BYTES)


def _cast_weights(sem, n_slots, jobs, after_start=None):
    ahead = max(n_slots - 1, 1)

    def copy(job, c):
        w_hbm, _, stage_ref, first_slot, first_sem, row_chunk = job
        return pltpu.make_async_copy(w_hbm.at[pl.ds(c * row_chunk, row_chunk)],
                                     stage_ref.at[first_slot + c % n_slots],
                                     sem.at[first_sem + c % n_slots])

    counts = [job[0].shape[0] // job[5] for job in jobs]
    for job, n in zip(jobs, counts):
        for c in range(min(ahead, n)):
            copy(job, c).start()
    if after_start is not None:
        after_start()
    for c in range(max(counts)):
        for job, n in zip(jobs, counts):
            if c >= n:
                continue
            _, dst_ref, stage_ref, first_slot, _, row_chunk = job
            if c + ahead < n:
                copy(job, c + ahead).start()
            copy(job, c).wait()
            dst_ref[pl.ds(c * row_chunk, row_chunk)] = (
                stage_ref[first_slot + c % n_slots].astype(BF16))


N_PREP_IN = 9
N_SEQ_PARAM = 4


def _mixer_kernel(x_hbm, xs_hbm, h0r_ref, h0i_ref, pool0_hbm, win_hbm, wout_hbm,
                  *refs, batch, steps, batch_s, start_pos_s):
    prep_in = refs[:N_PREP_IN]
    n_in = N_PREP_IN + N_SEQ_PARAM
    g_pre, g_post, d, pscale = refs[N_PREP_IN:n_in]
    (h1_hbm, hr_out, hi_out, pool_out, h1s_out, hrs_out, his_out,
     pools_hbm) = refs[n_in:n_in + 8]
    (slab, hsbf, hr_s, hi_s, hist_s, hrs_s, his_s, hists_s, xs, win_bf, wout_bf, xbuf, hbuf,
     a_re, a_im, bm, cm, glu, poolbd, in_sem, out_sem, s_sem, w_sem) = refs[n_in + 8:]
    params = (g_pre, g_post, win_bf, a_re, a_im, bm, cm, d, glu, poolbd, pscale, wout_bf)
    i = pl.program_id(0)
    n_chunks = pl.num_programs(0)
    slot = lax.rem(i, 2)

    def x_copy(b, chunk, sl):
        return pltpu.make_async_copy(
            x_hbm.at[b, pl.ds(chunk * steps, steps), :], xbuf.at[sl, :, b, :],
            in_sem.at[sl])

    def h_copy(chunk, sl):
        return pltpu.make_async_copy(
            hbuf.at[sl], h1_hbm.at[pl.ds(chunk * steps, steps)], out_sem.at[sl])

    xs_copy = pltpu.make_async_copy(xs_hbm.at[:, 0, :], xs, s_sem.at[0])
    pool_in = pltpu.make_async_copy(pool0_hbm, hists_s.at[pl.ds(1, POOL_HIST - 1)],
                                    s_sem.at[1])
    pool_o = pltpu.make_async_copy(hists_s.at[pl.ds(1, POOL_HIST - 1)], pools_hbm,
                                   s_sem.at[1])

    @pl.when(i == 0)
    def _():
        for b in range(batch):
            x_copy(b, 0, 0).start()
        xs_copy.start()
        pool_in.start()
        hr_s[...] = jnp.zeros_like(hr_s)
        hi_s[...] = jnp.zeros_like(hi_s)
        hist_s[...] = jnp.zeros_like(hist_s)
        _cast_weights(
            w_sem, 1, [(win_hbm, win_bf, hbuf, 0, 0, steps),
                       (wout_hbm, wout_bf, hbuf, 1, 1, steps)],
            after_start=lambda: _s5_prep_kernel(*prep_in, a_re, a_im, bm, cm, glu, poolbd))

    @pl.when(i + 1 < n_chunks)
    def _():
        for b in range(batch):
            x_copy(b, i + 1, 1 - slot).start()

    @pl.when(i >= 2)
    def _():
        h_copy(i - 2, slot).wait()

    for b in range(batch):
        x_copy(b, i, slot).wait()
    x = xbuf[slot].reshape(steps * batch, D_MODEL)
    h1 = _mixer_math(x, i, *params, slab, hsbf, hr_s, hi_s, hist_s,
                     batch=batch, steps=steps, start_pos=0)

    hbuf[slot] = h1.reshape(steps, batch, D_MODEL)
    h_copy(i, slot).start()

    @pl.when(i == n_chunks - 1)
    def _():
        for g in range(S5_GROUPS):
            hr_out[:, g, :] = hr_s[:, g * S5_STATE:(g + 1) * S5_STATE]
            hi_out[:, g, :] = hi_s[:, g * S5_STATE:(g + 1) * S5_STATE]
        pool_out[...] = hist_s[pl.ds(1, POOL_HIST - 1)]

        xs_copy.wait()
        pool_in.wait()
        hists_s[0] = jnp.zeros((batch_s, W_B), F32)
        hrs_s[...] = h0r_ref[...].T
        his_s[...] = h0i_ref[...].T
        h1s_out[...] = _mixer_math(
            xs[...], 0, *params, slab.at[:, pl.ds(0, batch_s), :],
            hsbf.at[:, pl.ds(0, batch_s), :], hrs_s, his_s, hists_s,
            batch=batch_s, steps=1, start_pos=start_pos_s)
        pool_o.start()
        hrs_out[...] = hrs_s[...].T
        his_out[...] = his_s[...].T
        @pl.when(i >= 1)
        def _():
            h_copy(i - 1, 1 - slot).wait()

        h_copy(i, slot).wait()
        pool_o.wait()


def _mixer(x, xs, h0r_t, h0i_t, pool0_t, w_in, w_out, prep_in, seq_params, *, steps,
           start_pos_s):
    batch, seq, _ = x.shape
    batch_s = xs.shape[0]
    assert len(prep_in) == N_PREP_IN and len(seq_params) == N_SEQ_PARAM
    assert seq % steps == 0 and batch_s <= steps * batch
    f32 = lambda *shape: jax.ShapeDtypeStruct(shape, F32)
    w3 = lambda w: w.reshape(D_MODEL // batch, batch, D_MODEL)
    return pl.pallas_call(
        functools.partial(_mixer_kernel, batch=batch, steps=steps, batch_s=batch_s,
                          start_pos_s=start_pos_s),
        out_shape=(f32(seq, batch, D_MODEL), f32(batch, S5_GROUPS, S5_STATE),
                   f32(batch, S5_GROUPS, S5_STATE), f32(POOL_HIST - 1, batch, W_B),
                   f32(batch_s, D_MODEL), f32(N_STATE, batch_s), f32(N_STATE, batch_s),
                   f32(*pool0_t.shape)),
        grid=(seq // steps,),
        in_specs=[_HBM, _HBM, _VMEM_WHOLE, _VMEM_WHOLE, _HBM, _HBM, _HBM]
        + [_VMEM_WHOLE] * (N_PREP_IN + N_SEQ_PARAM),
        out_specs=(_HBM, _full_spec((batch, S5_GROUPS, S5_STATE)),
                   _full_spec((batch, S5_GROUPS, S5_STATE)),
                   _full_spec((POOL_HIST - 1, batch, W_B)),
                   _full_spec((batch_s, D_MODEL)), _full_spec((N_STATE, batch_s)),
                   _full_spec((N_STATE, batch_s)), _HBM),
        scratch_shapes=_mixer_scratch(batch, steps) + [
            pltpu.VMEM((batch_s, N_STATE), F32),
            pltpu.VMEM((batch_s, N_STATE), F32),
            pltpu.VMEM((POOL_HIST, batch_s, W_B), F32),
            pltpu.VMEM((batch_s, D_MODEL), F32),
            pltpu.VMEM((D_MODEL // batch, batch, D_MODEL), BF16),
            pltpu.VMEM((D_MODEL // batch, batch, D_MODEL), BF16),
            pltpu.VMEM((2, steps, batch, D_MODEL), F32),
            pltpu.VMEM((2, steps, batch, D_MODEL), F32),
        ] + _s5_prep_scratch() + [
            pltpu.SemaphoreType.DMA((2,)),
            pltpu.SemaphoreType.DMA((2,)),
            pltpu.SemaphoreType.DMA((2,)),
            pltpu.SemaphoreType.DMA((2,)),
        ],
        compiler_params=_COMPILER_PARAMS,
        name="mixer",
    )(x, xs, h0r_t, h0i_t, pool0_t, w3(w_in), w3(w_out), *prep_in, *seq_params)


MLP_STAGE_ROWS = 64
MLP_STAGE_SLOTS = 4


def _mlp_kernel(h_ref, hs_ref, g_pre_ref, g_post_ref, wup_hbm, wdown_hbm,
                y_hbm, ys_hbm, wup_bf, wdown_bf, stage_up, stage_down, ybuf, ys,
                w_sem, o_sem, y_sem, *, batch, steps):
    i = pl.program_id(0)
    n_chunks = pl.num_programs(0)
    slot = lax.rem(i, 2)

    def y_copy(b, chunk, sl):
        return pltpu.make_async_copy(
            ybuf.at[sl, :, b, :], y_hbm.at[b, pl.ds(chunk * steps, steps), :],
            o_sem.at[sl])

    @pl.when(i == 0)
    def _():
        _cast_weights(w_sem, MLP_STAGE_SLOTS,
                      [(wup_hbm, wup_bf, stage_up, 0, 0, MLP_STAGE_ROWS),
                       (wdown_hbm, wdown_bf, stage_down, 0, MLP_STAGE_SLOTS,
                        4 * MLP_STAGE_ROWS)])

    @pl.when(i >= 2)
    def _():
        for b in range(batch):
            y_copy(b, i - 2, slot).wait()

    y = _mlp_math(h_ref[...], g_pre_ref, g_post_ref, wup_bf, wdown_bf)
    ybuf[slot] = y.reshape(steps, batch, D_MODEL)
    for b in range(batch):
        y_copy(b, i, slot).start()

    @pl.when(i == n_chunks - 1)
    def _():
        ys[...] = _mlp_math(hs_ref[...], g_pre_ref, g_post_ref, wup_bf, wdown_bf)
        y_out = pltpu.make_async_copy(ys, ys_hbm.at[:, 0, :], y_sem.at[0])
        y_out.start()

        @pl.when(i >= 1)
        def _():
            for b in range(batch):
                y_copy(b, i - 1, 1 - slot).wait()

        for b in range(batch):
            y_copy(b, i, slot).wait()
        y_out.wait()


def _mlp(h, hs, g_pre, g_post, wup, wdown, *, batch, steps):
    block_rows = steps * batch
    n_rows = h.shape[0]
    batch_s = hs.shape[0]
    assert n_rows % block_rows == 0
    seq = n_rows // batch
    return pl.pallas_call(
        functools.partial(_mlp_kernel, batch=batch, steps=steps),
        out_shape=(jax.ShapeDtypeStruct((batch, seq, D_MODEL), F32),
                   jax.ShapeDtypeStruct((batch_s, 1, D_MODEL), F32)),
        grid=(n_rows // block_rows,),
        in_specs=[
            pl.BlockSpec((block_rows, D_MODEL), lambda i: (i, 0)),
            _VMEM_WHOLE, _VMEM_WHOLE, _VMEM_WHOLE, _HBM, _HBM,
        ],
        out_specs=(_HBM, _HBM),
        scratch_shapes=[
            pltpu.VMEM((D_MODEL, D_FF), BF16),
            pltpu.VMEM((D_FF, D_MODEL), BF16),
            pltpu.VMEM((MLP_STAGE_SLOTS, MLP_STAGE_ROWS, D_FF), F32),
            pltpu.VMEM((MLP_STAGE_SLOTS, 4 * MLP_STAGE_ROWS, D_MODEL), F32),
            pltpu.VMEM((2, steps, batch, D_MODEL), F32),
            pltpu.VMEM((batch_s, D_MODEL), F32),
            pltpu.SemaphoreType.DMA((2 * MLP_STAGE_SLOTS,)),
            pltpu.SemaphoreType.DMA((2,)),
            pltpu.SemaphoreType.DMA((1,)),
        ],
        compiler_params=_COMPILER_PARAMS,
        name="mlp",
    )(h, hs, g_pre, g_post, wup, wdown)


def kernel(x_prompt, x_sample, state_s5_re, state_s5_im, state_pool, norm_mix_pre, norm_mix_post, norm_mlp_pre, norm_mlp_post, w_in, s5_lambda_re, s5_lambda_im, s5_log_dt, s5_b_re, s5_b_im, s5_c_re, s5_c_im, s5_d, s5_w_glu, pool_w, pool_scale, w_out, w_mlp_up, w_mlp_down):
    bp, seq, _ = x_prompt.shape
    bs = x_sample.shape[0]

    prep_in = _s5_prep_inputs(
        s5_lambda_re, s5_lambda_im, s5_log_dt, s5_b_re, s5_b_im, s5_c_re, s5_c_im,
        s5_w_glu, pool_w)
    row = lambda v: v.reshape(1, -1)
    seq_params = [row(norm_mix_pre), row(norm_mix_post), row(s5_d), row(pool_scale)]

    st_in = lambda a: jnp.transpose(a, (1, 2, 0)).reshape(N_STATE, bs)
    st_out = lambda a: jnp.transpose(a.reshape(S5_GROUPS, S5_STATE, bs), (2, 0, 1))
    tbc = lambda a: jnp.transpose(a, (1, 0, 2))

    h1p, hpr, hpi, pool_p, h1s, hsr, hsi, pool_s = _mixer(
        x_prompt, x_sample, st_in(state_s5_re), st_in(state_s5_im), tbc(state_pool),
        w_in, w_out, prep_in, seq_params, steps=PROMPT_STEPS, start_pos_s=PAST_LEN)
    yp, y_sample = _mlp(h1p.reshape(seq * bp, D_MODEL), h1s, row(norm_mlp_pre),
                        row(norm_mlp_post), w_mlp_up, w_mlp_down, batch=bp,
                        steps=PROMPT_STEPS)

    return (yp, y_sample, hpr, hpi, tbc(pool_p),
            st_out(hsr), st_out(hsi), tbc(pool_s))
```

```python
import functools
import math

import jax
import jax.numpy as jnp
from jax import lax
from jax.experimental import pallas as pl
from jax.experimental.pallas import tpu as pltpu

F32 = jnp.float32
BF16 = jnp.bfloat16

D_MODEL = 1024
W_A = 512
W_B = 512
S5_H = 16
S5_GROUPS = 32
S5_STATE = 64
N_STATE = S5_GROUPS * S5_STATE
N_SLAB = 4
SLAB_GROUPS = S5_GROUPS // N_SLAB
SLAB_U = SLAB_GROUPS * S5_H
SLAB_S = SLAB_GROUPS * S5_STATE
POOL_WINDOWS = (2, 4, 8, 16)
POOL_CH = 128
POOL_HIST = 16
D_FF = 4096
FF_CHUNK = 1024
EPS = 1e-6
PAST_LEN = 16384
PROMPT_STEPS = 128
GELU_C = math.sqrt(2.0 / math.pi)

VMEM_LIMIT_BYTES = 63 * 1024 * 1024


def _rms_norm(x, g):
    ms = jnp.mean(x * x, axis=-1, keepdims=True)
    return x * lax.rsqrt(ms + EPS) * g


def _full_spec(shape):
    return pl.BlockSpec(shape, lambda *_: (0,) * len(shape))


def _s5_prep_kernel(lam_re_ref, lam_im_ref, log_dt_ref, b_re_ref, b_im_ref,
                    c_re_ref, c_im_ref, wglu_ref, poolw_ref,
                    a_re_ref, a_im_ref, bm_ref, cm_ref, glu_ref, poolbd_ref):
    lam_re = lam_re_ref[...]
    lam_im = lam_im_ref[...]
    eye = (lax.broadcasted_iota(jnp.int32, (S5_GROUPS, S5_GROUPS), 0)
           == lax.broadcasted_iota(jnp.int32, (S5_GROUPS, S5_GROUPS), 1))
    log_dt = jnp.sum(jnp.where(eye, log_dt_ref[...], 0.0), axis=1, keepdims=True)
    dt = jnp.exp(log_dt)
    mag = jnp.exp(lam_re * dt)
    ang = lam_im * dt
    a_re = mag * jnp.cos(ang)
    a_im = mag * jnp.sin(ang)
    lanes = lambda m: jnp.concatenate([m[g:g + 1, :] for g in range(S5_GROUPS)], axis=1)
    a_re_ref[...] = lanes(a_re)
    a_im_ref[...] = lanes(a_im)
    n_re = a_re - 1.0
    n_im = a_im
    den = lam_re * lam_re + lam_im * lam_im
    k_re = (n_re * lam_re + n_im * lam_im) / den
    k_im = (n_im * lam_re - n_re * lam_im) / den
    per_h = lambda m: jnp.broadcast_to(m[:, None, :], (S5_GROUPS, S5_H, S5_STATE)).reshape(
        S5_GROUPS * S5_H, S5_STATE)
    k_re = per_h(k_re)
    k_im = per_h(k_im)
    b_re = b_re_ref[...]
    b_im = b_im_ref[...]
    bb_re = k_re * b_re - k_im * b_im
    bb_im = k_re * b_im + k_im * b_re
    c_re = c_re_ref[...]
    c_im_neg = -c_im_ref[...]

    rows = lax.broadcasted_iota(jnp.int32, (SLAB_U, SLAB_S), 0) // S5_H
    cols = lax.broadcasted_iota(jnp.int32, (SLAB_U, SLAB_S), 1) // S5_STATE
    diag = rows == cols

    def block_diag(m, j):
        sl = m[j * SLAB_U:(j + 1) * SLAB_U, :]
        tiled = jnp.concatenate([sl] * SLAB_GROUPS, axis=1)
        return jnp.where(diag, tiled, 0.0)

    for j in range(N_SLAB):
        bm_ref[j, :, :SLAB_S] = block_diag(bb_re, j).astype(BF16)
        bm_ref[j, :, SLAB_S:] = block_diag(bb_im, j).astype(BF16)
        cm_ref[j, :SLAB_S, :] = block_diag(c_re, j).T.astype(BF16)
        cm_ref[j, SLAB_S:, :] = block_diag(c_im_neg, j).T.astype(BF16)

    half = W_A // 2
    n_g = half // S5_H
    w_hk_g = wglu_ref[...].reshape(S5_H * S5_H, S5_GROUPS)
    w_hk_g = jnp.concatenate(
        [w_hk_g, jnp.zeros((S5_H * S5_H, 128 - S5_GROUPS), F32)], axis=1)
    w_g_hk = w_hk_g.T
    col_g = lax.broadcasted_iota(jnp.int32, (n_g, half), 1) // S5_H
    row_g = lax.broadcasted_iota(jnp.int32, (n_g, half), 0)
    sel_r = lax.broadcasted_iota(jnp.int32, (half, n_g), 0)
    sel_g = lax.broadcasted_iota(jnp.int32, (half, n_g), 1)
    for t in range(2):
        w_t = w_g_hk[t * n_g:(t + 1) * n_g, :]
        acc = jnp.zeros((half, half), F32)
        for h in range(S5_H):
            blk = w_t[:, h * S5_H:(h + 1) * S5_H]
            tiled = jnp.concatenate([blk] * n_g, axis=1)
            w_h = jnp.where(row_g == col_g, tiled, 0.0).astype(BF16)
            place = (sel_r == sel_g * S5_H + h).astype(BF16)
            acc = acc + jnp.dot(place, w_h, preferred_element_type=F32)
        glu_ref[t] = acc.astype(BF16)

    zeros = jnp.zeros((POOL_CH, POOL_CH), BF16)
    for t in range(2):
        poolbd_ref[t, :POOL_CH, :POOL_CH] = poolw_ref[2 * t].astype(BF16)
        poolbd_ref[t, :POOL_CH, POOL_CH:] = zeros
        poolbd_ref[t, POOL_CH:, :POOL_CH] = zeros
        poolbd_ref[t, POOL_CH:, POOL_CH:] = poolw_ref[2 * t + 1].astype(BF16)


def _s5_prep_inputs(lam_re, lam_im, log_dt, b_re, b_im, c_re, c_im, w_glu, pool_w):
    gh_p = lambda b: jnp.transpose(b, (0, 2, 1)).reshape(S5_GROUPS * S5_H, S5_STATE)
    return (lam_re, lam_im, log_dt.reshape(1, S5_GROUPS), gh_p(b_re), gh_p(b_im),
            c_re.reshape(S5_GROUPS * S5_H, S5_STATE),
            c_im.reshape(S5_GROUPS * S5_H, S5_STATE),
            jnp.transpose(w_glu, (1, 2, 0)), pool_w)


def _s5_prep_scratch():
    return [
        pltpu.VMEM((1, N_STATE), F32),
        pltpu.VMEM((1, N_STATE), F32),
        pltpu.VMEM((N_SLAB, SLAB_U, 2 * SLAB_S), BF16),
        pltpu.VMEM((N_SLAB, 2 * SLAB_S, SLAB_U), BF16),
        pltpu.VMEM((2, W_A // 2, W_A // 2), BF16),
        pltpu.VMEM((2, 2 * POOL_CH, 2 * POOL_CH), BF16),
    ]


def _mixer_math(x, i, g_pre_ref, g_post_ref, win_ref, a_re_ref, a_im_ref, bm_ref,
                cm_ref, d_ref, glu_ref, poolw_ref, pscale_ref, wout_ref,
                slab_ref, hsbf_ref, hr_s, hi_s, hist_s, *, batch, steps, start_pos):
    rows = batch * steps
    xn = _rms_norm(x, g_pre_ref[...]).astype(BF16)
    w_in = win_ref[...].reshape(D_MODEL, D_MODEL)
    ua = jnp.dot(xn, w_in[:, :W_A], preferred_element_type=F32)
    ua_bf = ua.astype(BF16)

    pair = max(1, 16 // batch)

    def project_in(j):
        slab_ref[j % 2] = jnp.dot(ua_bf[:, j * SLAB_U:(j + 1) * SLAB_U], bm_ref[j],
                                  preferred_element_type=F32)

    def scan(j):
        st = slice(j * SLAB_S, (j + 1) * SLAB_S)
        sb = slab_ref.at[j % 2]
        hb = hsbf_ref.at[j % 2]
        ar = jnp.broadcast_to(a_re_ref[:, st], (batch, SLAB_S))
        ai = jnp.broadcast_to(a_im_ref[:, st], (batch, SLAB_S))
        hr = hr_s[:, st]
        hi = hi_s[:, st]
        for t0 in range(0, steps, pair):
            res, ims = [], []
            for t in range(t0, min(t0 + pair, steps)):
                rt = slice(t * batch, (t + 1) * batch)
                nr = ar * hr - ai * hi + sb[rt, :SLAB_S]
                ni = ar * hi + ai * hr + sb[rt, SLAB_S:]
                res.append(nr)
                ims.append(ni)
                hr, hi = nr, ni
            rg = slice(t0 * batch, (t0 + len(res)) * batch)
            hb[rg, :SLAB_S] = jnp.concatenate(res, axis=0).astype(BF16)
            hb[rg, SLAB_S:] = jnp.concatenate(ims, axis=0).astype(BF16)
        hr_s[:, st] = hr
        hi_s[:, st] = hi

    def project_out(j):
        return jnp.dot(hsbf_ref[j % 2], cm_ref[j],
                       preferred_element_type=F32)

    ys = []
    project_in(0)
    scan(0)
    ub = jnp.dot(xn, w_in[:, W_A:], preferred_element_type=F32)
    for j in range(1, N_SLAB):
        project_in(j)
        ys.append(project_out(j - 1))
        scan(j)
    ys.append(project_out(N_SLAB - 1))

    y = jnp.concatenate(ys, axis=1) + d_ref[...] * ua
    y = y * (0.5 * (1.0 + jnp.tanh(GELU_C * (y + 0.044715 * (y * y * y)))))
    y_bf = y.astype(BF16)
    half = W_A // 2
    gate = jnp.concatenate(
        [jnp.dot(y_bf[:, :half], glu_ref[0], preferred_element_type=F32),
         jnp.dot(y_bf[:, half:], glu_ref[1], preferred_element_type=F32)], axis=1)
    ya = y * (1.0 / (1.0 + jnp.exp(-gate)))

    hist = hist_s[...].reshape(POOL_HIST * batch, W_B)
    ext = jnp.concatenate([hist, ub], axis=0)
    n_ext = POOL_HIST * batch + rows
    hist_s[...] = ext[n_ext - POOL_HIST * batch:, :].reshape(POOL_HIST, batch, W_B)
    t_loc = lax.broadcasted_iota(jnp.int32, (rows, 1), 0) // batch
    pos1 = t_loc + (start_pos + 1) + i * steps
    pooled = []
    for gi, w in enumerate(POOL_WINDOWS):
        s = ext[:, gi * POOL_CH:(gi + 1) * POOL_CH]
        span = 1
        while span < w:
            n = s.shape[0]
            s = s[span * batch:, :] + s[:n - span * batch, :]
            span *= 2
        win = s[s.shape[0] - rows:, :]
        count = jnp.minimum(pos1, w).astype(F32)
        pooled.append(win / count - ub[:, gi * POOL_CH:(gi + 1) * POOL_CH])
    pooled = jnp.concatenate(pooled, axis=1).astype(BF16)
    halfb = W_B // 2
    yb = jnp.concatenate(
        [jnp.dot(pooled[:, :halfb], poolw_ref[0], preferred_element_type=F32),
         jnp.dot(pooled[:, halfb:], poolw_ref[1], preferred_element_type=F32)], axis=1)
    yb = yb * pscale_ref[...]

    ycat = jnp.concatenate([ya, yb], axis=1).astype(BF16)
    mix = jnp.dot(ycat, wout_ref[...].reshape(D_MODEL, D_MODEL),
                  preferred_element_type=F32)
    return x + _rms_norm(mix, g_post_ref[...])


def _mlp_math(h, g_pre_ref, g_post_ref, wup_ref, wdown_ref):
    hn = _rms_norm(h, g_pre_ref[...]).astype(BF16)
    ffs = []
    for j in range(D_FF // FF_CHUNK):
        sl = slice(j * FF_CHUNK, (j + 1) * FF_CHUNK)
        up = jnp.dot(hn, wup_ref[:, sl], preferred_element_type=F32)
        up = jnp.maximum(up, 0.0)
        ffs.append((up * up).astype(BF16))
    acc = jnp.dot(jnp.concatenate(ffs, axis=1), wdown_ref[...], preferred_element_type=F32)
    return h + _rms_norm(acc, g_post_ref[...])


def _mixer_scratch(batch, steps):
    return [
        pltpu.VMEM((2, batch * steps, 2 * SLAB_S), F32),
        pltpu.VMEM((2, batch * steps, 2 * SLAB_S), BF16),
        pltpu.VMEM((batch, N_STATE), F32),
        pltpu.VMEM((batch, N_STATE), F32),
        pltpu.VMEM((POOL_HIST, batch, W_B), F32),
    ]


_VMEM_WHOLE = pl.BlockSpec(memory_space=pltpu.VMEM)
_HBM = pl.BlockSpec(memory_space=pl.ANY)
_COMPILER_PARAMS = pltpu.CompilerParams(dimension_semantics=("arbitrary",),
                                        vmem_limit_bytes=VMEM_LIMIT_BYTES)


def _cast_weights(sem, n_slots, jobs):
    ahead = max(n_slots - 1, 1)

    def copy(job, c):
        w_hbm, _, stage_ref, first_slot, first_sem, row_chunk = job
        return pltpu.make_async_copy(w_hbm.at[pl.ds(c * row_chunk, row_chunk)],
                                     stage_ref.at[first_slot + c % n_slots],
                                     sem.at[first_sem + c % n_slots])

    counts = [job[0].shape[0] // job[5] for job in jobs]
    for job, n in zip(jobs, counts):
        for c in range(min(ahead, n)):
            copy(job, c).start()
    for c in range(max(counts)):
        for job, n in zip(jobs, counts):
            if c >= n:
                continue
            _, dst_ref, stage_ref, first_slot, _, row_chunk = job
            if c + ahead < n:
                copy(job, c + ahead).start()
            copy(job, c).wait()
            dst_ref[pl.ds(c * row_chunk, row_chunk)] = (
                stage_ref[first_slot + c % n_slots].astype(BF16))


X0_PARTS = 4
N_PREP_IN = 9
N_SEQ_PARAM = 4


def _mixer_kernel(x_hbm, xs_hbm, h0r_ref, h0i_ref, pool0_hbm, win_hbm, wout_hbm,
                  *refs, batch, steps, batch_s, start_pos_s):
    prep_in = refs[:N_PREP_IN]
    n_in = N_PREP_IN + N_SEQ_PARAM
    g_pre, g_post, d, pscale = refs[N_PREP_IN:n_in]
    (h1_hbm, hr_out, hi_out, pool_out, h1s_out, hrs_out, his_out,
     pools_hbm) = refs[n_in:n_in + 8]
    (slab, hsbf, hr_s, hi_s, hist_s, hrs_s, his_s, hists_s, xs, win_bf, wout_bf, xbuf, hbuf,
     a_re, a_im, bm, cm, glu, poolbd, in_sem, out_sem, s_sem, w_sem) = refs[n_in + 8:]
    params = (g_pre, g_post, win_bf, a_re, a_im, bm, cm, d, glu, poolbd, pscale, wout_bf)
    i = pl.program_id(0)
    n_chunks = pl.num_programs(0)
    slot = lax.rem(i, 2)

    def x_copy(b, chunk, sl):
        return pltpu.make_async_copy(
            x_hbm.at[b, pl.ds(chunk * steps, steps), :], xbuf.at[sl, :, b, :],
            in_sem.at[sl])

    def h_copy(chunk, sl):
        return pltpu.make_async_copy(
            hbuf.at[sl], h1_hbm.at[pl.ds(chunk * steps, steps)], out_sem.at[sl])

    xs_copy = pltpu.make_async_copy(xs_hbm.at[:, 0, :], xs, s_sem.at[0])
    pool_in = pltpu.make_async_copy(pool0_hbm, hists_s.at[pl.ds(1, POOL_HIST - 1)],
                                    s_sem.at[1])
    pool_o = pltpu.make_async_copy(hists_s.at[pl.ds(1, POOL_HIST - 1)], pools_hbm,
                                   s_sem.at[1])

    def x0_copy(b, part):
        rows = pl.ds(part * (steps // X0_PARTS), steps // X0_PARTS)
        return pltpu.make_async_copy(x_hbm.at[b, rows, :], xbuf.at[0, rows, b, :],
                                     in_sem.at[0])

    @pl.when(i == 0)
    def _():
        for part in range(X0_PARTS):
            for b in range(batch):
                x0_copy(b, part).start()
        xs_copy.start()
        pool_in.start()
        hr_s[...] = jnp.zeros_like(hr_s)
        hi_s[...] = jnp.zeros_like(hi_s)
        hist_s[...] = jnp.zeros_like(hist_s)
        _s5_prep_kernel(*prep_in, a_re, a_im, bm, cm, glu, poolbd)
        _cast_weights(w_sem, 1, [(win_hbm, win_bf, hbuf, 0, 0, steps),
                                 (wout_hbm, wout_bf, hbuf, 1, 1, steps)])
        for part in range(X0_PARTS):
            for b in range(batch):
                x0_copy(b, part).wait()

    @pl.when(i + 1 < n_chunks)
    def _():
        for b in range(batch):
            x_copy(b, i + 1, 1 - slot).start()

    @pl.when(i >= 2)
    def _():
        h_copy(i - 2, slot).wait()

    @pl.when(i >= 1)
    def _():
        for b in range(batch):
            x_copy(b, i, slot).wait()

    x = xbuf[slot].reshape(steps * batch, D_MODEL)
    h1 = _mixer_math(x, i, *params, slab, hsbf, hr_s, hi_s, hist_s,
                     batch=batch, steps=steps, start_pos=0)

    hbuf[slot] = h1.reshape(steps, batch, D_MODEL)
    h_copy(i, slot).start()

    @pl.when(i == n_chunks - 1)
    def _():
        for g in range(S5_GROUPS):
            hr_out[:, g, :] = hr_s[:, g * S5_STATE:(g + 1) * S5_STATE]
            hi_out[:, g, :] = hi_s[:, g * S5_STATE:(g + 1) * S5_STATE]
        pool_out[...] = hist_s[pl.ds(1, POOL_HIST - 1)]

        xs_copy.wait()
        pool_in.wait()
        hists_s[0] = jnp.zeros((batch_s, W_B), F32)
        hrs_s[...] = h0r_ref[...].T
        his_s[...] = h0i_ref[...].T
        h1s_out[...] = _mixer_math(
            xs[...], 0, *params, slab.at[:, pl.ds(0, batch_s), :],
            hsbf.at[:, pl.ds(0, batch_s), :], hrs_s, his_s, hists_s,
            batch=batch_s, steps=1, start_pos=start_pos_s)
        pool_o.start()
        hrs_out[...] = hrs_s[...].T
        his_out[...] = his_s[...].T
        @pl.when(i >= 1)
        def _():
            h_copy(i - 1, 1 - slot).wait()

        h_copy(i, slot).wait()
        pool_o.wait()


def _mixer(x, xs, h0r_t, h0i_t, pool0_t, w_in, w_out, prep_in, seq_params, *, steps,
           start_pos_s):
    batch, seq, _ = x.shape
    batch_s = xs.shape[0]
    assert len(prep_in) == N_PREP_IN and len(seq_params) == N_SEQ_PARAM
    assert seq % steps == 0 and batch_s <= steps * batch
    f32 = lambda *shape: jax.ShapeDtypeStruct(shape, F32)
    w3 = lambda w: w.reshape(D_MODEL // batch, batch, D_MODEL)
    return pl.pallas_call(
        functools.partial(_mixer_kernel, batch=batch, steps=steps, batch_s=batch_s,
                          start_pos_s=start_pos_s),
        out_shape=(f32(seq, batch, D_MODEL), f32(batch, S5_GROUPS, S5_STATE),
                   f32(batch, S5_GROUPS, S5_STATE), f32(POOL_HIST - 1, batch, W_B),
                   f32(batch_s, D_MODEL), f32(N_STATE, batch_s), f32(N_STATE, batch_s),
                   f32(*pool0_t.shape)),
        grid=(seq // steps,),
        in_specs=[_HBM, _HBM, _VMEM_WHOLE, _VMEM_WHOLE, _HBM, _HBM, _HBM]
        + [_VMEM_WHOLE] * (N_PREP_IN + N_SEQ_PARAM),
        out_specs=(_HBM, _full_spec((batch, S5_GROUPS, S5_STATE)),
                   _full_spec((batch, S5_GROUPS, S5_STATE)),
                   _full_spec((POOL_HIST - 1, batch, W_B)),
                   _full_spec((batch_s, D_MODEL)), _full_spec((N_STATE, batch_s)),
                   _full_spec((N_STATE, batch_s)), _HBM),
        scratch_shapes=_mixer_scratch(batch, steps) + [
            pltpu.VMEM((batch_s, N_STATE), F32),
            pltpu.VMEM((batch_s, N_STATE), F32),
            pltpu.VMEM((POOL_HIST, batch_s, W_B), F32),
            pltpu.VMEM((batch_s, D_MODEL), F32),
            pltpu.VMEM((D_MODEL // batch, batch, D_MODEL), BF16),
            pltpu.VMEM((D_MODEL // batch, batch, D_MODEL), BF16),
            pltpu.VMEM((2, steps, batch, D_MODEL), F32),
            pltpu.VMEM((2, steps, batch, D_MODEL), F32),
        ] + _s5_prep_scratch() + [
            pltpu.SemaphoreType.DMA((2,)),
            pltpu.SemaphoreType.DMA((2,)),
            pltpu.SemaphoreType.DMA((2,)),
            pltpu.SemaphoreType.DMA((2,)),
        ],
        compiler_params=_COMPILER_PARAMS,
        name="mixer",
    )(x, xs, h0r_t, h0i_t, pool0_t, w3(w_in), w3(w_out), *prep_in, *seq_params)


MLP_STAGE_ROWS = 64
MLP_STAGE_SLOTS = 4


def _mlp_kernel(h_ref, hs_ref, g_pre_ref, g_post_ref, wup_hbm, wdown_hbm,
                y_hbm, ys_hbm, wup_bf, wdown_bf, stage_up, stage_down, ybuf, ys,
                w_sem, o_sem, y_sem, *, batch, steps):
    i = pl.program_id(0)
    n_chunks = pl.num_programs(0)
    slot = lax.rem(i, 2)

    def y_copy(b, chunk, sl):
        return pltpu.make_async_copy(
            ybuf.at[sl, :, b, :], y_hbm.at[b, pl.ds(chunk * steps, steps), :],
            o_sem.at[sl])

    @pl.when(i == 0)
    def _():
        _cast_weights(w_sem, MLP_STAGE_SLOTS,
                      [(wup_hbm, wup_bf, stage_up, 0, 0, MLP_STAGE_ROWS),
                       (wdown_hbm, wdown_bf, stage_down, 0, MLP_STAGE_SLOTS,
                        4 * MLP_STAGE_ROWS)])

    @pl.when(i >= 2)
    def _():
        for b in range(batch):
            y_copy(b, i - 2, slot).wait()

    y = _mlp_math(h_ref[...], g_pre_ref, g_post_ref, wup_bf, wdown_bf)
    ybuf[slot] = y.reshape(steps, batch, D_MODEL)
    for b in range(batch):
        y_copy(b, i, slot).start()

    @pl.when(i == n_chunks - 1)
    def _():
        ys[...] = _mlp_math(hs_ref[...], g_pre_ref, g_post_ref, wup_bf, wdown_bf)
        y_out = pltpu.make_async_copy(ys, ys_hbm.at[:, 0, :], y_sem.at[0])
        y_out.start()

        @pl.when(i >= 1)
        def _():
            for b in range(batch):
                y_copy(b, i - 1, 1 - slot).wait()

        for b in range(batch):
            y_copy(b, i, slot).wait()
        y_out.wait()


def _mlp(h, hs, g_pre, g_post, wup, wdown, *, batch, steps):
    block_rows = steps * batch
    n_rows = h.shape[0]
    batch_s = hs.shape[0]
    assert n_rows % block_rows == 0
    seq = n_rows // batch
    return pl.pallas_call(
        functools.partial(_mlp_kernel, batch=batch, steps=steps),
        out_shape=(jax.ShapeDtypeStruct((batch, seq, D_MODEL), F32),
                   jax.ShapeDtypeStruct((batch_s, 1, D_MODEL), F32)),
        grid=(n_rows // block_rows,),
        in_specs=[
            pl.BlockSpec((block_rows, D_MODEL), lambda i: (i, 0)),
            _VMEM_WHOLE, _VMEM_WHOLE, _VMEM_WHOLE, _HBM, _HBM,
        ],
        out_specs=(_HBM, _HBM),
        scratch_shapes=[
            pltpu.VMEM((D_MODEL, D_FF), BF16),
            pltpu.VMEM((D_FF, D_MODEL), BF16),
            pltpu.VMEM((MLP_STAGE_SLOTS, MLP_STAGE_ROWS, D_FF), F32),
            pltpu.VMEM((MLP_STAGE_SLOTS, 4 * MLP_STAGE_ROWS, D_MODEL), F32),
            pltpu.VMEM((2, steps, batch, D_MODEL), F32),
            pltpu.VMEM((batch_s, D_MODEL), F32),
            pltpu.SemaphoreType.DMA((2 * MLP_STAGE_SLOTS,)),
            pltpu.SemaphoreType.DMA((2,)),
            pltpu.SemaphoreType.DMA((1,)),
        ],
        compiler_params=_COMPILER_PARAMS,
        name="mlp",
    )(h, hs, g_pre, g_post, wup, wdown)


def kernel(x_prompt, x_sample, state_s5_re, state_s5_im, state_pool, norm_mix_pre, norm_mix_post, norm_mlp_pre, norm_mlp_post, w_in, s5_lambda_re, s5_lambda_im, s5_log_dt, s5_b_re, s5_b_im, s5_c_re, s5_c_im, s5_d, s5_w_glu, pool_w, pool_scale, w_out, w_mlp_up, w_mlp_down):
    bp, seq, _ = x_prompt.shape
    bs = x_sample.shape[0]

    prep_in = _s5_prep_inputs(
        s5_lambda_re, s5_lambda_im, s5_log_dt, s5_b_re, s5_b_im, s5_c_re, s5_c_im,
        s5_w_glu, pool_w)
    row = lambda v: v.reshape(1, -1)
    seq_params = [row(norm_mix_pre), row(norm_mix_post), row(s5_d), row(pool_scale)]

    st_in = lambda a: jnp.transpose(a, (1, 2, 0)).reshape(N_STATE, bs)
    st_out = lambda a: jnp.transpose(a.reshape(S5_GROUPS, S5_STATE, bs), (2, 0, 1))
    tbc = lambda a: jnp.transpose(a, (1, 0, 2))

    h1p, hpr, hpi, pool_p, h1s, hsr, hsi, pool_s = _mixer(
        x_prompt, x_sample, st_in(state_s5_re), st_in(state_s5_im), tbc(state_pool),
        w_in, w_out, prep_in, seq_params, steps=PROMPT_STEPS, start_pos_s=PAST_LEN)
    yp, y_sample = _mlp(h1p.reshape(seq * bp, D_MODEL), h1s, row(norm_mlp_pre),
                        row(norm_mlp_post), w_mlp_up, w_mlp_down, batch=bp,
                        steps=PROMPT_STEPS)

    return (yp, y_sample, hpr, hpi, tbc(pool_p),
            st_out(hsr), st_out(hsi), tbc(pool_s))
```

```python
import functools
import math

import jax
import jax.numpy as jnp
from jax import lax
from jax.experimental import pallas as pl
from jax.experimental.pallas import tpu as pltpu

F32 = jnp.float32
BF16 = jnp.bfloat16

D_MODEL = 1024
W_A = 512
W_B = 512
S5_H = 16
S5_GROUPS = 32
S5_STATE = 64
N_STATE = S5_GROUPS * S5_STATE
N_SLAB = 4
SLAB_GROUPS = S5_GROUPS // N_SLAB
SLAB_U = SLAB_GROUPS * S5_H
SLAB_S = SLAB_GROUPS * S5_STATE
POOL_WINDOWS = (2, 4, 8, 16)
POOL_CH = 128
POOL_HIST = 16
D_FF = 4096
FF_CHUNK = 1024
EPS = 1e-6
PAST_LEN = 16384
PROMPT_STEPS = 128
GELU_C = math.sqrt(2.0 / math.pi)

VMEM_LIMIT_BYTES = 63 * 1024 * 1024


def _rms_norm(x, g):
    ms = jnp.mean(x * x, axis=-1, keepdims=True)
    return x * lax.rsqrt(ms + EPS) * g


def _full_spec(shape):
    return pl.BlockSpec(shape, lambda *_: (0,) * len(shape))


def _s5_prep_kernel(lam_re_ref, lam_im_ref, log_dt_ref, b_re_ref, b_im_ref,
                    c_re_ref, c_im_ref, wglu_ref, poolw_ref,
                    a_re_ref, a_im_ref, bm_ref, cm_ref, glu_ref, poolbd_ref):
    lam_re = lam_re_ref[...]
    lam_im = lam_im_ref[...]
    eye = (lax.broadcasted_iota(jnp.int32, (S5_GROUPS, S5_GROUPS), 0)
           == lax.broadcasted_iota(jnp.int32, (S5_GROUPS, S5_GROUPS), 1))
    log_dt = jnp.sum(jnp.where(eye, log_dt_ref[...], 0.0), axis=1, keepdims=True)
    dt = jnp.exp(log_dt)
    mag = jnp.exp(lam_re * dt)
    ang = lam_im * dt
    a_re = mag * jnp.cos(ang)
    a_im = mag * jnp.sin(ang)
    lanes = lambda m: jnp.concatenate([m[g:g + 1, :] for g in range(S5_GROUPS)], axis=1)
    a_re_ref[...] = lanes(a_re)
    a_im_ref[...] = lanes(a_im)
    n_re = a_re - 1.0
    n_im = a_im
    den = lam_re * lam_re + lam_im * lam_im
    k_re = (n_re * lam_re + n_im * lam_im) / den
    k_im = (n_im * lam_re - n_re * lam_im) / den
    per_h = lambda m: jnp.broadcast_to(m[:, None, :], (S5_GROUPS, S5_H, S5_STATE)).reshape(
        S5_GROUPS * S5_H, S5_STATE)
    k_re = per_h(k_re)
    k_im = per_h(k_im)
    b_re = b_re_ref[...]
    b_im = b_im_ref[...]
    bb_re = k_re * b_re - k_im * b_im
    bb_im = k_re * b_im + k_im * b_re
    c_re = c_re_ref[...]
    c_im_neg = -c_im_ref[...]

    rows = lax.broadcasted_iota(jnp.int32, (SLAB_U, SLAB_S), 0) // S5_H
    cols = lax.broadcasted_iota(jnp.int32, (SLAB_U, SLAB_S), 1) // S5_STATE
    diag = rows == cols

    def block_diag(m, j):
        sl = m[j * SLAB_U:(j + 1) * SLAB_U, :]
        tiled = jnp.concatenate([sl] * SLAB_GROUPS, axis=1)
        return jnp.where(diag, tiled, 0.0)

    for j in range(N_SLAB):
        bm_ref[j, :, :SLAB_S] = block_diag(bb_re, j).astype(BF16)
        bm_ref[j, :, SLAB_S:] = block_diag(bb_im, j).astype(BF16)
        cm_ref[j, :SLAB_S, :] = block_diag(c_re, j).T.astype(BF16)
        cm_ref[j, SLAB_S:, :] = block_diag(c_im_neg, j).T.astype(BF16)

    half = W_A // 2
    n_g = half // S5_H
    w_hk_g = wglu_ref[...].reshape(S5_H * S5_H, S5_GROUPS)
    w_hk_g = jnp.concatenate(
        [w_hk_g, jnp.zeros((S5_H * S5_H, 128 - S5_GROUPS), F32)], axis=1)
    w_g_hk = w_hk_g.T
    col_g = lax.broadcasted_iota(jnp.int32, (n_g, half), 1) // S5_H
    row_g = lax.broadcasted_iota(jnp.int32, (n_g, half), 0)
    sel_r = lax.broadcasted_iota(jnp.int32, (half, n_g), 0)
    sel_g = lax.broadcasted_iota(jnp.int32, (half, n_g), 1)
    for t in range(2):
        w_t = w_g_hk[t * n_g:(t + 1) * n_g, :]
        acc = jnp.zeros((half, half), F32)
        for h in range(S5_H):
            blk = w_t[:, h * S5_H:(h + 1) * S5_H]
            tiled = jnp.concatenate([blk] * n_g, axis=1)
            w_h = jnp.where(row_g == col_g, tiled, 0.0).astype(BF16)
            place = (sel_r == sel_g * S5_H + h).astype(BF16)
            acc = acc + jnp.dot(place, w_h, preferred_element_type=F32)
        glu_ref[t] = acc.astype(BF16)

    zeros = jnp.zeros((POOL_CH, POOL_CH), BF16)
    for t in range(2):
        poolbd_ref[t, :POOL_CH, :POOL_CH] = poolw_ref[2 * t].astype(BF16)
        poolbd_ref[t, :POOL_CH, POOL_CH:] = zeros
        poolbd_ref[t, POOL_CH:, :POOL_CH] = zeros
        poolbd_ref[t, POOL_CH:, POOL_CH:] = poolw_ref[2 * t + 1].astype(BF16)


def _s5_prep_inputs(lam_re, lam_im, log_dt, b_re, b_im, c_re, c_im, w_glu, pool_w):
    gh_p = lambda b: jnp.transpose(b, (0, 2, 1)).reshape(S5_GROUPS * S5_H, S5_STATE)
    return (lam_re, lam_im, log_dt.reshape(1, S5_GROUPS), gh_p(b_re), gh_p(b_im),
            c_re.reshape(S5_GROUPS * S5_H, S5_STATE),
            c_im.reshape(S5_GROUPS * S5_H, S5_STATE),
            jnp.transpose(w_glu, (1, 2, 0)), pool_w)


def _s5_prep_scratch():
    return [
        pltpu.VMEM((1, N_STATE), F32),
        pltpu.VMEM((1, N_STATE), F32),
        pltpu.VMEM((N_SLAB, SLAB_U, 2 * SLAB_S), BF16),
        pltpu.VMEM((N_SLAB, 2 * SLAB_S, SLAB_U), BF16),
        pltpu.VMEM((2, W_A // 2, W_A // 2), BF16),
        pltpu.VMEM((2, 2 * POOL_CH, 2 * POOL_CH), BF16),
    ]


def _mixer_math(x, i, g_pre_ref, g_post_ref, win_ref, a_re_ref, a_im_ref, bm_ref,
                cm_ref, d_ref, glu_ref, poolw_ref, pscale_ref, wout_ref,
                slab_ref, hsbf_ref, hr_s, hi_s, hist_s, *, batch, steps, start_pos):
    rows = batch * steps
    xn = _rms_norm(x, g_pre_ref[...]).astype(BF16)
    w_in = win_ref[...].reshape(D_MODEL, D_MODEL)
    ua = jnp.dot(xn, w_in[:, :W_A], preferred_element_type=F32)
    ua_bf = ua.astype(BF16)

    pair = max(1, 16 // batch)

    def project_in(j):
        slab_ref[j % 2] = jnp.dot(ua_bf[:, j * SLAB_U:(j + 1) * SLAB_U], bm_ref[j],
                                  preferred_element_type=F32)

    def scan(j):
        st = slice(j * SLAB_S, (j + 1) * SLAB_S)
        sb = slab_ref.at[j % 2]
        hb = hsbf_ref.at[j % 2]
        ar = jnp.broadcast_to(a_re_ref[:, st], (batch, SLAB_S))
        ai = jnp.broadcast_to(a_im_ref[:, st], (batch, SLAB_S))
        hr = hr_s[:, st]
        hi = hi_s[:, st]
        for t0 in range(0, steps, pair):
            res, ims = [], []
            for t in range(t0, min(t0 + pair, steps)):
                rt = slice(t * batch, (t + 1) * batch)
                nr = ar * hr - ai * hi + sb[rt, :SLAB_S]
                ni = ar * hi + ai * hr + sb[rt, SLAB_S:]
                res.append(nr)
                ims.append(ni)
                hr, hi = nr, ni
            rg = slice(t0 * batch, (t0 + len(res)) * batch)
            hb[rg, :SLAB_S] = jnp.concatenate(res, axis=0).astype(BF16)
            hb[rg, SLAB_S:] = jnp.concatenate(ims, axis=0).astype(BF16)
        hr_s[:, st] = hr
        hi_s[:, st] = hi

    def project_out(j):
        return jnp.dot(hsbf_ref[j % 2], cm_ref[j],
                       preferred_element_type=F32)

    ys = []
    project_in(0)
    scan(0)
    ub = jnp.dot(xn, w_in[:, W_A:], preferred_element_type=F32)
    for j in range(1, N_SLAB):
        project_in(j)
        ys.append(project_out(j - 1))
        scan(j)
    ys.append(project_out(N_SLAB - 1))

    y = jnp.concatenate(ys, axis=1) + d_ref[...] * ua
    y = y * (0.5 * (1.0 + jnp.tanh(GELU_C * (y + 0.044715 * (y * y * y)))))
    y_bf = y.astype(BF16)
    half = W_A // 2
    gate = jnp.concatenate(
        [jnp.dot(y_bf[:, :half], glu_ref[0], preferred_element_type=F32),
         jnp.dot(y_bf[:, half:], glu_ref[1], preferred_element_type=F32)], axis=1)
    ya = y * (1.0 / (1.0 + jnp.exp(-gate)))

    hist = hist_s[...].reshape(POOL_HIST * batch, W_B)
    ext = jnp.concatenate([hist, ub], axis=0)
    n_ext = POOL_HIST * batch + rows
    hist_s[...] = ext[n_ext - POOL_HIST * batch:, :].reshape(POOL_HIST, batch, W_B)
    t_loc = lax.broadcasted_iota(jnp.int32, (rows, 1), 0) // batch
    pos1 = t_loc + (start_pos + 1) + i * steps
    pooled = []
    for gi, w in enumerate(POOL_WINDOWS):
        s = ext[:, gi * POOL_CH:(gi + 1) * POOL_CH]
        span = 1
        while span < w:
            n = s.shape[0]
            s = s[span * batch:, :] + s[:n - span * batch, :]
            span *= 2
        win = s[s.shape[0] - rows:, :]
        count = jnp.minimum(pos1, w).astype(F32)
        pooled.append(win / count - ub[:, gi * POOL_CH:(gi + 1) * POOL_CH])
    pooled = jnp.concatenate(pooled, axis=1).astype(BF16)
    halfb = W_B // 2
    yb = jnp.concatenate(
        [jnp.dot(pooled[:, :halfb], poolw_ref[0], preferred_element_type=F32),
         jnp.dot(pooled[:, halfb:], poolw_ref[1], preferred_element_type=F32)], axis=1)
    yb = yb * pscale_ref[...]

    ycat = jnp.concatenate([ya, yb], axis=1).astype(BF16)
    mix = jnp.dot(ycat, wout_ref[...].reshape(D_MODEL, D_MODEL),
                  preferred_element_type=F32)
    return x + _rms_norm(mix, g_post_ref[...])


def _mlp_math(h, g_pre_ref, g_post_ref, wup_ref, wdown_ref):
    hn = _rms_norm(h, g_pre_ref[...]).astype(BF16)
    ffs = []
    for j in range(D_FF // FF_CHUNK):
        sl = slice(j * FF_CHUNK, (j + 1) * FF_CHUNK)
        up = jnp.dot(hn, wup_ref[:, sl], preferred_element_type=F32)
        up = jnp.maximum(up, 0.0)
        ffs.append((up * up).astype(BF16))
    acc = jnp.dot(jnp.concatenate(ffs, axis=1), wdown_ref[...], preferred_element_type=F32)
    return h + _rms_norm(acc, g_post_ref[...])


def _mixer_scratch(batch, steps):
    return [
        pltpu.VMEM((2, batch * steps, 2 * SLAB_S), F32),
        pltpu.VMEM((2, batch * steps, 2 * SLAB_S), BF16),
        pltpu.VMEM((batch, N_STATE), F32),
        pltpu.VMEM((batch, N_STATE), F32),
        pltpu.VMEM((POOL_HIST, batch, W_B), F32),
    ]


_VMEM_WHOLE = pl.BlockSpec(memory_space=pltpu.VMEM)
_HBM = pl.BlockSpec(memory_space=pl.ANY)
_COMPILER_PARAMS = pltpu.CompilerParams(dimension_semantics=("arbitrary",),
                                        vmem_limit_bytes=VMEM_LIMIT_BYTES)


def _cast_weights(sem, n_slots, jobs):
    ahead = max(n_slots - 1, 1)

    def copy(job, c):
        w_hbm, _, stage_ref, first_slot, first_sem, row_chunk = job
        return pltpu.make_async_copy(w_hbm.at[pl.ds(c * row_chunk, row_chunk)],
                                     stage_ref.at[first_slot + c % n_slots],
                                     sem.at[first_sem + c % n_slots])

    counts = [job[0].shape[0] // job[5] for job in jobs]
    for job, n in zip(jobs, counts):
        for c in range(min(ahead, n)):
            copy(job, c).start()
    for c in range(max(counts)):
        for job, n in zip(jobs, counts):
            if c >= n:
                continue
            _, dst_ref, stage_ref, first_slot, _, row_chunk = job
            if c + ahead < n:
                copy(job, c + ahead).start()
            copy(job, c).wait()
            dst_ref[pl.ds(c * row_chunk, row_chunk)] = (
                stage_ref[first_slot + c % n_slots].astype(BF16))


N_PREP_IN = 9
N_SEQ_PARAM = 4


def _mixer_kernel(x_hbm, xs_hbm, h0r_ref, h0i_ref, pool0_hbm, win_hbm, wout_hbm,
                  *refs, batch, steps, batch_s, start_pos_s):
    prep_in = refs[:N_PREP_IN]
    n_in = N_PREP_IN + N_SEQ_PARAM
    g_pre, g_post, d, pscale = refs[N_PREP_IN:n_in]
    (h1_hbm, hr_out, hi_out, pool_out, h1s_out, hrs_out, his_out,
     pools_hbm) = refs[n_in:n_in + 8]
    (slab, hsbf, hr_s, hi_s, hist_s, hrs_s, his_s, hists_s, xs, win_bf, wout_bf, xbuf, hbuf,
     a_re, a_im, bm, cm, glu, poolbd, in_sem, out_sem, s_sem, w_sem) = refs[n_in + 8:]
    params = (g_pre, g_post, win_bf, a_re, a_im, bm, cm, d, glu, poolbd, pscale, wout_bf)
    i = pl.program_id(0)
    n_chunks = pl.num_programs(0)
    slot = lax.rem(i, 2)

    def x_copy(b, chunk, sl):
        return pltpu.make_async_copy(
            x_hbm.at[b, pl.ds(chunk * steps, steps), :], xbuf.at[sl, :, b, :],
            in_sem.at[sl])

    def h_copy(chunk, sl):
        return pltpu.make_async_copy(
            hbuf.at[sl], h1_hbm.at[pl.ds(chunk * steps, steps)], out_sem.at[sl])

    xs_copy = pltpu.make_async_copy(xs_hbm.at[:, 0, :], xs, s_sem.at[0])
    pool_in = pltpu.make_async_copy(pool0_hbm, hists_s.at[pl.ds(1, POOL_HIST - 1)],
                                    s_sem.at[1])
    pool_o = pltpu.make_async_copy(hists_s.at[pl.ds(1, POOL_HIST - 1)], pools_hbm,
                                   s_sem.at[1])

    @pl.when(i == 0)
    def _():
        for b in range(batch):
            x_copy(b, 0, 0).start(priority=b % 2)
        xs_copy.start()
        pool_in.start()
        hr_s[...] = jnp.zeros_like(hr_s)
        hi_s[...] = jnp.zeros_like(hi_s)
        hist_s[...] = jnp.zeros_like(hist_s)
        _s5_prep_kernel(*prep_in, a_re, a_im, bm, cm, glu, poolbd)
        _cast_weights(w_sem, 1, [(win_hbm, win_bf, hbuf, 0, 0, steps),
                                 (wout_hbm, wout_bf, hbuf, 1, 1, steps)])

    @pl.when(i + 1 < n_chunks)
    def _():
        for b in range(batch):
            x_copy(b, i + 1, 1 - slot).start(priority=b % 2)

    @pl.when(i >= 2)
    def _():
        h_copy(i - 2, slot).wait()

    for b in range(batch):
        x_copy(b, i, slot).wait()
    x = xbuf[slot].reshape(steps * batch, D_MODEL)
    h1 = _mixer_math(x, i, *params, slab, hsbf, hr_s, hi_s, hist_s,
                     batch=batch, steps=steps, start_pos=0)

    hbuf[slot] = h1.reshape(steps, batch, D_MODEL)
    h_copy(i, slot).start()

    @pl.when(i == n_chunks - 1)
    def _():
        for g in range(S5_GROUPS):
            hr_out[:, g, :] = hr_s[:, g * S5_STATE:(g + 1) * S5_STATE]
            hi_out[:, g, :] = hi_s[:, g * S5_STATE:(g + 1) * S5_STATE]
        pool_out[...] = hist_s[pl.ds(1, POOL_HIST - 1)]

        xs_copy.wait()
        pool_in.wait()
        hists_s[0] = jnp.zeros((batch_s, W_B), F32)
        hrs_s[...] = h0r_ref[...].T
        his_s[...] = h0i_ref[...].T
        h1s_out[...] = _mixer_math(
            xs[...], 0, *params, slab.at[:, pl.ds(0, batch_s), :],
            hsbf.at[:, pl.ds(0, batch_s), :], hrs_s, his_s, hists_s,
            batch=batch_s, steps=1, start_pos=start_pos_s)
        pool_o.start()
        hrs_out[...] = hrs_s[...].T
        his_out[...] = his_s[...].T
        @pl.when(i >= 1)
        def _():
            h_copy(i - 1, 1 - slot).wait()

        h_copy(i, slot).wait()
        pool_o.wait()


def _mixer(x, xs, h0r_t, h0i_t, pool0_t, w_in, w_out, prep_in, seq_params, *, steps,
           start_pos_s):
    batch, seq, _ = x.shape
    batch_s = xs.shape[0]
    assert len(prep_in) == N_PREP_IN and len(seq_params) == N_SEQ_PARAM
    assert seq % steps == 0 and batch_s <= steps * batch
    f32 = lambda *shape: jax.ShapeDtypeStruct(shape, F32)
    w3 = lambda w: w.reshape(D_MODEL // batch, batch, D_MODEL)
    return pl.pallas_call(
        functools.partial(_mixer_kernel, batch=batch, steps=steps, batch_s=batch_s,
                          start_pos_s=start_pos_s),
        out_shape=(f32(seq, batch, D_MODEL), f32(batch, S5_GROUPS, S5_STATE),
                   f32(batch, S5_GROUPS, S5_STATE), f32(POOL_HIST - 1, batch, W_B),
                   f32(batch_s, D_MODEL), f32(N_STATE, batch_s), f32(N_STATE, batch_s),
                   f32(*pool0_t.shape)),
        grid=(seq // steps,),
        in_specs=[_HBM, _HBM, _VMEM_WHOLE, _VMEM_WHOLE, _HBM, _HBM, _HBM]
        + [_VMEM_WHOLE] * (N_PREP_IN + N_SEQ_PARAM),
        out_specs=(_HBM, _full_spec((batch, S5_GROUPS, S5_STATE)),
                   _full_spec((batch, S5_GROUPS, S5_STATE)),
                   _full_spec((POOL_HIST - 1, batch, W_B)),
                   _full_spec((batch_s, D_MODEL)), _full_spec((N_STATE, batch_s)),
                   _full_spec((N_STATE, batch_s)), _HBM),
        scratch_shapes=_mixer_scratch(batch, steps) + [
            pltpu.VMEM((batch_s, N_STATE), F32),
            pltpu.VMEM((batch_s, N_STATE), F32),
            pltpu.VMEM((POOL_HIST, batch_s, W_B), F32),
            pltpu.VMEM((batch_s, D_MODEL), F32),
            pltpu.VMEM((D_MODEL // batch, batch, D_MODEL), BF16),
            pltpu.VMEM((D_MODEL // batch, batch, D_MODEL), BF16),
            pltpu.VMEM((2, steps, batch, D_MODEL), F32),
            pltpu.VMEM((2, steps, batch, D_MODEL), F32),
        ] + _s5_prep_scratch() + [
            pltpu.SemaphoreType.DMA((2,)),
            pltpu.SemaphoreType.DMA((2,)),
            pltpu.SemaphoreType.DMA((2,)),
            pltpu.SemaphoreType.DMA((2,)),
        ],
        compiler_params=_COMPILER_PARAMS,
        name="mixer",
    )(x, xs, h0r_t, h0i_t, pool0_t, w3(w_in), w3(w_out), *prep_in, *seq_params)


MLP_STAGE_ROWS = 64
MLP_STAGE_SLOTS = 4


def _mlp_kernel(h_ref, hs_ref, g_pre_ref, g_post_ref, wup_hbm, wdown_hbm,
                y_hbm, ys_hbm, wup_bf, wdown_bf, stage_up, stage_down, ybuf, ys,
                w_sem, o_sem, y_sem, *, batch, steps):
    i = pl.program_id(0)
    n_chunks = pl.num_programs(0)
    slot = lax.rem(i, 2)

    def y_copy(b, chunk, sl):
        return pltpu.make_async_copy(
            ybuf.at[sl, :, b, :], y_hbm.at[b, pl.ds(chunk * steps, steps), :],
            o_sem.at[sl])

    @pl.when(i == 0)
    def _():
        _cast_weights(w_sem, MLP_STAGE_SLOTS,
                      [(wup_hbm, wup_bf, stage_up, 0, 0, MLP_STAGE_ROWS),
                       (wdown_hbm, wdown_bf, stage_down, 0, MLP_STAGE_SLOTS,
                        4 * MLP_STAGE_ROWS)])

    @pl.when(i >= 2)
    def _():
        for b in range(batch):
            y_copy(b, i - 2, slot).wait()

    y = _mlp_math(h_ref[...], g_pre_ref, g_post_ref, wup_bf, wdown_bf)
    ybuf[slot] = y.reshape(steps, batch, D_MODEL)
    for b in range(batch):
        y_copy(b, i, slot).start(priority=b % 2)

    @pl.when(i == n_chunks - 1)
    def _():
        ys[...] = _mlp_math(hs_ref[...], g_pre_ref, g_post_ref, wup_bf, wdown_bf)
        y_out = pltpu.make_async_copy(ys, ys_hbm.at[:, 0, :], y_sem.at[0])
        y_out.start()

        @pl.when(i >= 1)
        def _():
            for b in range(batch):
                y_copy(b, i - 1, 1 - slot).wait()

        for b in range(batch):
            y_copy(b, i, slot).wait()
        y_out.wait()


def _mlp(h, hs, g_pre, g_post, wup, wdown, *, batch, steps):
    block_rows = steps * batch
    n_rows = h.shape[0]
    batch_s = hs.shape[0]
    assert n_rows % block_rows == 0
    seq = n_rows // batch
    return pl.pallas_call(
        functools.partial(_mlp_kernel, batch=batch, steps=steps),
        out_shape=(jax.ShapeDtypeStruct((batch, seq, D_MODEL), F32),
                   jax.ShapeDtypeStruct((batch_s, 1, D_MODEL), F32)),
        grid=(n_rows // block_rows,),
        in_specs=[
            pl.BlockSpec((block_rows, D_MODEL), lambda i: (i, 0)),
            _VMEM_WHOLE, _VMEM_WHOLE, _VMEM_WHOLE, _HBM, _HBM,
        ],
        out_specs=(_HBM, _HBM),
        scratch_shapes=[
            pltpu.VMEM((D_MODEL, D_FF), BF16),
            pltpu.VMEM((D_FF, D_MODEL), BF16),
            pltpu.VMEM((MLP_STAGE_SLOTS, MLP_STAGE_ROWS, D_FF), F32),
            pltpu.VMEM((MLP_STAGE_SLOTS, 4 * MLP_STAGE_ROWS, D_MODEL), F32),
            pltpu.VMEM((2, steps, batch, D_MODEL), F32),
            pltpu.VMEM((batch_s, D_MODEL), F32),
            pltpu.SemaphoreType.DMA((2 * MLP_STAGE_SLOTS,)),
            pltpu.SemaphoreType.DMA((2,)),
            pltpu.SemaphoreType.DMA((1,)),
        ],
        compiler_params=_COMPILER_PARAMS,
        name="mlp",
    )(h, hs, g_pre, g_post, wup, wdown)


def kernel(x_prompt, x_sample, state_s5_re, state_s5_im, state_pool, norm_mix_pre, norm_mix_post, norm_mlp_pre, norm_mlp_post, w_in, s5_lambda_re, s5_lambda_im, s5_log_dt, s5_b_re, s5_b_im, s5_c_re, s5_c_im, s5_d, s5_w_glu, pool_w, pool_scale, w_out, w_mlp_up, w_mlp_down):
    bp, seq, _ = x_prompt.shape
    bs = x_sample.shape[0]

    prep_in = _s5_prep_inputs(
        s5_lambda_re, s5_lambda_im, s5_log_dt, s5_b_re, s5_b_im, s5_c_re, s5_c_im,
        s5_w_glu, pool_w)
    row = lambda v: v.reshape(1, -1)
    seq_params = [row(norm_mix_pre), row(norm_mix_post), row(s5_d), row(pool_scale)]

    st_in = lambda a: jnp.transpose(a, (1, 2, 0)).reshape(N_STATE, bs)
    st_out = lambda a: jnp.transpose(a.reshape(S5_GROUPS, S5_STATE, bs), (2, 0, 1))
    tbc = lambda a: jnp.transpose(a, (1, 0, 2))

    h1p, hpr, hpi, pool_p, h1s, hsr, hsi, pool_s = _mixer(
        x_prompt, x_sample, st_in(state_s5_re), st_in(state_s5_im), tbc(state_pool),
        w_in, w_out, prep_in, seq_params, steps=PROMPT_STEPS, start_pos_s=PAST_LEN)
    yp, y_sample = _mlp(h1p.reshape(seq * bp, D_MODEL), h1s, row(norm_mlp_pre),
                        row(norm_mlp_post), w_mlp_up, w_mlp_down, batch=bp,
                        steps=PROMPT_STEPS)

    return (yp, y_sample, hpr, hpi, tbc(pool_p),
            st_out(hsr), st_out(hsi), tbc(pool_s))
```

```python
import functools
import math

import jax
import jax.numpy as jnp
from jax import lax
from jax.experimental import pallas as pl
from jax.experimental.pallas import tpu as pltpu

F32 = jnp.float32
BF16 = jnp.bfloat16

D_MODEL = 1024
W_A = 512
W_B = 512
S5_H = 16
S5_GROUPS = 32
S5_STATE = 64
N_STATE = S5_GROUPS * S5_STATE
N_SLAB = 4
SLAB_GROUPS = S5_GROUPS // N_SLAB
SLAB_U = SLAB_GROUPS * S5_H
SLAB_S = SLAB_GROUPS * S5_STATE
POOL_WINDOWS = (2, 4, 8, 16)
POOL_CH = 128
POOL_HIST = 16
D_FF = 4096
FF_CHUNK = 1024
EPS = 1e-6
PAST_LEN = 16384
PROMPT_STEPS = 128
GELU_C = math.sqrt(2.0 / math.pi)

VMEM_LIMIT_BYTES = 63 * 1024 * 1024


def _rms_norm(x, g):
    ms = jnp.mean(x * x, axis=-1, keepdims=True)
    return x * lax.rsqrt(ms + EPS) * g


def _full_spec(shape):
    return pl.BlockSpec(shape, lambda *_: (0,) * len(shape))


def _s5_prep_kernel(lam_re_ref, lam_im_ref, log_dt_ref, b_re_ref, b_im_ref,
                    c_re_ref, c_im_ref, wglu_ref, poolw_ref,
                    a_re_ref, a_im_ref, bm_ref, cm_ref, glu_ref, poolbd_ref):
    lam_re = lam_re_ref[...]
    lam_im = lam_im_ref[...]
    eye = (lax.broadcasted_iota(jnp.int32, (S5_GROUPS, S5_GROUPS), 0)
           == lax.broadcasted_iota(jnp.int32, (S5_GROUPS, S5_GROUPS), 1))
    log_dt = jnp.sum(jnp.where(eye, log_dt_ref[...], 0.0), axis=1, keepdims=True)
    dt = jnp.exp(log_dt)
    mag = jnp.exp(lam_re * dt)
    ang = lam_im * dt
    a_re = mag * jnp.cos(ang)
    a_im = mag * jnp.sin(ang)
    lanes = lambda m: jnp.concatenate([m[g:g + 1, :] for g in range(S5_GROUPS)], axis=1)
    a_re_ref[...] = lanes(a_re)
    a_im_ref[...] = lanes(a_im)
    n_re = a_re - 1.0
    n_im = a_im
    den = lam_re * lam_re + lam_im * lam_im
    k_re = (n_re * lam_re + n_im * lam_im) / den
    k_im = (n_im * lam_re - n_re * lam_im) / den
    per_h = lambda m: jnp.broadcast_to(m[:, None, :], (S5_GROUPS, S5_H, S5_STATE)).reshape(
        S5_GROUPS * S5_H, S5_STATE)
    k_re = per_h(k_re)
    k_im = per_h(k_im)
    b_re = b_re_ref[...]
    b_im = b_im_ref[...]
    bb_re = k_re * b_re - k_im * b_im
    bb_im = k_re * b_im + k_im * b_re
    c_re = c_re_ref[...]
    c_im_neg = -c_im_ref[...]

    rows = lax.broadcasted_iota(jnp.int32, (SLAB_U, SLAB_S), 0) // S5_H
    cols = lax.broadcasted_iota(jnp.int32, (SLAB_U, SLAB_S), 1) // S5_STATE
    diag = rows == cols

    def block_diag(m, j):
        sl = m[j * SLAB_U:(j + 1) * SLAB_U, :]
        tiled = jnp.concatenate([sl] * SLAB_GROUPS, axis=1)
        return jnp.where(diag, tiled, 0.0)

    for j in range(N_SLAB):
        bm_ref[j, :, :SLAB_S] = block_diag(bb_re, j).astype(BF16)
        bm_ref[j, :, SLAB_S:] = block_diag(bb_im, j).astype(BF16)
        cm_ref[j, :SLAB_S, :] = block_diag(c_re, j).T.astype(BF16)
        cm_ref[j, SLAB_S:, :] = block_diag(c_im_neg, j).T.astype(BF16)

    half = W_A // 2
    n_g = half // S5_H
    w_hk_g = wglu_ref[...].reshape(S5_H * S5_H, S5_GROUPS)
    w_hk_g = jnp.concatenate(
        [w_hk_g, jnp.zeros((S5_H * S5_H, 128 - S5_GROUPS), F32)], axis=1)
    w_g_hk = w_hk_g.T
    col_g = lax.broadcasted_iota(jnp.int32, (n_g, half), 1) // S5_H
    row_g = lax.broadcasted_iota(jnp.int32, (n_g, half), 0)
    sel_r = lax.broadcasted_iota(jnp.int32, (half, n_g), 0)
    sel_g = lax.broadcasted_iota(jnp.int32, (half, n_g), 1)
    for t in range(2):
        w_t = w_g_hk[t * n_g:(t + 1) * n_g, :]
        acc = jnp.zeros((half, half), F32)
        for h in range(S5_H):
            blk = w_t[:, h * S5_H:(h + 1) * S5_H]
            tiled = jnp.concatenate([blk] * n_g, axis=1)
            w_h = jnp.where(row_g == col_g, tiled, 0.0).astype(BF16)
            place = (sel_r == sel_g * S5_H + h).astype(BF16)
            acc = acc + jnp.dot(place, w_h, preferred_element_type=F32)
        glu_ref[t] = acc.astype(BF16)

    zeros = jnp.zeros((POOL_CH, POOL_CH), BF16)
    for t in range(2):
        poolbd_ref[t, :POOL_CH, :POOL_CH] = poolw_ref[2 * t].astype(BF16)
        poolbd_ref[t, :POOL_CH, POOL_CH:] = zeros
        poolbd_ref[t, POOL_CH:, :POOL_CH] = zeros
        poolbd_ref[t, POOL_CH:, POOL_CH:] = poolw_ref[2 * t + 1].astype(BF16)


def _s5_prep_inputs(lam_re, lam_im, log_dt, b_re, b_im, c_re, c_im, w_glu, pool_w):
    gh_p = lambda b: jnp.transpose(b, (0, 2, 1)).reshape(S5_GROUPS * S5_H, S5_STATE)
    return (lam_re, lam_im, log_dt.reshape(1, S5_GROUPS), gh_p(b_re), gh_p(b_im),
            c_re.reshape(S5_GROUPS * S5_H, S5_STATE),
            c_im.reshape(S5_GROUPS * S5_H, S5_STATE),
            jnp.transpose(w_glu, (1, 2, 0)), pool_w)


def _s5_prep_scratch():
    return [
        pltpu.VMEM((1, N_STATE), F32),
        pltpu.VMEM((1, N_STATE), F32),
        pltpu.VMEM((N_SLAB, SLAB_U, 2 * SLAB_S), BF16),
        pltpu.VMEM((N_SLAB, 2 * SLAB_S, SLAB_U), BF16),
        pltpu.VMEM((2, W_A // 2, W_A // 2), BF16),
        pltpu.VMEM((2, 2 * POOL_CH, 2 * POOL_CH), BF16),
    ]


def _mixer_math(x, i, g_pre_ref, g_post_ref, win_ref, a_re_ref, a_im_ref, bm_ref,
                cm_ref, d_ref, glu_ref, poolw_ref, pscale_ref, wout_ref,
                slab_ref, hsbf_ref, hr_s, hi_s, hist_s, *, batch, steps, start_pos):
    rows = batch * steps
    xn = _rms_norm(x, g_pre_ref[...]).astype(BF16)
    w_in = win_ref[...].reshape(D_MODEL, D_MODEL)
    ua = jnp.dot(xn, w_in[:, :W_A], preferred_element_type=F32)
    ua_bf = ua.astype(BF16)

    pair = max(1, 16 // batch)

    def project_in(j):
        slab_ref[j % 2] = jnp.dot(ua_bf[:, j * SLAB_U:(j + 1) * SLAB_U], bm_ref[j],
                                  preferred_element_type=F32)

    def scan(j):
        st = slice(j * SLAB_S, (j + 1) * SLAB_S)
        sb = slab_ref.at[j % 2]
        hb = hsbf_ref.at[j % 2]
        ar = jnp.broadcast_to(a_re_ref[:, st], (batch, SLAB_S))
        ai = jnp.broadcast_to(a_im_ref[:, st], (batch, SLAB_S))
        hr = hr_s[:, st]
        hi = hi_s[:, st]
        for t0 in range(0, steps, pair):
            res, ims = [], []
            for t in range(t0, min(t0 + pair, steps)):
                rt = slice(t * batch, (t + 1) * batch)
                nr = ar * hr - ai * hi + sb[rt, :SLAB_S]
                ni = ar * hi + ai * hr + sb[rt, SLAB_S:]
                res.append(nr)
                ims.append(ni)
                hr, hi = nr, ni
            rg = slice(t0 * batch, (t0 + len(res)) * batch)
            hb[rg, :SLAB_S] = jnp.concatenate(res, axis=0).astype(BF16)
            hb[rg, SLAB_S:] = jnp.concatenate(ims, axis=0).astype(BF16)
        hr_s[:, st] = hr
        hi_s[:, st] = hi

    def project_out(j):
        return jnp.dot(hsbf_ref[j % 2], cm_ref[j],
                       preferred_element_type=F32)

    ys = []
    project_in(0)
    scan(0)
    ub = jnp.dot(xn, w_in[:, W_A:], preferred_element_type=F32)
    for j in range(1, N_SLAB):
        project_in(j)
        ys.append(project_out(j - 1))
        scan(j)
    ys.append(project_out(N_SLAB - 1))

    y = jnp.concatenate(ys, axis=1) + d_ref[...] * ua
    y = y * (0.5 * (1.0 + jnp.tanh(GELU_C * (y + 0.044715 * (y * y * y)))))
    y_bf = y.astype(BF16)
    half = W_A // 2
    gate = jnp.concatenate(
        [jnp.dot(y_bf[:, :half], glu_ref[0], preferred_element_type=F32),
         jnp.dot(y_bf[:, half:], glu_ref[1], preferred_element_type=F32)], axis=1)
    ya = y * (1.0 / (1.0 + jnp.exp(-gate)))

    hist = hist_s[...].reshape(POOL_HIST * batch, W_B)
    ext = jnp.concatenate([hist, ub], axis=0)
    n_ext = POOL_HIST * batch + rows
    hist_s[...] = ext[n_ext - POOL_HIST * batch:, :].reshape(POOL_HIST, batch, W_B)
    t_loc = lax.broadcasted_iota(jnp.int32, (rows, 1), 0) // batch
    pos1 = t_loc + (start_pos + 1) + i * steps
    pooled = []
    for gi, w in enumerate(POOL_WINDOWS):
        s = ext[:, gi * POOL_CH:(gi + 1) * POOL_CH]
        span = 1
        while span < w:
            n = s.shape[0]
            s = s[span * batch:, :] + s[:n - span * batch, :]
            span *= 2
        win = s[s.shape[0] - rows:, :]
        count = jnp.minimum(pos1, w).astype(F32)
        pooled.append(win / count - ub[:, gi * POOL_CH:(gi + 1) * POOL_CH])
    pooled = jnp.concatenate(pooled, axis=1).astype(BF16)
    halfb = W_B // 2
    yb = jnp.concatenate(
        [jnp.dot(pooled[:, :halfb], poolw_ref[0], preferred_element_type=F32),
         jnp.dot(pooled[:, halfb:], poolw_ref[1], preferred_element_type=F32)], axis=1)
    yb = yb * pscale_ref[...]

    ycat = jnp.concatenate([ya, yb], axis=1).astype(BF16)
    mix = jnp.dot(ycat, wout_ref[...].reshape(D_MODEL, D_MODEL),
                  preferred_element_type=F32)
    return x + _rms_norm(mix, g_post_ref[...])


def _mlp_math(h, g_pre_ref, g_post_ref, wup_ref, wdown_ref):
    hn = _rms_norm(h, g_pre_ref[...]).astype(BF16)
    ffs = []
    for j in range(D_FF // FF_CHUNK):
        sl = slice(j * FF_CHUNK, (j + 1) * FF_CHUNK)
        up = jnp.dot(hn, wup_ref[:, sl], preferred_element_type=F32)
        up = jnp.maximum(up, 0.0)
        ffs.append((up * up).astype(BF16))
    acc = jnp.dot(jnp.concatenate(ffs, axis=1), wdown_ref[...], preferred_element_type=F32)
    return h + _rms_norm(acc, g_post_ref[...])


def _mixer_scratch(batch, steps):
    return [
        pltpu.VMEM((2, batch * steps, 2 * SLAB_S), F32),
        pltpu.VMEM((2, batch * steps, 2 * SLAB_S), BF16),
        pltpu.VMEM((batch, N_STATE), F32),
        pltpu.VMEM((batch, N_STATE), F32),
        pltpu.VMEM((POOL_HIST, batch, W_B), F32),
    ]


_VMEM_WHOLE = pl.BlockSpec(memory_space=pltpu.VMEM)
_HBM = pl.BlockSpec(memory_space=pl.ANY)
_COMPILER_PARAMS = pltpu.CompilerParams(dimension_semantics=("arbitrary",),
                                        vmem_limit_bytes=VMEM_LIMIT_BYTES)


def _cast_weights(sem, n_slots, jobs):
    ahead = max(n_slots - 1, 1)

    def copy(job, c):
        w_hbm, _, stage_ref, first_slot, first_sem, row_chunk = job
        return pltpu.make_async_copy(w_hbm.at[pl.ds(c * row_chunk, row_chunk)],
                                     stage_ref.at[first_slot + c % n_slots],
                                     sem.at[first_sem + c % n_slots])

    counts = [job[0].shape[0] // job[5] for job in jobs]
    for job, n in zip(jobs, counts):
        for c in range(min(ahead, n)):
            copy(job, c).start(priority=1)
    for c in range(max(counts)):
        for job, n in zip(jobs, counts):
            if c >= n:
                continue
            _, dst_ref, stage_ref, first_slot, _, row_chunk = job
            if c + ahead < n:
                copy(job, c + ahead).start(priority=1)
            copy(job, c).wait()
            dst_ref[pl.ds(c * row_chunk, row_chunk)] = (
                stage_ref[first_slot + c % n_slots].astype(BF16))


N_PREP_IN = 9
N_SEQ_PARAM = 4


def _mixer_kernel(x_hbm, xs_hbm, h0r_ref, h0i_ref, pool0_hbm, win_hbm, wout_hbm,
                  *refs, batch, steps, batch_s, start_pos_s):
    prep_in = refs[:N_PREP_IN]
    n_in = N_PREP_IN + N_SEQ_PARAM
    g_pre, g_post, d, pscale = refs[N_PREP_IN:n_in]
    (h1_hbm, hr_out, hi_out, pool_out, h1s_out, hrs_out, his_out,
     pools_hbm) = refs[n_in:n_in + 8]
    (slab, hsbf, hr_s, hi_s, hist_s, hrs_s, his_s, hists_s, xs, win_bf, wout_bf, xbuf, hbuf,
     a_re, a_im, bm, cm, glu, poolbd, in_sem, out_sem, s_sem, w_sem) = refs[n_in + 8:]
    params = (g_pre, g_post, win_bf, a_re, a_im, bm, cm, d, glu, poolbd, pscale, wout_bf)
    i = pl.program_id(0)
    n_chunks = pl.num_programs(0)
    slot = lax.rem(i, 2)

    def x_copy(b, chunk, sl):
        return pltpu.make_async_copy(
            x_hbm.at[b, pl.ds(chunk * steps, steps), :], xbuf.at[sl, :, b, :],
            in_sem.at[sl])

    def h_copy(chunk, sl):
        return pltpu.make_async_copy(
            hbuf.at[sl], h1_hbm.at[pl.ds(chunk * steps, steps)], out_sem.at[sl])

    xs_copy = pltpu.make_async_copy(xs_hbm.at[:, 0, :], xs, s_sem.at[0])
    pool_in = pltpu.make_async_copy(pool0_hbm, hists_s.at[pl.ds(1, POOL_HIST - 1)],
                                    s_sem.at[1])
    pool_o = pltpu.make_async_copy(hists_s.at[pl.ds(1, POOL_HIST - 1)], pools_hbm,
                                   s_sem.at[1])

    @pl.when(i == 0)
    def _():
        for b in range(batch):
            x_copy(b, 0, 0).start()
        xs_copy.start()
        pool_in.start()
        hr_s[...] = jnp.zeros_like(hr_s)
        hi_s[...] = jnp.zeros_like(hi_s)
        hist_s[...] = jnp.zeros_like(hist_s)
        _s5_prep_kernel(*prep_in, a_re, a_im, bm, cm, glu, poolbd)
        _cast_weights(w_sem, 1, [(win_hbm, win_bf, hbuf, 0, 0, steps),
                                 (wout_hbm, wout_bf, hbuf, 1, 1, steps)])

    @pl.when(i + 1 < n_chunks)
    def _():
        for b in range(batch):
            x_copy(b, i + 1, 1 - slot).start()

    @pl.when(i >= 2)
    def _():
        h_copy(i - 2, slot).wait()

    for b in range(batch):
        x_copy(b, i, slot).wait()
    x = xbuf[slot].reshape(steps * batch, D_MODEL)
    h1 = _mixer_math(x, i, *params, slab, hsbf, hr_s, hi_s, hist_s,
                     batch=batch, steps=steps, start_pos=0)

    hbuf[slot] = h1.reshape(steps, batch, D_MODEL)
    h_copy(i, slot).start()

    @pl.when(i == n_chunks - 1)
    def _():
        for g in range(S5_GROUPS):
            hr_out[:, g, :] = hr_s[:, g * S5_STATE:(g + 1) * S5_STATE]
            hi_out[:, g, :] = hi_s[:, g * S5_STATE:(g + 1) * S5_STATE]
        pool_out[...] = hist_s[pl.ds(1, POOL_HIST - 1)]

        xs_copy.wait()
        pool_in.wait()
        hists_s[0] = jnp.zeros((batch_s, W_B), F32)
        hrs_s[...] = h0r_ref[...].T
        his_s[...] = h0i_ref[...].T
        h1s_out[...] = _mixer_math(
            xs[...], 0, *params, slab.at[:, pl.ds(0, batch_s), :],
            hsbf.at[:, pl.ds(0, batch_s), :], hrs_s, his_s, hists_s,
            batch=batch_s, steps=1, start_pos=start_pos_s)
        pool_o.start()
        hrs_out[...] = hrs_s[...].T
        his_out[...] = his_s[...].T
        @pl.when(i >= 1)
        def _():
            h_copy(i - 1, 1 - slot).wait()

        h_copy(i, slot).wait()
        pool_o.wait()


def _mixer(x, xs, h0r_t, h0i_t, pool0_t, w_in, w_out, prep_in, seq_params, *, steps,
           start_pos_s):
    batch, seq, _ = x.shape
    batch_s = xs.shape[0]
    assert len(prep_in) == N_PREP_IN and len(seq_params) == N_SEQ_PARAM
    assert seq % steps == 0 and batch_s <= steps * batch
    f32 = lambda *shape: jax.ShapeDtypeStruct(shape, F32)
    w3 = lambda w: w.reshape(D_MODEL // batch, batch, D_MODEL)
    return pl.pallas_call(
        functools.partial(_mixer_kernel, batch=batch, steps=steps, batch_s=batch_s,
                          start_pos_s=start_pos_s),
        out_shape=(f32(seq, batch, D_MODEL), f32(batch, S5_GROUPS, S5_STATE),
                   f32(batch, S5_GROUPS, S5_STATE), f32(POOL_HIST - 1, batch, W_B),
                   f32(batch_s, D_MODEL), f32(N_STATE, batch_s), f32(N_STATE, batch_s),
                   f32(*pool0_t.shape)),
        grid=(seq // steps,),
        in_specs=[_HBM, _HBM, _VMEM_WHOLE, _VMEM_WHOLE, _HBM, _HBM, _HBM]
        + [_VMEM_WHOLE] * (N_PREP_IN + N_SEQ_PARAM),
        out_specs=(_HBM, _full_spec((batch, S5_GROUPS, S5_STATE)),
                   _full_spec((batch, S5_GROUPS, S5_STATE)),
                   _full_spec((POOL_HIST - 1, batch, W_B)),
                   _full_spec((batch_s, D_MODEL)), _full_spec((N_STATE, batch_s)),
                   _full_spec((N_STATE, batch_s)), _HBM),
        scratch_shapes=_mixer_scratch(batch, steps) + [
            pltpu.VMEM((batch_s, N_STATE), F32),
            pltpu.VMEM((batch_s, N_STATE), F32),
            pltpu.VMEM((POOL_HIST, batch_s, W_B), F32),
            pltpu.VMEM((batch_s, D_MODEL), F32),
            pltpu.VMEM((D_MODEL // batch, batch, D_MODEL), BF16),
            pltpu.VMEM((D_MODEL // batch, batch, D_MODEL), BF16),
            pltpu.VMEM((2, steps, batch, D_MODEL), F32),
            pltpu.VMEM((2, steps, batch, D_MODEL), F32),
        ] + _s5_prep_scratch() + [
            pltpu.SemaphoreType.DMA((2,)),
            pltpu.SemaphoreType.DMA((2,)),
            pltpu.SemaphoreType.DMA((2,)),
            pltpu.SemaphoreType.DMA((2,)),
        ],
        compiler_params=_COMPILER_PARAMS,
        name="mixer",
    )(x, xs, h0r_t, h0i_t, pool0_t, w3(w_in), w3(w_out), *prep_in, *seq_params)


MLP_STAGE_ROWS = 64
MLP_STAGE_SLOTS = 4


def _mlp_kernel(h_ref, hs_ref, g_pre_ref, g_post_ref, wup_hbm, wdown_hbm,
                y_hbm, ys_hbm, wup_bf, wdown_bf, stage_up, stage_down, ybuf, ys,
                w_sem, o_sem, y_sem, *, batch, steps):
    i = pl.program_id(0)
    n_chunks = pl.num_programs(0)
    slot = lax.rem(i, 2)

    def y_copy(b, chunk, sl):
        return pltpu.make_async_copy(
            ybuf.at[sl, :, b, :], y_hbm.at[b, pl.ds(chunk * steps, steps), :],
            o_sem.at[sl])

    @pl.when(i == 0)
    def _():
        _cast_weights(w_sem, MLP_STAGE_SLOTS,
                      [(wup_hbm, wup_bf, stage_up, 0, 0, MLP_STAGE_ROWS),
                       (wdown_hbm, wdown_bf, stage_down, 0, MLP_STAGE_SLOTS,
                        4 * MLP_STAGE_ROWS)])

    @pl.when(i >= 2)
    def _():
        for b in range(batch):
            y_copy(b, i - 2, slot).wait()

    y = _mlp_math(h_ref[...], g_pre_ref, g_post_ref, wup_bf, wdown_bf)
    ybuf[slot] = y.reshape(steps, batch, D_MODEL)
    for b in range(batch):
        y_copy(b, i, slot).start()

    @pl.when(i == n_chunks - 1)
    def _():
        ys[...] = _mlp_math(hs_ref[...], g_pre_ref, g_post_ref, wup_bf, wdown_bf)
        y_out = pltpu.make_async_copy(ys, ys_hbm.at[:, 0, :], y_sem.at[0])
        y_out.start()

        @pl.when(i >= 1)
        def _():
            for b in range(batch):
                y_copy(b, i - 1, 1 - slot).wait()

        for b in range(batch):
            y_copy(b, i, slot).wait()
        y_out.wait()


def _mlp(h, hs, g_pre, g_post, wup, wdown, *, batch, steps):
    block_rows = steps * batch
    n_rows = h.shape[0]
    batch_s = hs.shape[0]
    assert n_rows % block_rows == 0
    seq = n_rows // batch
    return pl.pallas_call(
        functools.partial(_mlp_kernel, batch=batch, steps=steps),
        out_shape=(jax.ShapeDtypeStruct((batch, seq, D_MODEL), F32),
                   jax.ShapeDtypeStruct((batch_s, 1, D_MODEL), F32)),
        grid=(n_rows // block_rows,),
        in_specs=[
            pl.BlockSpec((block_rows, D_MODEL), lambda i: (i, 0)),
            _VMEM_WHOLE, _VMEM_WHOLE, _VMEM_WHOLE, _HBM, _HBM,
        ],
        out_specs=(_HBM, _HBM),
        scratch_shapes=[
            pltpu.VMEM((D_MODEL, D_FF), BF16),
            pltpu.VMEM((D_FF, D_MODEL), BF16),
            pltpu.VMEM((MLP_STAGE_SLOTS, MLP_STAGE_ROWS, D_FF), F32),
            pltpu.VMEM((MLP_STAGE_SLOTS, 4 * MLP_STAGE_ROWS, D_MODEL), F32),
            pltpu.VMEM((2, steps, batch, D_MODEL), F32),
            pltpu.VMEM((batch_s, D_MODEL), F32),
            pltpu.SemaphoreType.DMA((2 * MLP_STAGE_SLOTS,)),
            pltpu.SemaphoreType.DMA((2,)),
            pltpu.SemaphoreType.DMA((1,)),
        ],
        compiler_params=_COMPILER_PARAMS,
        name="mlp",
    )(h, hs, g_pre, g_post, wup, wdown)


def kernel(x_prompt, x_sample, state_s5_re, state_s5_im, state_pool, norm_mix_pre, norm_mix_post, norm_mlp_pre, norm_mlp_post, w_in, s5_lambda_re, s5_lambda_im, s5_log_dt, s5_b_re, s5_b_im, s5_c_re, s5_c_im, s5_d, s5_w_glu, pool_w, pool_scale, w_out, w_mlp_up, w_mlp_down):
    bp, seq, _ = x_prompt.shape
    bs = x_sample.shape[0]

    prep_in = _s5_prep_inputs(
        s5_lambda_re, s5_lambda_im, s5_log_dt, s5_b_re, s5_b_im, s5_c_re, s5_c_im,
        s5_w_glu, pool_w)
    row = lambda v: v.reshape(1, -1)
    seq_params = [row(norm_mix_pre), row(norm_mix_post), row(s5_d), row(pool_scale)]

    st_in = lambda a: jnp.transpose(a, (1, 2, 0)).reshape(N_STATE, bs)
    st_out = lambda a: jnp.transpose(a.reshape(S5_GROUPS, S5_STATE, bs), (2, 0, 1))
    tbc = lambda a: jnp.transpose(a, (1, 0, 2))

    h1p, hpr, hpi, pool_p, h1s, hsr, hsi, pool_s = _mixer(
        x_prompt, x_sample, st_in(state_s5_re), st_in(state_s5_im), tbc(state_pool),
        w_in, w_out, prep_in, seq_params, steps=PROMPT_STEPS, start_pos_s=PAST_LEN)
    yp, y_sample = _mlp(h1p.reshape(seq * bp, D_MODEL), h1s, row(norm_mlp_pre),
                        row(norm_mlp_post), w_mlp_up, w_mlp_down, batch=bp,
                        steps=PROMPT_STEPS)

    return (yp, y_sample, hpr, hpi, tbc(pool_p),
            st_out(hsr), st_out(hsi), tbc(pool_s))
```
